```python
import jax, jax.numpy as jnp
from jax import lax
import numpy as np

D_MODEL = 2048
BATCH = 2
SEQ = 16384
DEPTH = 1

CHUNK = 64
Q_BLOCK = 128
EPS = 1e-6
FOX_HEADS = 16
FOX_HEAD_DIM = D_MODEL // FOX_HEADS
FOX_WIDTH = FOX_HEADS * FOX_HEAD_DIM
MLA_HEADS = 16
MLA_NOPE_DIM = 128
MLA_ROPE_DIM = 64
MLA_V_DIM = D_MODEL // MLA_HEADS
MLA_Q_RANK = 512
MLA_KV_RANK = 512
ROPE_THETA = 10000.0
N_MEM = 256
MEM_HEADS = 4
MEM_HEAD_DIM = D_MODEL // MEM_HEADS
N_GROUPS = 8
EXPERTS_PER_GROUP = 8
N_EXPERTS = N_GROUPS * EXPERTS_PER_GROUP
TOP_K_IN_GROUP = 2
EXPERT_FF = 1408
EXPERT_BLOCK = 128
IN_SIZES = (FOX_WIDTH, FOX_WIDTH, FOX_WIDTH, FOX_HEADS, MLA_Q_RANK, MLA_KV_RANK, MLA_ROPE_DIM, D_MODEL, D_MODEL)
IN_WIDTH = sum(IN_SIZES)

kernel_name = 'hybrid_fox_mla_hmoe_block'


def rms_norm(x, g):
    x32 = x.astype(jnp.float32)
    y = x32 * lax.rsqrt(jnp.mean(x32 * x32, axis=-1, keepdims=True) + EPS)
    return (y * g.astype(jnp.float32)).astype(x.dtype)


def rope(x, positions):
    half = MLA_ROPE_DIM // 2
    inv_freq = ROPE_THETA ** (-jnp.arange(half, dtype=jnp.float32) / half)
    ang = positions.astype(jnp.float32)[..., None] * inv_freq
    cos = jnp.cos(ang)[:, :, None, :]
    sin = jnp.sin(ang)[:, :, None, :]
    x32 = x.astype(jnp.float32)
    x1, x2 = x32[..., :half], x32[..., half:]
    return jnp.concatenate([x1 * cos - x2 * sin, x1 * sin + x2 * cos], axis=-1).astype(x.dtype)


def block_sweep_attention(q, k, v, scale, causal_unit, log_forget=None):
    B, S, H, _ = q.shape
    nb = S // Q_BLOCK
    k_unit = jnp.arange(S) // causal_unit
    qb = q.reshape(B, nb, Q_BLOCK, H, q.shape[-1]).swapaxes(0, 1)
    if log_forget is not None:
        c = jnp.cumsum(log_forget.astype(jnp.float32), axis=1).swapaxes(1, 2)
    else:
        c = None

    def body(args):
        i, qi = args
        qpos = i * Q_BLOCK + jnp.arange(Q_BLOCK)
        s = jnp.einsum('bqhd,bkhd->bhqk', qi, k, preferred_element_type=jnp.float32) * scale
        if c is not None:
            ci = lax.dynamic_slice_in_dim(c, i * Q_BLOCK, Q_BLOCK, axis=2)
            s = s + (ci[..., :, None] - c[..., None, :])
        allowed = k_unit[None, :] <= (qpos // causal_unit)[:, None]
        s = jnp.where(allowed, s, -jnp.inf)
        p = jax.nn.softmax(s, axis=-1)
        return jnp.einsum('bhqk,bkhd->bqhd', p.astype(v.dtype), v)

    out = lax.map(body, (jnp.arange(nb), qb))
    return out.swapaxes(0, 1).reshape(B, S, H, v.shape[-1])


def memory_cross_attention(n, mem, g_mem, w_q, w_kv, w_o):
    B, S, _ = n.shape
    q = (n @ w_q).reshape(B, S, MEM_HEADS, MEM_HEAD_DIM)
    m = rms_norm(mem, g_mem)
    mk, mv = jnp.split(m @ w_kv, 2, axis=-1)
    mk = mk.reshape(B, N_MEM, MEM_HEADS, MEM_HEAD_DIM)
    mv = mv.reshape(B, N_MEM, MEM_HEADS, MEM_HEAD_DIM)
    s = jnp.einsum('bshd,bmhd->bhsm', q, mk, preferred_element_type=jnp.float32) * (MEM_HEAD_DIM ** -0.5)
    p = jax.nn.softmax(s, axis=-1)
    o = jnp.einsum('bhsm,bmhd->bshd', p.astype(mv.dtype), mv).reshape(B, S, D_MODEL)
    return o @ w_o


def hierarchical_moe(z, w_gr, b_gr, w_er, b_er, w_gate, w_up, w_down):
    B, S, D = z.shape
    N = B * S
    zf = z.reshape(N, D)
    g_logits = (zf @ w_gr + b_gr).astype(jnp.float32)
    g_prob = jax.nn.softmax(g_logits, axis=-1)
    g_idx = jnp.argmax(g_logits, axis=-1).astype(jnp.int32)
    g_w = jnp.take_along_axis(g_prob, g_idx[:, None], axis=-1)
    e_logits = (zf @ w_er + b_er).astype(jnp.float32).reshape(N, N_GROUPS, EXPERTS_PER_GROUP)
    sel = jnp.broadcast_to(g_idx[:, None, None], (N, 1, EXPERTS_PER_GROUP))
    e_logits = jnp.take_along_axis(e_logits, sel, axis=1)[:, 0]
    e_prob = jax.nn.softmax(e_logits, axis=-1)
    top_p, top_i = lax.top_k(e_prob, TOP_K_IN_GROUP)
    wts = g_w * top_p / jnp.sum(top_p, axis=-1, keepdims=True)
    eid = g_idx[:, None] * EXPERTS_PER_GROUP + top_i.astype(jnp.int32)

    n_slots = N * TOP_K_IN_GROUP
    slot_e = eid.reshape(-1)
    slot_tok = jnp.repeat(jnp.arange(N, dtype=jnp.int32), TOP_K_IN_GROUP)
    slot_w = wts.reshape(-1)
    counts = jnp.zeros((N_EXPERTS,), jnp.int32).at[slot_e].add(1)
    padded = ((counts + EXPERT_BLOCK - 1) // EXPERT_BLOCK) * EXPERT_BLOCK
    pad_end = jnp.cumsum(padded)
    pad_start = pad_end - padded
    raw_start = jnp.cumsum(counts) - counts
    order = jnp.argsort(slot_e)
    se = slot_e[order]
    dest = pad_start[se] + (jnp.arange(n_slots, dtype=jnp.int32) - raw_start[se])
    P = n_slots + N_EXPERTS * EXPERT_BLOCK
    buf_tok = jnp.zeros((P,), jnp.int32).at[dest].set(slot_tok[order])
    buf_w = jnp.zeros((P,), zf.dtype).at[dest].set(slot_w[order].astype(zf.dtype))
    blk_starts = jnp.arange(P // EXPERT_BLOCK, dtype=jnp.int32) * EXPERT_BLOCK
    blk_e = jnp.minimum(jnp.searchsorted(pad_end, blk_starts, side='right'), N_EXPERTS - 1).astype(jnp.int32)

    def expert_block(args):
        e, tok, wt = args
        xb = zf[tok]
        hdn = jax.nn.silu(xb @ w_gate[e]) * (xb @ w_up[e])
        return (hdn @ w_down[e]) * wt[:, None]

    y = lax.map(expert_block, (blk_e, buf_tok.reshape(-1, EXPERT_BLOCK), buf_w.reshape(-1, EXPERT_BLOCK)))
    out = jnp.zeros((N, D), zf.dtype).at[buf_tok].add(y.reshape(P, D))
    return out.reshape(B, S, D)


def setup_inputs(seed: int = 0) -> dict:
    key = jax.random.key(seed)
    ks = jax.random.split(key, 32)
    L = DEPTH

    def nrm(k, shape, fan_in):
        return jax.random.normal(k, shape, jnp.float32) * (fan_in ** -0.5)

    def gain(k, shape):
        return 1.0 + 0.02 * jax.random.normal(k, shape, jnp.float32)

    x = jax.random.normal(ks[0], (BATCH, SEQ, D_MODEL), jnp.float32)
    mem = jax.random.normal(ks[1], (BATCH, N_MEM, D_MODEL), jnp.float32)
    offset = jax.random.randint(ks[2], (BATCH, 1), 0, 1024, dtype=jnp.int32) * CHUNK
    positions = (offset + jnp.arange(SEQ, dtype=jnp.int32)[None, :]).astype(jnp.int32)
    return {
        'x': x,
        'mem': mem,
        'positions': positions,
        'g_mix': gain(ks[3], (L, D_MODEL)),
        'w_in': nrm(ks[4], (L, D_MODEL, IN_WIDTH), D_MODEL),
        'b_forget': jax.random.uniform(ks[5], (L, FOX_HEADS), jnp.float32, 1.0, 4.0),
        'g_q_latent': gain(ks[6], (L, MLA_Q_RANK)),
        'g_kv_latent': gain(ks[7], (L, MLA_KV_RANK)),
        'w_q_up': nrm(ks[8], (L, MLA_Q_RANK, MLA_HEADS * (MLA_NOPE_DIM + MLA_ROPE_DIM)), MLA_Q_RANK),
        'w_k_up': nrm(ks[9], (L, MLA_KV_RANK, MLA_HEADS * MLA_NOPE_DIM), MLA_KV_RANK),
        'w_v_up': nrm(ks[10], (L, MLA_KV_RANK, MLA_HEADS * MLA_V_DIM), MLA_KV_RANK),
        'w_out': nrm(ks[11], (L, D_MODEL, D_MODEL), D_MODEL),
        'g_cross': gain(ks[12], (L, D_MODEL)),
        'g_mem': gain(ks[13], (L, D_MODEL)),
        'w_cross_q': nrm(ks[14], (L, D_MODEL, D_MODEL), D_MODEL),
        'w_cross_kv': nrm(ks[15], (L, D_MODEL, 2 * D_MODEL), D_MODEL),
        'w_cross_out': nrm(ks[16], (L, D_MODEL, D_MODEL), D_MODEL),
        'g_moe': gain(ks[17], (L, D_MODEL)),
        'w_group_router': nrm(ks[18], (L, D_MODEL, N_GROUPS), D_MODEL),
        'b_group_router': 0.01 * jax.random.normal(ks[19], (L, N_GROUPS), jnp.float32),
        'w_expert_router': nrm(ks[20], (L, D_MODEL, N_EXPERTS), D_MODEL),
        'b_expert_router': 0.01 * jax.random.normal(ks[21], (L, N_EXPERTS), jnp.float32),
        'w_exp_gate': nrm(ks[22], (L, N_EXPERTS, D_MODEL, EXPERT_FF), D_MODEL),
        'w_exp_up': nrm(ks[23], (L, N_EXPERTS, D_MODEL, EXPERT_FF), D_MODEL),
        'w_exp_down': nrm(ks[24], (L, N_EXPERTS, EXPERT_FF, D_MODEL), EXPERT_FF),
        'g_final': gain(ks[25], (D_MODEL,)),
    }


def reference(x, mem, positions, g_mix, w_in, b_forget, g_q_latent, g_kv_latent, w_q_up, w_k_up, w_v_up, w_out, g_cross, g_mem, w_cross_q, w_cross_kv, w_cross_out, g_moe, w_group_router, b_group_router, w_expert_router, b_expert_router, w_exp_gate, w_exp_up, w_exp_down, g_final):
    B, S, _ = x.shape
    split_at = [int(v) for v in np.cumsum(IN_SIZES)[:-1]]
    h = x
    for l in range(DEPTH):
        n = rms_norm(h, g_mix[l])
        proj = n @ w_in[l]
        qf, kf, vf, f_logit, c_q, c_kv, k_r, gate_a, gate_b = jnp.split(proj, split_at, axis=-1)
        qf = qf.reshape(B, S, FOX_HEADS, FOX_HEAD_DIM)
        kf = kf.reshape(B, S, FOX_HEADS, FOX_HEAD_DIM)
        vf = vf.reshape(B, S, FOX_HEADS, FOX_HEAD_DIM)
        log_f = jax.nn.log_sigmoid((f_logit + b_forget[l]).astype(jnp.float32))
        o_a = block_sweep_attention(qf, kf, vf, FOX_HEAD_DIM ** -0.5, 1, log_f).reshape(B, S, FOX_WIDTH)
        c_q = rms_norm(c_q, g_q_latent[l])
        q_m = (c_q @ w_q_up[l]).reshape(B, S, MLA_HEADS, MLA_NOPE_DIM + MLA_ROPE_DIM)
        q_nope, q_rope = q_m[..., :MLA_NOPE_DIM], rope(q_m[..., MLA_NOPE_DIM:], positions)
        c_kv = rms_norm(c_kv, g_kv_latent[l])
        k_nope = (c_kv @ w_k_up[l]).reshape(B, S, MLA_HEADS, MLA_NOPE_DIM)
        v_m = (c_kv @ w_v_up[l]).reshape(B, S, MLA_HEADS, MLA_V_DIM)
        k_rope = jnp.broadcast_to(rope(k_r[:, :, None, :], positions), (B, S, MLA_HEADS, MLA_ROPE_DIM))
        q_full = jnp.concatenate([q_nope, q_rope], axis=-1)
        k_full = jnp.concatenate([k_nope, k_rope], axis=-1)
        o_b = block_sweep_attention(q_full, k_full, v_m, (MLA_NOPE_DIM + MLA_ROPE_DIM) ** -0.5, CHUNK).reshape(B, S, MLA_HEADS * MLA_V_DIM)
        merged = jax.nn.sigmoid(gate_a) * o_a + jax.nn.sigmoid(gate_b) * o_b
        h = h + merged @ w_out[l]
        h = h + memory_cross_attention(rms_norm(h, g_cross[l]), mem, g_mem[l], w_cross_q[l], w_cross_kv[l], w_cross_out[l])
        h = h + hierarchical_moe(rms_norm(h, g_moe[l]), w_group_router[l], b_group_router[l], w_expert_router[l], b_expert_router[l], w_exp_gate[l], w_exp_up[l], w_exp_down[l])
    return rms_norm(h, g_final)
```

```python
import functools
import math

import jax
import jax.numpy as jnp
from jax import lax
from jax.experimental import pallas as pl
from jax.experimental.pallas import tpu as pltpu

F32 = jnp.float32
BF16 = jnp.bfloat16

LANE = 128
EPS = 1e-6
CHUNK = 64
FOX_HEADS = 16
MLA_HEADS = 16
MLA_NOPE = 128
MLA_ROPE = 64
MLA_RANK = 512
ROPE_THETA = 10000.0
MEM_HEADS = 4
N_GROUPS = 8
EXPERTS_PER_GROUP = 8
N_EXPERTS = N_GROUPS * EXPERTS_PER_GROUP
LOG2E = math.log2(math.e)
VMEM_LIMIT = 56 * 1024 * 1024
NEG_INF = float("-inf")


def _params(*sem):
    return pltpu.CompilerParams(dimension_semantics=sem, vmem_limit_bytes=VMEM_LIMIT)


def _rope_rotate(x, cos, sin_lo, sin_hi):
    return x * cos + pltpu.roll(x, 96, 1) * sin_lo + pltpu.roll(x, 32, 1) * sin_hi


def _norm_matmul_kernel(*refs, rope, blocked, nblk):
    if rope:
        x_ref, g_ref, w_ref, cos_ref, slo_ref, shi_ref, o_ref, xn_ref = refs
    else:
        x_ref, g_ref, w_ref, o_ref, xn_ref = refs

    @pl.when(pl.program_id(1) == 0)
    def _():
        x = x_ref[...].astype(F32)
        y = x * lax.rsqrt(jnp.mean(x * x, axis=-1, keepdims=True) + EPS)
        xn_ref[...] = (y * g_ref[...]).astype(BF16)

    acc = jnp.dot(xn_ref[...], w_ref[...], preferred_element_type=F32)
    if blocked:
        for c in range(nblk):
            blk = acc[:, c * LANE:(c + 1) * LANE]
            if rope and c % 2 == 1:
                blk = _rope_rotate(blk, cos_ref[...], slo_ref[...], shi_ref[...])
            o_ref[c] = blk.astype(o_ref.dtype)
    else:
        o_ref[...] = acc.astype(o_ref.dtype)


def _norm_matmul(x, x_col, k, g, w, *, blocked, out_dtype, tm, tn, rope_tabs=None, name=None):
    m = x.shape[0]
    nc = w.shape[1]
    tm = min(tm, m)
    tn = min(tn, nc)
    assert m % tm == 0 and nc % tn == 0 and tn % LANE == 0
    nblk = tn // LANE
    in_specs = [
        pl.BlockSpec((tm, k), lambda i, j: (i, x_col)),
        pl.BlockSpec((1, k), lambda i, j: (0, 0)),
        pl.BlockSpec((k, tn), lambda i, j: (0, j)),
    ]
    args = [x, g.reshape(1, k).astype(F32), w]
    if rope_tabs is not None:
        in_specs += [pl.BlockSpec((tm, LANE), lambda i, j: (i, 0))] * 3
        args += list(rope_tabs)
    if blocked:
        out_shape = jax.ShapeDtypeStruct((nc // LANE, m, LANE), out_dtype)
        out_spec = pl.BlockSpec((nblk, tm, LANE), lambda i, j: (j, i, 0))
    else:
        out_shape = jax.ShapeDtypeStruct((m, nc), out_dtype)
        out_spec = pl.BlockSpec((tm, tn), lambda i, j: (i, j))
    return pl.pallas_call(
        functools.partial(_norm_matmul_kernel, rope=rope_tabs is not None, blocked=blocked, nblk=nblk),
        grid=(m // tm, nc // tn),
        in_specs=in_specs,
        out_specs=out_spec,
        out_shape=out_shape,
        scratch_shapes=[pltpu.VMEM((tm, k), BF16)],
        compiler_params=_params("parallel", "arbitrary"),
        name=name,
    )(*args)


def _krope_kernel(x_ref, cos_ref, slo_ref, shi_ref, o_ref):
    o_ref[...] = _rope_rotate(x_ref[...], cos_ref[...], slo_ref[...], shi_ref[...]).astype(o_ref.dtype)


def _krope(small, col_blk, rope_tabs, tm):
    m = small.shape[0]
    tm = min(tm, m)
    return pl.pallas_call(
        _krope_kernel,
        grid=(m // tm,),
        in_specs=[pl.BlockSpec((tm, LANE), lambda i: (i, col_blk))] + [pl.BlockSpec((tm, LANE), lambda i: (i, 0))] * 3,
        out_specs=pl.BlockSpec((tm, LANE), lambda i: (i, 0)),
        out_shape=jax.ShapeDtypeStruct((m, LANE), BF16),
        compiler_params=_params("parallel"),
        name="k_rope",
    )(small, *rope_tabs)


def _forget_cumsum_kernel(f_ref, b_ref, o_ref, carry_ref, *, ts):
    @pl.when(pl.program_id(1) == 0)
    def _():
        carry_ref[...] = jnp.zeros_like(carry_ref)

    x = f_ref[...] + b_ref[...]
    log_f = jnp.minimum(x, 0.0) - jnp.log1p(jnp.exp(-jnp.abs(x)))
    row = lax.broadcasted_iota(jnp.int32, (ts, ts), 0)
    col = lax.broadcasted_iota(jnp.int32, (ts, ts), 1)
    upper = (row <= col).astype(F32)
    c = jnp.dot(log_f, upper, preferred_element_type=F32, precision=lax.Precision.HIGHEST) + carry_ref[:, 0:1]
    carry_ref[...] = jnp.broadcast_to(c[:, ts - 1:ts], carry_ref.shape)
    o_ref[...] = c * LOG2E


def _forget_cumsum(f_t, b_forget, ts):
    b, h, s = f_t.shape
    ts = min(ts, s)
    return pl.pallas_call(
        functools.partial(_forget_cumsum_kernel, ts=ts),
        grid=(b, s // ts),
        in_specs=[pl.BlockSpec((None, h, ts), lambda bi, j: (bi, 0, j)), pl.BlockSpec((h, 1), lambda bi, j: (0, 0))],
        out_specs=pl.BlockSpec((None, h, ts), lambda bi, j: (bi, 0, j)),
        out_shape=jax.ShapeDtypeStruct((b, h, s), F32),
        scratch_shapes=[pltpu.VMEM((h, LANE), F32)],
        compiler_params=_params("parallel", "arbitrary"),
        name="forget_cumsum",
    )(f_t, b_forget.reshape(h, 1).astype(F32))


def _softmax_step(s, v, m_scr, l_scr, acc_scr):
    m_prev = m_scr[...]
    m_new = jnp.maximum(m_prev, jnp.max(s, axis=1, keepdims=True))
    alpha = jnp.exp2(m_prev - m_new)
    p = jnp.exp2(s - m_new)
    l_scr[...] = alpha * l_scr[...] + jnp.sum(p, axis=1, keepdims=True)
    acc_scr[...] = alpha * acc_scr[...] + jnp.dot(p.astype(BF16), v, preferred_element_type=F32)
    m_scr[...] = m_new


def _init_softmax(m_scr, l_scr, acc_scr):
    m_scr[...] = jnp.full(m_scr.shape, NEG_INF, F32)
    l_scr[...] = jnp.zeros(l_scr.shape, F32)
    acc_scr[...] = jnp.zeros(acc_scr.shape, F32)


def _fox_kernel(q_ref, k_ref, v_ref, c_ref, o_ref, m_scr, l_scr, acc_scr, *, t):
    i = pl.program_id(2)
    q = q_ref[...]
    _init_softmax(m_scr, l_scr, acc_scr)
    c0 = c_ref[:, pl.ds(pl.multiple_of(i * t, t), LANE)][:, 0:1]

    def step(j, masked):
        off = pl.multiple_of(j * t, t)
        k = k_ref[pl.ds(off, t), :]
        v = v_ref[pl.ds(off, t), :]
        s = lax.dot_general(q, k, (((1,), (1,)), ((), ())), preferred_element_type=F32)
        s = s + (c0 - c_ref[:, pl.ds(off, t)])
        if masked:
            row = lax.broadcasted_iota(jnp.int32, (t, t), 0)
            col = lax.broadcasted_iota(jnp.int32, (t, t), 1)
            s = jnp.where(col <= row, s, NEG_INF)
        _softmax_step(s, v, m_scr, l_scr, acc_scr)

    def body(j, carry):
        step(j, False)
        return carry

    lax.fori_loop(0, i, body, 0)
    step(i, True)
    o_ref[...] = (acc_scr[...] / l_scr[...]).astype(o_ref.dtype)


def _fox_attention(qkv, c, b, s, t):
    h = FOX_HEADS
    t = min(t, s)
    nq = s // t
    return pl.pallas_call(
        functools.partial(_fox_kernel, t=t),
        grid=(b, h, nq),
        in_specs=[
            pl.BlockSpec((None, t, LANE), lambda bi, hi, i: (hi, bi * nq + i, 0)),
            pl.BlockSpec((None, s, LANE), lambda bi, hi, i: (h + hi, bi, 0)),
            pl.BlockSpec((None, s, LANE), lambda bi, hi, i: (2 * h + hi, bi, 0)),
            pl.BlockSpec((None, None, 1, s), lambda bi, hi, i: (bi, hi, 0, 0)),
        ],
        out_specs=pl.BlockSpec((None, t, LANE), lambda bi, hi, i: (hi, bi * nq + i, 0)),
        out_shape=jax.ShapeDtypeStruct((h, b * s, LANE), BF16),
        scratch_shapes=[pltpu.VMEM((t, 1), F32), pltpu.VMEM((t, 1), F32), pltpu.VMEM((t, LANE), F32)],
        compiler_params=_params("parallel", "parallel", "arbitrary"),
        name="fox_attention",
    )(qkv, qkv, qkv, c)


def _mla_kernel(qn_ref, qr_ref, kn_ref, kr_ref, v_ref, o_ref, m_scr, l_scr, acc_scr, *, t):
    i = pl.program_id(2)
    q = jnp.concatenate([qn_ref[...], qr_ref[...]], axis=1)
    _init_softmax(m_scr, l_scr, acc_scr)

    def step(j, masked):
        off = pl.multiple_of(j * t, t)
        k = jnp.concatenate([kn_ref[pl.ds(off, t), :], kr_ref[pl.ds(off, t), :]], axis=1)
        v = v_ref[pl.ds(off, t), :]
        s = lax.dot_general(q, k, (((1,), (1,)), ((), ())), preferred_element_type=F32)
        if masked:
            row = lax.broadcasted_iota(jnp.int32, (t, t), 0)
            col = lax.broadcasted_iota(jnp.int32, (t, t), 1)
            s = jnp.where((col // CHUNK) <= (row // CHUNK), s, NEG_INF)
        _softmax_step(s, v, m_scr, l_scr, acc_scr)

    def body(j, carry):
        step(j, False)
        return carry

    lax.fori_loop(0, i, body, 0)
    step(i, True)
    o_ref[...] = (acc_scr[...] / l_scr[...]).astype(o_ref.dtype)


def _mla_attention(qm, kv, k_rope, b, s, t):
    h = MLA_HEADS
    t = min(t, s)
    assert t % CHUNK == 0
    nq = s // t
    return pl.pallas_call(
        functools.partial(_mla_kernel, t=t),
        grid=(b, h, nq),
        in_specs=[
            pl.BlockSpec((None, t, LANE), lambda bi, hi, i: (2 * hi, bi * nq + i, 0)),
            pl.BlockSpec((None, t, LANE), lambda bi, hi, i: (2 * hi + 1, bi * nq + i, 0)),
            pl.BlockSpec((None, s, LANE), lambda bi, hi, i: (hi, bi, 0)),
            pl.BlockSpec((s, LANE), lambda bi, hi, i: (bi, 0)),
            pl.BlockSpec((None, s, LANE), lambda bi, hi, i: (h + hi, bi, 0)),
        ],
        out_specs=pl.BlockSpec((None, t, LANE), lambda bi, hi, i: (hi, bi * nq + i, 0)),
        out_shape=jax.ShapeDtypeStruct((h, b * s, LANE), BF16),
        scratch_shapes=[pltpu.VMEM((t, 1), F32), pltpu.VMEM((t, 1), F32), pltpu.VMEM((t, LANE), F32)],
        compiler_params=_params("parallel", "parallel", "arbitrary"),
        name="mla_attention",
    )(qm, qm, kv, k_rope, kv)


def _proj_residual_kernel(*refs, merge, nb):
    if merge:
        oa_ref, ob_ref, ga_ref, gb_ref, w_ref, res_ref, o_ref, a_scr = refs
    else:
        oa_ref, w_ref, res_ref, o_ref, a_scr = refs

    @pl.when(pl.program_id(1) == 0)
    def _():
        for c in range(nb):
            if merge:
                a = (jax.nn.sigmoid(ga_ref[c].astype(F32)) * oa_ref[c].astype(F32)
                     + jax.nn.sigmoid(gb_ref[c].astype(F32)) * ob_ref[c].astype(F32))
            else:
                a = oa_ref[c]
            a_scr[:, c * LANE:(c + 1) * LANE] = a.astype(BF16)

    o_ref[...] = res_ref[...] + jnp.dot(a_scr[...], w_ref[...], preferred_element_type=F32)


def _proj_residual(o_a, w, res, *, tm, tn, merge=None, name=None):
    nb, m, _ = o_a.shape
    k = nb * LANE
    nc = w.shape[1]
    tm = min(tm, m)
    tn = min(tn, nc)
    head_spec = pl.BlockSpec((nb, tm, LANE), lambda i, j: (0, i, 0))
    in_specs = [head_spec]
    args = [o_a]
    if merge is not None:
        o_b, gates, ga_grp, gb_grp = merge
        in_specs += [head_spec,
                     pl.BlockSpec((nb, tm, LANE), lambda i, j: (ga_grp, i, 0)),
                     pl.BlockSpec((nb, tm, LANE), lambda i, j: (gb_grp, i, 0))]
        args += [o_b, gates, gates]
    in_specs += [pl.BlockSpec((k, tn), lambda i, j: (0, j)), pl.BlockSpec((tm, tn), lambda i, j: (i, j))]
    args += [w, res]
    return pl.pallas_call(
        functools.partial(_proj_residual_kernel, merge=merge is not None, nb=nb),
        grid=(m // tm, nc // tn),
        in_specs=in_specs,
        out_specs=pl.BlockSpec((tm, tn), lambda i, j: (i, j)),
        out_shape=jax.ShapeDtypeStruct((m, nc), F32),
        scratch_shapes=[pltpu.VMEM((tm, k), BF16)],
        compiler_params=_params("parallel", "arbitrary"),
        name=name,
    )(*args)


def _cross_kernel(q_ref, mk_ref, mv_ref, o_ref, *, bph):
    for hh in range(MEM_HEADS):
        blocks = range(hh * bph, (hh + 1) * bph)
        q = jnp.concatenate([q_ref[c] for c in blocks], axis=1)
        k = jnp.concatenate([mk_ref[c] for c in blocks], axis=1)
        v = jnp.concatenate([mv_ref[c] for c in blocks], axis=1)
        s = lax.dot_general(q, k, (((1,), (1,)), ((), ())), preferred_element_type=F32)
        p = jnp.exp2(s - jnp.max(s, axis=1, keepdims=True))
        p = p / jnp.sum(p, axis=1, keepdims=True)
        o = jnp.dot(p.astype(BF16), v, preferred_element_type=F32)
        for ci, c in enumerate(blocks):
            o_ref[c] = o[:, ci * LANE:(ci + 1) * LANE].astype(o_ref.dtype)


def _cross_attention(q, mkv, b, s, n_mem, tm):
    nb, n, _ = q.shape
    tm = min(tm, s)
    per_b = s // tm
    return pl.pallas_call(
        functools.partial(_cross_kernel, bph=nb // MEM_HEADS),
        grid=(n // tm,),
        in_specs=[
            pl.BlockSpec((nb, tm, LANE), lambda i: (0, i, 0)),
            pl.BlockSpec((nb, n_mem, LANE), lambda i: (0, i // per_b, 0)),
            pl.BlockSpec((nb, n_mem, LANE), lambda i: (1, i // per_b, 0)),
        ],
        out_specs=pl.BlockSpec((nb, tm, LANE), lambda i: (0, i, 0)),
        out_shape=jax.ShapeDtypeStruct((nb, n, LANE), BF16),
        compiler_params=_params("parallel"),
        name="cross_attention",
    )(q, mkv, mkv)


def _router_kernel(h_ref, g_ref, w_ref, b_ref, z_ref, info_ref):
    x = h_ref[...]
    z = x * lax.rsqrt(jnp.mean(x * x, axis=-1, keepdims=True) + EPS) * g_ref[...]
    z_ref[...] = z
    logits = jnp.dot(z, w_ref[...], preferred_element_type=F32, precision=lax.Precision.HIGHEST) + b_ref[...]
    lane = lax.broadcasted_iota(jnp.int32, logits.shape, 1).astype(F32)
    big = float(LANE)

    def first_where(mask):
        return jnp.min(jnp.where(mask, lane, big), axis=1, keepdims=True)

    gmask = lane < N_GROUPS
    gl = jnp.where(gmask, logits, NEG_INF)
    gmax = jnp.max(gl, axis=1, keepdims=True)
    gsum = jnp.sum(jnp.exp(gl - gmax), axis=1, keepdims=True)
    g_idx = first_where(gl == gmax)
    g_w = 1.0 / gsum
    lo = N_GROUPS + g_idx * EXPERTS_PER_GROUP
    emask = (lane >= lo) & (lane < lo + EXPERTS_PER_GROUP)
    el = jnp.where(emask, logits, NEG_INF)
    emax = jnp.max(el, axis=1, keepdims=True)
    eexp = jnp.exp(el - emax)
    prob = jnp.where(emask, eexp / jnp.sum(eexp, axis=1, keepdims=True), -1.0)
    p1 = jnp.max(prob, axis=1, keepdims=True)
    i1 = first_where(prob == p1)
    prob2 = jnp.where(lane == i1, -1.0, prob)
    p2 = jnp.max(prob2, axis=1, keepdims=True)
    i2 = first_where(prob2 == p2)
    denom = p1 + p2
    w1 = g_w * p1 / denom
    w2 = g_w * p2 / denom
    info = jnp.where(lane == 0, i1 - N_GROUPS,
                     jnp.where(lane == 1, i2 - N_GROUPS, jnp.where(lane == 2, w1, jnp.where(lane == 3, w2, 0.0))))
    info_ref[...] = info


def _router(h, g, w_r, b_r, tm):
    n, d = h.shape
    tm = min(tm, n)
    return pl.pallas_call(
        _router_kernel,
        grid=(n // tm,),
        in_specs=[pl.BlockSpec((tm, d), lambda i: (i, 0)), pl.BlockSpec((1, d), lambda i: (0, 0)),
                  pl.BlockSpec((d, LANE), lambda i: (0, 0)), pl.BlockSpec((1, LANE), lambda i: (0, 0))],
        out_specs=[pl.BlockSpec((tm, d), lambda i: (i, 0)), pl.BlockSpec((tm, LANE), lambda i: (i, 0))],
        out_shape=[jax.ShapeDtypeStruct((n, d), F32), jax.ShapeDtypeStruct((n, LANE), F32)],
        compiler_params=_params("parallel"),
        name="moe_router",
    )(h, g.reshape(1, d).astype(F32), w_r, b_r)


def _row_copy(src_hbm, row, dst, dst_row, sem):
    return pltpu.make_async_copy(src_hbm.at[pl.ds(row, 1)], dst.at[pl.ds(dst_row, 1)], sem)


def _gather_rows(idx_ref, idx_off, src_hbm, dst, sem, n_rows):
    def issue(r, carry):
        _row_copy(src_hbm, idx_ref[0, idx_off + r], dst, r, sem).start()
        return carry

    lax.fori_loop(0, n_rows, issue, 0)

    def wait(r, carry):
        _row_copy(src_hbm, 0, dst, r, sem).wait()
        return carry

    lax.fori_loop(0, n_rows, wait, 0)


def _expert_kernel(blk_e_ref, nvalid_ref, tok_ref, z_hbm, wg_ref, wu_ref, wd_ref, y_ref, xbuf, sem, *, tb):
    i = pl.program_id(0)

    @pl.when(i < nvalid_ref[0])
    def _():
        _gather_rows(tok_ref, 0, z_hbm, xbuf, sem, tb)
        x = xbuf[...].astype(BF16)
        gate = jnp.dot(x, wg_ref[...], preferred_element_type=F32)
        up = jnp.dot(x, wu_ref[...], preferred_element_type=F32)
        hidden = (gate * jax.nn.sigmoid(gate) * up).astype(BF16)
        y_ref[...] = jnp.dot(hidden, wd_ref[...], preferred_element_type=F32)

    @pl.when(i >= nvalid_ref[0])
    def _():
        y_ref[...] = jnp.zeros_like(y_ref)


def _expert_blocks(blk_e, nvalid, tok, z, wg, wu, wd, tb):
    nblk = tok.shape[0]
    d = z.shape[1]
    ff = wg.shape[2]
    grid_spec = pltpu.PrefetchScalarGridSpec(
        num_scalar_prefetch=2,
        grid=(nblk,),
        in_specs=[
            pl.BlockSpec((None, 1, tb), lambda i, be, nv: (i, 0, 0), memory_space=pltpu.SMEM),
            pl.BlockSpec(memory_space=pl.ANY),
            pl.BlockSpec((None, d, ff), lambda i, be, nv: (be[i], 0, 0)),
            pl.BlockSpec((None, d, ff), lambda i, be, nv: (be[i], 0, 0)),
            pl.BlockSpec((None, ff, d), lambda i, be, nv: (be[i], 0, 0)),
        ],
        out_specs=pl.BlockSpec((tb, d), lambda i, be, nv: (i, 0)),
        scratch_shapes=[pltpu.VMEM((tb, d), F32), pltpu.SemaphoreType.DMA(())],
    )
    return pl.pallas_call(
        functools.partial(_expert_kernel, tb=tb),
        grid_spec=grid_spec,
        out_shape=jax.ShapeDtypeStruct((nblk * tb, d), F32),
        compiler_params=_params("arbitrary"),
        name="moe_experts",
    )(blk_e, nvalid, tok, z, wg, wu, wd)


def _combine_kernel(dest_ref, y_hbm, h_ref, info_ref, g_ref, o_ref, ybuf, sem, *, tm, final_norm):
    _gather_rows(dest_ref, 0, y_hbm, ybuf.at[0], sem.at[0], tm)
    _gather_rows(dest_ref, tm, y_hbm, ybuf.at[1], sem.at[1], tm)
    info = info_ref[...]
    moe = ybuf[0] * info[:, 2:3] + ybuf[1] * info[:, 3:4]
    hx = h_ref[...] + moe
    if final_norm:
        hx = hx * lax.rsqrt(jnp.mean(hx * hx, axis=-1, keepdims=True) + EPS) * g_ref[...]
    o_ref[...] = hx


def _combine(dest, y, h, info, g_final, tm, final_norm):
    n, d = h.shape
    return pl.pallas_call(
        functools.partial(_combine_kernel, tm=tm, final_norm=final_norm),
        grid=(n // tm,),
        in_specs=[
            pl.BlockSpec((None, 1, 2 * tm), lambda i: (i, 0, 0), memory_space=pltpu.SMEM),
            pl.BlockSpec(memory_space=pl.ANY),
            pl.BlockSpec((tm, d), lambda i: (i, 0)),
            pl.BlockSpec((tm, LANE), lambda i: (i, 0)),
            pl.BlockSpec((1, d), lambda i: (0, 0)),
        ],
        out_specs=pl.BlockSpec((tm, d), lambda i: (i, 0)),
        out_shape=jax.ShapeDtypeStruct((n, d), F32),
        scratch_shapes=[pltpu.VMEM((2, tm, d), F32), pltpu.SemaphoreType.DMA((2,))],
        compiler_params=_params("arbitrary"),
        name="moe_combine_final_norm",
    )(dest, y, h, info, g_final.reshape(1, d).astype(F32))


def _dispatch_plan(eid, n, tb):
    n_slots = 2 * n
    nblk = n_slots // tb + N_EXPERTS
    slot_e = eid.reshape(-1)
    onehot = (slot_e[:, None] == jnp.arange(N_EXPERTS, dtype=jnp.int32)[None, :]).astype(jnp.int32)
    csum = jnp.cumsum(onehot, axis=0)
    rank = jnp.take_along_axis(csum, slot_e[:, None], axis=1)[:, 0] - 1
    counts = csum[-1]
    blocks_per_e = (counts + tb - 1) // tb
    blk_end = jnp.cumsum(blocks_per_e)
    blk_start = blk_end - blocks_per_e
    dest = blk_start[slot_e] * tb + rank
    slot_tok = jnp.arange(n_slots, dtype=jnp.int32) // 2
    tok = jnp.zeros((nblk * tb,), jnp.int32).at[dest].set(slot_tok)
    blk_e = jnp.minimum(jnp.searchsorted(blk_end, jnp.arange(nblk, dtype=jnp.int32), side="right"),
                        N_EXPERTS - 1).astype(jnp.int32)
    nvalid = blk_end[-1:].astype(jnp.int32)
    return blk_e, nvalid, tok.reshape(nblk, 1, tb), dest.reshape(n, 2)


def _rope_tables(positions):
    half = MLA_ROPE // 2
    inv_freq = ROPE_THETA ** (-jnp.arange(half, dtype=F32) / half)
    ang = positions.astype(F32).reshape(-1, 1) * inv_freq[None, :]
    cos, sin = jnp.cos(ang), jnp.sin(ang)
    zero = jnp.zeros_like(cos)
    cos_t = jnp.concatenate([cos, cos, zero, zero], axis=1)
    sin_lo = jnp.concatenate([-sin, zero, zero, zero], axis=1)
    sin_hi = jnp.concatenate([zero, sin, zero, zero], axis=1)
    return cos_t, sin_lo, sin_hi


def kernel(x, mem, positions, g_mix, w_in, b_forget, g_q_latent, g_kv_latent, w_q_up, w_k_up, w_v_up, w_out, g_cross, g_mem, w_cross_q, w_cross_kv, w_cross_out, g_moe, w_group_router, b_group_router, w_expert_router, b_expert_router, w_exp_gate, w_exp_up, w_exp_down, g_final):
    b, s, d = x.shape
    n = b * s
    n_mem = mem.shape[1]
    depth = w_in.shape[0]
    fw = FOX_HEADS * LANE
    h = x.reshape(n, d)
    rope_tabs = _rope_tables(positions)

    for l in range(depth):
        wi = w_in[l]
        fox_scale = (LANE ** -0.5) * LOG2E
        o_f = 3 * fw
        o_cq = o_f + FOX_HEADS
        o_kr = o_cq + 2 * MLA_RANK
        o_g = o_kr + MLA_ROPE
        w_big = jnp.concatenate([wi[:, :fw] * fox_scale, wi[:, fw:3 * fw], wi[:, o_g:]], axis=1).astype(BF16)
        w_small = jnp.concatenate(
            [wi[:, o_cq:o_kr], wi[:, o_kr:o_g], wi[:, o_f:o_cq],
             jnp.zeros((d, LANE - MLA_ROPE - FOX_HEADS), F32)], axis=1).astype(BF16)
        mla_scale = ((MLA_NOPE + MLA_ROPE) ** -0.5) * LOG2E
        wq = (w_q_up[l] * mla_scale).reshape(MLA_RANK, MLA_HEADS, MLA_NOPE + MLA_ROPE)
        wq = jnp.pad(wq, ((0, 0), (0, 0), (0, 2 * LANE - MLA_NOPE - MLA_ROPE))).reshape(MLA_RANK, -1).astype(BF16)
        wkv = jnp.concatenate([w_k_up[l], w_v_up[l]], axis=1).astype(BF16)

        big = _norm_matmul(h, 0, d, g_mix[l], w_big, blocked=True, out_dtype=BF16, tm=512, tn=512, name="in_proj")
        small = _norm_matmul(h, 0, d, g_mix[l], w_small, blocked=False, out_dtype=F32, tm=512, tn=384,
                             name="in_proj_small")
        f_t = small[:, 2 * MLA_RANK + MLA_ROPE:2 * MLA_RANK + MLA_ROPE + FOX_HEADS]
        f_t = f_t.reshape(b, s, FOX_HEADS).transpose(0, 2, 1)
        c = _forget_cumsum(f_t, b_forget[l], 512).reshape(b, FOX_HEADS, 1, s)
        o_a = _fox_attention(big, c, b, s, 512)
        qm = _norm_matmul(small, 0, MLA_RANK, g_q_latent[l], wq, blocked=True, out_dtype=BF16, tm=512, tn=512,
                          rope_tabs=rope_tabs, name="mla_q_up")
        kv = _norm_matmul(small, 1, MLA_RANK, g_kv_latent[l], wkv, blocked=True, out_dtype=BF16, tm=512, tn=512,
                          name="mla_kv_up")
        k_rope = _krope(small, 2 * MLA_RANK // LANE, rope_tabs, 512)
        o_b = _mla_attention(qm, kv, k_rope, b, s, 512)
        h = _proj_residual(o_a, w_out[l].astype(BF16), h, tm=512, tn=512, merge=(o_b, big, 3, 4), name="out_proj")
        cross_scale = ((d // MEM_HEADS) ** -0.5) * LOG2E
        q_c = _norm_matmul(h, 0, d, g_cross[l], (w_cross_q[l] * cross_scale).astype(BF16), blocked=True,
                           out_dtype=BF16, tm=512, tn=512, name="cross_q")
        mkv = _norm_matmul(mem.reshape(b * n_mem, d), 0, d, g_mem[l], w_cross_kv[l].astype(BF16), blocked=True,
                           out_dtype=BF16, tm=512, tn=512, name="cross_mem_kv")
        o_c = _cross_attention(q_c, mkv, b, s, n_mem, 512)
        h = _proj_residual(o_c, w_cross_out[l].astype(BF16), h, tm=512, tn=512, name="cross_out")
        w_r = jnp.concatenate([w_group_router[l], w_expert_router[l],
                               jnp.zeros((d, LANE - N_GROUPS - N_EXPERTS), F32)], axis=1)
        b_r = jnp.concatenate([b_group_router[l], b_expert_router[l],
                               jnp.zeros((LANE - N_GROUPS - N_EXPERTS,), F32)]).reshape(1, LANE)
        z, info = _router(h, g_moe[l], w_r, b_r, 256)
        tb = 256
        eid = info[:, 0:2].astype(jnp.int32)
        blk_e, nvalid, tok, dest = _dispatch_plan(eid, n, tb)
        y = _expert_blocks(blk_e, nvalid, tok, z, w_exp_gate[l].astype(BF16), w_exp_up[l].astype(BF16),
                           w_exp_down[l].astype(BF16), tb)
        tm_c = min(256, n)
        dest_blk = dest.reshape(n // tm_c, tm_c, 2).transpose(0, 2, 1).reshape(n // tm_c, 1, 2 * tm_c)
        h = _combine(dest_blk, y, h, info, g_final, tm_c, final_norm=(l + 1 == depth))
    return h.reshape(b, s, d)
```

```python
import functools
import math

import jax
import jax.numpy as jnp
from jax import lax
from jax.experimental import pallas as pl
from jax.experimental.pallas import tpu as pltpu

F32 = jnp.float32
BF16 = jnp.bfloat16

LANE = 128
EPS = 1e-6
CHUNK = 64
FOX_HEADS = 16
MLA_HEADS = 16
MLA_NOPE = 128
MLA_ROPE = 64
MLA_RANK = 512
ROPE_THETA = 10000.0
MEM_HEADS = 4
N_GROUPS = 8
EXPERTS_PER_GROUP = 8
N_EXPERTS = N_GROUPS * EXPERTS_PER_GROUP
LOG2E = math.log2(math.e)
VMEM_LIMIT = 56 * 1024 * 1024
NEG_INF = float("-inf")
ATTN_TILE = 1024


def _params(*sem):
    return pltpu.CompilerParams(dimension_semantics=sem, vmem_limit_bytes=VMEM_LIMIT)


def _rope_rotate(x, cos, sin_lo, sin_hi):
    return x * cos + pltpu.roll(x, 96, 1) * sin_lo + pltpu.roll(x, 32, 1) * sin_hi


def _norm_matmul_kernel(*refs, rope, blocked, nblk):
    if rope:
        x_ref, g_ref, w_ref, cos_ref, slo_ref, shi_ref, o_ref, xn_ref = refs
    else:
        x_ref, g_ref, w_ref, o_ref, xn_ref = refs

    @pl.when(pl.program_id(1) == 0)
    def _():
        x = x_ref[...].astype(F32)
        y = x * lax.rsqrt(jnp.mean(x * x, axis=-1, keepdims=True) + EPS)
        xn_ref[...] = (y * g_ref[...]).astype(BF16)

    acc = jnp.dot(xn_ref[...], w_ref[...], preferred_element_type=F32)
    if blocked:
        for c in range(nblk):
            blk = acc[:, c * LANE:(c + 1) * LANE]
            if rope and c % 2 == 1:
                blk = _rope_rotate(blk, cos_ref[...], slo_ref[...], shi_ref[...])
            o_ref[c] = blk.astype(o_ref.dtype)
    else:
        o_ref[...] = acc.astype(o_ref.dtype)


def _norm_matmul(x, x_col, k, g, w, *, blocked, out_dtype, tm, tn, rope_tabs=None, name=None):
    m = x.shape[0]
    nc = w.shape[1]
    tm = min(tm, m)
    tn = min(tn, nc)
    assert m % tm == 0 and nc % tn == 0 and tn % LANE == 0
    nblk = tn // LANE
    in_specs = [
        pl.BlockSpec((tm, k), lambda i, j: (i, x_col)),
        pl.BlockSpec((1, k), lambda i, j: (0, 0)),
        pl.BlockSpec((k, tn), lambda i, j: (0, j)),
    ]
    args = [x, g.reshape(1, k).astype(F32), w]
    if rope_tabs is not None:
        in_specs += [pl.BlockSpec((tm, LANE), lambda i, j: (i, 0))] * 3
        args += list(rope_tabs)
    if blocked:
        out_shape = jax.ShapeDtypeStruct((nc // LANE, m, LANE), out_dtype)
        out_spec = pl.BlockSpec((nblk, tm, LANE), lambda i, j: (j, i, 0))
    else:
        out_shape = jax.ShapeDtypeStruct((m, nc), out_dtype)
        out_spec = pl.BlockSpec((tm, tn), lambda i, j: (i, j))
    return pl.pallas_call(
        functools.partial(_norm_matmul_kernel, rope=rope_tabs is not None, blocked=blocked, nblk=nblk),
        grid=(m // tm, nc // tn),
        in_specs=in_specs,
        out_specs=out_spec,
        out_shape=out_shape,
        scratch_shapes=[pltpu.VMEM((tm, k), BF16)],
        compiler_params=_params("parallel", "arbitrary"),
        name=name,
    )(*args)


def _krope_kernel(x_ref, cos_ref, slo_ref, shi_ref, o_ref):
    o_ref[...] = _rope_rotate(x_ref[...], cos_ref[...], slo_ref[...], shi_ref[...]).astype(o_ref.dtype)


def _krope(small, col_blk, rope_tabs, tm):
    m = small.shape[0]
    tm = min(tm, m)
    return pl.pallas_call(
        _krope_kernel,
        grid=(m // tm,),
        in_specs=[pl.BlockSpec((tm, LANE), lambda i: (i, col_blk))] + [pl.BlockSpec((tm, LANE), lambda i: (i, 0))] * 3,
        out_specs=pl.BlockSpec((tm, LANE), lambda i: (i, 0)),
        out_shape=jax.ShapeDtypeStruct((m, LANE), BF16),
        compiler_params=_params("parallel"),
        name="k_rope",
    )(small, *rope_tabs)


N_SPLIT = 3


def _forget_bias_kernel(f_ref, b_ref, o_ref, carry_ref, *, ts, lane0, heads):
    @pl.when(pl.program_id(1) == 0)
    def _():
        carry_ref[...] = jnp.zeros_like(carry_ref)

    x = f_ref[...] + b_ref[...]
    log_f = jnp.minimum(x, 0.0) - jnp.log1p(jnp.exp(-jnp.abs(x)))
    row = lax.broadcasted_iota(jnp.int32, (ts, ts), 0)
    col = lax.broadcasted_iota(jnp.int32, (ts, ts), 1)
    lower = (col <= row).astype(F32)
    c = jnp.dot(lower, log_f, preferred_element_type=F32, precision=lax.Precision.HIGHEST) + carry_ref[0:1, :]
    carry_ref[...] = jnp.broadcast_to(c[ts - 1:ts, :], carry_ref.shape)
    rest = -(c * LOG2E)
    pieces = []
    for _ in range(N_SPLIT):
        piece = rest.astype(BF16).astype(F32)
        pieces.append(piece)
        rest = rest - piece
    lane = lax.broadcasted_iota(jnp.int32, (ts, LANE), 1)
    for hd in range(heads):
        out = jnp.zeros((ts, LANE), F32)
        for pi, piece in enumerate(pieces):
            out = jnp.where(lane == pi, piece[:, lane0 + hd:lane0 + hd + 1], out)
        o_ref[hd] = out.astype(o_ref.dtype)


def _forget_bias(small, col_blk, lane0, b_forget, b, s, ts):
    heads = b_forget.shape[0]
    ts = min(ts, s)
    per_b = s // ts
    b_row = jnp.zeros((1, LANE), F32).at[0, lane0:lane0 + heads].set(b_forget.astype(F32))
    return pl.pallas_call(
        functools.partial(_forget_bias_kernel, ts=ts, lane0=lane0, heads=heads),
        grid=(b, per_b),
        in_specs=[pl.BlockSpec((ts, LANE), lambda bi, j: (bi * per_b + j, col_blk)),
                  pl.BlockSpec((1, LANE), lambda bi, j: (0, 0))],
        out_specs=pl.BlockSpec((heads, ts, LANE), lambda bi, j: (0, bi * per_b + j, 0)),
        out_shape=jax.ShapeDtypeStruct((heads, b * s, LANE), BF16),
        scratch_shapes=[pltpu.VMEM((8, LANE), F32)],
        compiler_params=_params("parallel", "arbitrary"),
        name="forget_bias",
    )(small, b_row)


def _norm_matmul_t_kernel(x_ref, g_ref, w_ref, o_ref, xn_ref):
    @pl.when(pl.program_id(1) == 0)
    def _():
        x = x_ref[...].astype(F32)
        y = x * lax.rsqrt(jnp.mean(x * x, axis=-1, keepdims=True) + EPS)
        xn_ref[...] = (y * g_ref[...]).astype(BF16)

    o_ref[...] = lax.dot_general(w_ref[...], xn_ref[...], (((1,), (1,)), ((), ())),
                                 preferred_element_type=F32).astype(o_ref.dtype)


def _norm_matmul_t(x, x_col, k, g, w_t, *, tm, tr, name=None):
    m = x.shape[0]
    r = w_t.shape[0]
    tm = min(tm, m)
    tr = min(tr, r)
    return pl.pallas_call(
        _norm_matmul_t_kernel,
        grid=(m // tm, r // tr),
        in_specs=[pl.BlockSpec((tm, k), lambda i, j: (i, x_col)), pl.BlockSpec((1, k), lambda i, j: (0, 0)),
                  pl.BlockSpec((tr, k), lambda i, j: (j, 0))],
        out_specs=pl.BlockSpec((tr, tm), lambda i, j: (j, i)),
        out_shape=jax.ShapeDtypeStruct((r, m), BF16),
        scratch_shapes=[pltpu.VMEM((tm, k), BF16)],
        compiler_params=_params("parallel", "arbitrary"),
        name=name,
    )(x, g.reshape(1, k).astype(F32), w_t)


def _attn_kernel(*refs, t, unit, q_ones):
    if q_ones:
        qa_ref, ka_ref, kb_ref, vt_ref, o_ref = refs[:5]
        lane = lax.broadcasted_iota(jnp.int32, (t, LANE), 1)
        q_b = jnp.where(lane < N_SPLIT, 1.0, 0.0).astype(BF16)
    else:
        qa_ref, qb_ref, ka_ref, kb_ref, vt_ref, o_ref = refs[:6]
        q_b = qb_ref[...]
    m_scr, l_scr, acc_scr, s_ref, p_ref, a_ref, bmax_ref = refs[-7:]
    i = pl.program_id(2)
    q = jnp.concatenate([qa_ref[...], q_b], axis=1)
    m_scr[...] = jnp.full(m_scr.shape, NEG_INF, F32)
    l_scr[...] = jnp.zeros(l_scr.shape, F32)
    acc_scr[...] = jnp.zeros(acc_scr.shape, F32)
    p_ref[...] = jnp.zeros(p_ref.shape, BF16)
    a_ref[...] = jnp.ones(a_ref.shape, F32)

    def logits(j):
        off = pl.multiple_of(j * t, t)
        k = jnp.concatenate([ka_ref[pl.ds(off, t), :], kb_ref[pl.ds(off, t), :]], axis=1)
        s = lax.dot_general(k, q, (((1,), (1,)), ((), ())), preferred_element_type=F32)
        s_ref[...] = s
        bmax_ref[...] = jnp.max(s, axis=0, keepdims=True)

    def softmax(masked):
        s = s_ref[...]
        if masked:
            key = lax.broadcasted_iota(jnp.int32, (t, t), 0)
            qry = lax.broadcasted_iota(jnp.int32, (t, t), 1)
            s = jnp.where((key // unit) <= (qry // unit), s, NEG_INF)
            block_max = jnp.max(s, axis=0, keepdims=True)
        else:
            block_max = bmax_ref[...]
        m_prev = m_scr[...]
        m_new = jnp.maximum(m_prev, block_max)
        alpha = jnp.exp2(m_prev - m_new)
        p = jnp.exp2(s - m_new)
        l_scr[...] = alpha * l_scr[...] + jnp.sum(p, axis=0, keepdims=True)
        m_scr[...] = m_new
        p_ref[...] = p.astype(BF16)
        a_ref[...] = alpha

    def values(j):
        off = pl.multiple_of(jnp.maximum(j, 0) * t, t)
        pv = jnp.dot(vt_ref[:, pl.ds(off, t)], p_ref[...], preferred_element_type=F32)
        acc_scr[...] = a_ref[...] * acc_scr[...] + pv

    def body(tau, carry):
        values(tau - 2)
        softmax(False)
        logits(tau)
        return carry

    logits(0)
    lax.fori_loop(1, i + 1, body, 0)
    values(i - 1)
    softmax(True)
    values(i)
    o_ref[...] = jnp.transpose(acc_scr[...] / l_scr[...]).astype(o_ref.dtype)


def _causal_attention(q_a, qa_blk, q_b, qb_blk, k_a, ka_blk, k_b, kb_blk, v_t, heads, b, s, t, unit, name):
    t = min(t, s)
    assert t % unit == 0
    nq = s // t
    q_spec = lambda blk: pl.BlockSpec((None, t, LANE), lambda bi, hi, i: (blk(hi), bi * nq + i, 0))
    k_spec = lambda blk: pl.BlockSpec((None, s, LANE), lambda bi, hi, i: (blk(hi), bi, 0))
    in_specs = [q_spec(qa_blk)]
    args = [q_a]
    if q_b is not None:
        in_specs.append(q_spec(qb_blk))
        args.append(q_b)
    in_specs += [k_spec(ka_blk), k_spec(kb_blk), pl.BlockSpec((LANE, s), lambda bi, hi, i: (hi, bi))]
    args += [k_a, k_b, v_t]
    return pl.pallas_call(
        functools.partial(_attn_kernel, t=t, unit=unit, q_ones=q_b is None),
        grid=(b, heads, nq),
        in_specs=in_specs,
        out_specs=pl.BlockSpec((None, t, LANE), lambda bi, hi, i: (hi, bi * nq + i, 0)),
        out_shape=jax.ShapeDtypeStruct((heads, b * s, LANE), BF16),
        scratch_shapes=[pltpu.VMEM((1, t), F32), pltpu.VMEM((1, t), F32), pltpu.VMEM((LANE, t), F32),
                        pltpu.VMEM((t, t), F32), pltpu.VMEM((t, t), BF16), pltpu.VMEM((1, t), F32),
                        pltpu.VMEM((1, t), F32)],
        compiler_params=_params("parallel", "parallel", "arbitrary"),
        name=name,
    )(*args)


def _proj_residual_kernel(*refs, merge, nb):
    if merge:
        oa_ref, ob_ref, ga_ref, gb_ref, w_ref, res_ref, o_ref, a_scr = refs
    else:
        oa_ref, w_ref, res_ref, o_ref, a_scr = refs

    @pl.when(pl.program_id(1) == 0)
    def _():
        for c in range(nb):
            if merge:
                a = (jax.nn.sigmoid(ga_ref[c].astype(F32)) * oa_ref[c].astype(F32)
                     + jax.nn.sigmoid(gb_ref[c].astype(F32)) * ob_ref[c].astype(F32))
            else:
                a = oa_ref[c]
            a_scr[:, c * LANE:(c + 1) * LANE] = a.astype(BF16)

    o_ref[...] = res_ref[...] + jnp.dot(a_scr[...], w_ref[...], preferred_element_type=F32)


def _proj_residual(o_a, w, res, *, tm, tn, merge=None, name=None):
    nb, m, _ = o_a.shape
    k = nb * LANE
    nc = w.shape[1]
    tm = min(tm, m)
    tn = min(tn, nc)
    head_spec = pl.BlockSpec((nb, tm, LANE), lambda i, j: (0, i, 0))
    in_specs = [head_spec]
    args = [o_a]
    if merge is not None:
        o_b, gates, ga_grp, gb_grp = merge
        in_specs += [head_spec,
                     pl.BlockSpec((nb, tm, LANE), lambda i, j: (ga_grp, i, 0)),
                     pl.BlockSpec((nb, tm, LANE), lambda i, j: (gb_grp, i, 0))]
        args += [o_b, gates, gates]
    in_specs += [pl.BlockSpec((k, tn), lambda i, j: (0, j)), pl.BlockSpec((tm, tn), lambda i, j: (i, j))]
    args += [w, res]
    return pl.pallas_call(
        functools.partial(_proj_residual_kernel, merge=merge is not None, nb=nb),
        grid=(m // tm, nc // tn),
        in_specs=in_specs,
        out_specs=pl.BlockSpec((tm, tn), lambda i, j: (i, j)),
        out_shape=jax.ShapeDtypeStruct((m, nc), F32),
        scratch_shapes=[pltpu.VMEM((tm, k), BF16)],
        compiler_params=_params("parallel", "arbitrary"),
        name=name,
    )(*args)


def _cross_kernel(q_ref, mk_ref, mv_ref, o_ref, *, bph):
    for hh in range(MEM_HEADS):
        blocks = range(hh * bph, (hh + 1) * bph)
        q = jnp.concatenate([q_ref[c] for c in blocks], axis=1)
        k = jnp.concatenate([mk_ref[c] for c in blocks], axis=1)
        v = jnp.concatenate([mv_ref[c] for c in blocks], axis=1)
        s = lax.dot_general(q, k, (((1,), (1,)), ((), ())), preferred_element_type=F32)
        p = jnp.exp2(s - jnp.max(s, axis=1, keepdims=True))
        p = p / jnp.sum(p, axis=1, keepdims=True)
        o = jnp.dot(p.astype(BF16), v, preferred_element_type=F32)
        for ci, c in enumerate(blocks):
            o_ref[c] = o[:, ci * LANE:(ci + 1) * LANE].astype(o_ref.dtype)


def _cross_attention(q, mkv, b, s, n_mem, tm):
    nb, n, _ = q.shape
    tm = min(tm, s)
    per_b = s // tm
    return pl.pallas_call(
        functools.partial(_cross_kernel, bph=nb // MEM_HEADS),
        grid=(n // tm,),
        in_specs=[
            pl.BlockSpec((nb, tm, LANE), lambda i: (0, i, 0)),
            pl.BlockSpec((nb, n_mem, LANE), lambda i: (0, i // per_b, 0)),
            pl.BlockSpec((nb, n_mem, LANE), lambda i: (1, i // per_b, 0)),
        ],
        out_specs=pl.BlockSpec((nb, tm, LANE), lambda i: (0, i, 0)),
        out_shape=jax.ShapeDtypeStruct((nb, n, LANE), BF16),
        compiler_params=_params("parallel"),
        name="cross_attention",
    )(q, mkv, mkv)


def _router_kernel(h_ref, g_ref, w_ref, b_ref, z_ref, info_ref):
    x = h_ref[...]
    z = x * lax.rsqrt(jnp.mean(x * x, axis=-1, keepdims=True) + EPS) * g_ref[...]
    z_ref[...] = z
    logits = jnp.dot(z, w_ref[...], preferred_element_type=F32, precision=lax.Precision.HIGHEST) + b_ref[...]
    lane = lax.broadcasted_iota(jnp.int32, logits.shape, 1).astype(F32)
    big = float(LANE)

    def first_where(mask):
        return jnp.min(jnp.where(mask, lane, big), axis=1, keepdims=True)

    gmask = lane < N_GROUPS
    gl = jnp.where(gmask, logits, NEG_INF)
    gmax = jnp.max(gl, axis=1, keepdims=True)
    gsum = jnp.sum(jnp.exp(gl - gmax), axis=1, keepdims=True)
    g_idx = first_where(gl == gmax)
    g_w = 1.0 / gsum
    lo = N_GROUPS + g_idx * EXPERTS_PER_GROUP
    emask = (lane >= lo) & (lane < lo + EXPERTS_PER_GROUP)
    el = jnp.where(emask, logits, NEG_INF)
    emax = jnp.max(el, axis=1, keepdims=True)
    eexp = jnp.exp(el - emax)
    prob = jnp.where(emask, eexp / jnp.sum(eexp, axis=1, keepdims=True), -1.0)
    p1 = jnp.max(prob, axis=1, keepdims=True)
    i1 = first_where(prob == p1)
    prob2 = jnp.where(lane == i1, -1.0, prob)
    p2 = jnp.max(prob2, axis=1, keepdims=True)
    i2 = first_where(prob2 == p2)
    denom = p1 + p2
    w1 = g_w * p1 / denom
    w2 = g_w * p2 / denom
    info = jnp.where(lane == 0, i1 - N_GROUPS,
                     jnp.where(lane == 1, i2 - N_GROUPS, jnp.where(lane == 2, w1, jnp.where(lane == 3, w2, 0.0))))
    info_ref[...] = info


def _router(h, g, w_r, b_r, tm):
    n, d = h.shape
    tm = min(tm, n)
    return pl.pallas_call(
        _router_kernel,
        grid=(n // tm,),
        in_specs=[pl.BlockSpec((tm, d), lambda i: (i, 0)), pl.BlockSpec((1, d), lambda i: (0, 0)),
                  pl.BlockSpec((d, LANE), lambda i: (0, 0)), pl.BlockSpec((1, LANE), lambda i: (0, 0))],
        out_specs=[pl.BlockSpec((tm, d), lambda i: (i, 0)), pl.BlockSpec((tm, LANE), lambda i: (i, 0))],
        out_shape=[jax.ShapeDtypeStruct((n, d), F32), jax.ShapeDtypeStruct((n, LANE), F32)],
        compiler_params=_params("parallel"),
        name="moe_router",
    )(h, g.reshape(1, d).astype(F32), w_r, b_r)


def _row_copy(src_hbm, row, dst, dst_row, sem):
    return pltpu.make_async_copy(src_hbm.at[pl.ds(row, 1)], dst.at[pl.ds(dst_row, 1)], sem)


def _gather_rows(idx_ref, idx_off, src_hbm, dst, sem, n_rows):
    def issue(r, carry):
        _row_copy(src_hbm, idx_ref[0, idx_off + r], dst, r, sem).start()
        return carry

    lax.fori_loop(0, n_rows, issue, 0)

    def wait(r, carry):
        _row_copy(src_hbm, 0, dst, r, sem).wait()
        return carry

    lax.fori_loop(0, n_rows, wait, 0)


def _expert_kernel(blk_e_ref, nvalid_ref, tok_ref, z_hbm, wg_ref, wu_ref, wd_ref, y_ref, xbuf, sem, *, tb):
    i = pl.program_id(0)

    @pl.when(i < nvalid_ref[0])
    def _():
        _gather_rows(tok_ref, 0, z_hbm, xbuf, sem, tb)
        x = xbuf[...].astype(BF16)
        gate = jnp.dot(x, wg_ref[...], preferred_element_type=F32)
        up = jnp.dot(x, wu_ref[...], preferred_element_type=F32)
        hidden = (gate * jax.nn.sigmoid(gate) * up).astype(BF16)
        y_ref[...] = jnp.dot(hidden, wd_ref[...], preferred_element_type=F32)

    @pl.when(i >= nvalid_ref[0])
    def _():
        y_ref[...] = jnp.zeros_like(y_ref)


def _expert_blocks(blk_e, nvalid, tok, z, wg, wu, wd, tb):
    nblk = tok.shape[0]
    d = z.shape[1]
    ff = wg.shape[2]
    grid_spec = pltpu.PrefetchScalarGridSpec(
        num_scalar_prefetch=2,
        grid=(nblk,),
        in_specs=[
            pl.BlockSpec((None, 1, tb), lambda i, be, nv: (i, 0, 0), memory_space=pltpu.SMEM),
            pl.BlockSpec(memory_space=pl.ANY),
            pl.BlockSpec((None, d, ff), lambda i, be, nv: (be[i], 0, 0)),
            pl.BlockSpec((None, d, ff), lambda i, be, nv: (be[i], 0, 0)),
            pl.BlockSpec((None, ff, d), lambda i, be, nv: (be[i], 0, 0)),
        ],
        out_specs=pl.BlockSpec((tb, d), lambda i, be, nv: (i, 0)),
        scratch_shapes=[pltpu.VMEM((tb, d), F32), pltpu.SemaphoreType.DMA(())],
    )
    return pl.pallas_call(
        functools.partial(_expert_kernel, tb=tb),
        grid_spec=grid_spec,
        out_shape=jax.ShapeDtypeStruct((nblk * tb, d), F32),
        compiler_params=_params("arbitrary"),
        name="moe_experts",
    )(blk_e, nvalid, tok, z, wg, wu, wd)


def _combine_kernel(dest_ref, y_hbm, h_ref, info_ref, g_ref, o_ref, ybuf, sem, *, tm, final_norm):
    _gather_rows(dest_ref, 0, y_hbm, ybuf.at[0], sem.at[0], tm)
    _gather_rows(dest_ref, tm, y_hbm, ybuf.at[1], sem.at[1], tm)
    info = info_ref[...]
    moe = ybuf[0] * info[:, 2:3] + ybuf[1] * info[:, 3:4]
    hx = h_ref[...] + moe
    if final_norm:
        hx = hx * lax.rsqrt(jnp.mean(hx * hx, axis=-1, keepdims=True) + EPS) * g_ref[...]
    o_ref[...] = hx


def _combine(dest, y, h, info, g_final, tm, final_norm):
    n, d = h.shape
    return pl.pallas_call(
        functools.partial(_combine_kernel, tm=tm, final_norm=final_norm),
        grid=(n // tm,),
        in_specs=[
            pl.BlockSpec((None, 1, 2 * tm), lambda i: (i, 0, 0), memory_space=pltpu.SMEM),
            pl.BlockSpec(memory_space=pl.ANY),
            pl.BlockSpec((tm, d), lambda i: (i, 0)),
            pl.BlockSpec((tm, LANE), lambda i: (i, 0)),
            pl.BlockSpec((1, d), lambda i: (0, 0)),
        ],
        out_specs=pl.BlockSpec((tm, d), lambda i: (i, 0)),
        out_shape=jax.ShapeDtypeStruct((n, d), F32),
        scratch_shapes=[pltpu.VMEM((2, tm, d), F32), pltpu.SemaphoreType.DMA((2,))],
        compiler_params=_params("arbitrary"),
        name="moe_combine_final_norm",
    )(dest, y, h, info, g_final.reshape(1, d).astype(F32))


def _dispatch_plan(eid, n, tb):
    n_slots = 2 * n
    nblk = n_slots // tb + N_EXPERTS
    slot_e = eid.reshape(-1)
    onehot = (slot_e[:, None] == jnp.arange(N_EXPERTS, dtype=jnp.int32)[None, :]).astype(jnp.int32)
    csum = jnp.cumsum(onehot, axis=0)
    rank = jnp.take_along_axis(csum, slot_e[:, None], axis=1)[:, 0] - 1
    counts = csum[-1]
    blocks_per_e = (counts + tb - 1) // tb
    blk_end = jnp.cumsum(blocks_per_e)
    blk_start = blk_end - blocks_per_e
    dest = blk_start[slot_e] * tb + rank
    slot_tok = jnp.arange(n_slots, dtype=jnp.int32) // 2
    tok = jnp.zeros((nblk * tb,), jnp.int32).at[dest].set(slot_tok)
    blk_e = jnp.minimum(jnp.searchsorted(blk_end, jnp.arange(nblk, dtype=jnp.int32), side="right"),
                        N_EXPERTS - 1).astype(jnp.int32)
    nvalid = blk_end[-1:].astype(jnp.int32)
    return blk_e, nvalid, tok.reshape(nblk, 1, tb), dest.reshape(n, 2)


def _rope_tables(positions):
    half = MLA_ROPE // 2
    inv_freq = ROPE_THETA ** (-jnp.arange(half, dtype=F32) / half)
    ang = positions.astype(F32).reshape(-1, 1) * inv_freq[None, :]
    cos, sin = jnp.cos(ang), jnp.sin(ang)
    zero = jnp.zeros_like(cos)
    cos_t = jnp.concatenate([cos, cos, zero, zero], axis=1)
    sin_lo = jnp.concatenate([-sin, zero, zero, zero], axis=1)
    sin_hi = jnp.concatenate([zero, sin, zero, zero], axis=1)
    return cos_t, sin_lo, sin_hi


def kernel(x, mem, positions, g_mix, w_in, b_forget, g_q_latent, g_kv_latent, w_q_up, w_k_up, w_v_up, w_out, g_cross, g_mem, w_cross_q, w_cross_kv, w_cross_out, g_moe, w_group_router, b_group_router, w_expert_router, b_expert_router, w_exp_gate, w_exp_up, w_exp_down, g_final):
    b, s, d = x.shape
    n = b * s
    n_mem = mem.shape[1]
    depth = w_in.shape[0]
    fw = FOX_HEADS * LANE
    h = x.reshape(n, d)
    rope_tabs = _rope_tables(positions)

    for l in range(depth):
        wi = w_in[l]
        fox_scale = (LANE ** -0.5) * LOG2E
        o_f = 3 * fw
        o_cq = o_f + FOX_HEADS
        o_kr = o_cq + 2 * MLA_RANK
        o_g = o_kr + MLA_ROPE
        w_big = jnp.concatenate([wi[:, :fw] * fox_scale, wi[:, fw:2 * fw], wi[:, o_g:]], axis=1).astype(BF16)
        w_fox_v_t = wi[:, 2 * fw:3 * fw].T.astype(BF16)
        w_small = jnp.concatenate(
            [wi[:, o_cq:o_kr], wi[:, o_kr:o_g], wi[:, o_f:o_cq],
             jnp.zeros((d, LANE - MLA_ROPE - FOX_HEADS), F32)], axis=1).astype(BF16)
        mla_scale = ((MLA_NOPE + MLA_ROPE) ** -0.5) * LOG2E
        wq = (w_q_up[l] * mla_scale).reshape(MLA_RANK, MLA_HEADS, MLA_NOPE + MLA_ROPE)
        wq = jnp.pad(wq, ((0, 0), (0, 0), (0, 2 * LANE - MLA_NOPE - MLA_ROPE))).reshape(MLA_RANK, -1).astype(BF16)
        wk = w_k_up[l].astype(BF16)
        w_mla_v_t = w_v_up[l].T.astype(BF16)

        big = _norm_matmul(h, 0, d, g_mix[l], w_big, blocked=True, out_dtype=BF16, tm=512, tn=512, name="in_proj")
        small = _norm_matmul(h, 0, d, g_mix[l], w_small, blocked=False, out_dtype=F32, tm=512, tn=384,
                             name="in_proj_small")
        nh = FOX_HEADS
        fox_v_t = _norm_matmul_t(h, 0, d, g_mix[l], w_fox_v_t, tm=512, tr=512, name="fox_v_t")
        bias = _forget_bias(small, 2 * MLA_RANK // LANE, MLA_ROPE, b_forget[l], b, s, 512)
        o_a = _causal_attention(big, lambda hd: hd, None, None, big, lambda hd: nh + hd, bias, lambda hd: hd,
                                fox_v_t, nh, b, s, ATTN_TILE, 1, "fox_attention")
        qm = _norm_matmul(small, 0, MLA_RANK, g_q_latent[l], wq, blocked=True, out_dtype=BF16, tm=512, tn=512,
                          rope_tabs=rope_tabs, name="mla_q_up")
        k_nope = _norm_matmul(small, 1, MLA_RANK, g_kv_latent[l], wk, blocked=True, out_dtype=BF16, tm=512, tn=512,
                              name="mla_k_up")
        mla_v_t = _norm_matmul_t(small, 1, MLA_RANK, g_kv_latent[l], w_mla_v_t, tm=512, tr=512, name="mla_v_t")
        k_rope = _krope(small, 2 * MLA_RANK // LANE, rope_tabs, 512).reshape(1, n, LANE)
        o_b = _causal_attention(qm, lambda hd: 2 * hd, qm, lambda hd: 2 * hd + 1, k_nope, lambda hd: hd,
                                k_rope, lambda hd: 0, mla_v_t, MLA_HEADS, b, s, ATTN_TILE, CHUNK, "mla_attention")
        h = _proj_residual(o_a, w_out[l].astype(BF16), h, tm=512, tn=512, merge=(o_b, big, 2, 3), name="out_proj")
        cross_scale = ((d // MEM_HEADS) ** -0.5) * LOG2E
        q_c = _norm_matmul(h, 0, d, g_cross[l], (w_cross_q[l] * cross_scale).astype(BF16), blocked=True,
                           out_dtype=BF16, tm=512, tn=512, name="cross_q")
        mkv = _norm_matmul(mem.reshape(b * n_mem, d), 0, d, g_mem[l], w_cross_kv[l].astype(BF16), blocked=True,
                           out_dtype=BF16, tm=512, tn=512, name="cross_mem_kv")
        o_c = _cross_attention(q_c, mkv, b, s, n_mem, 512)
        h = _proj_residual(o_c, w_cross_out[l].astype(BF16), h, tm=512, tn=512, name="cross_out")
        w_r = jnp.concatenate([w_group_router[l], w_expert_router[l],
                               jnp.zeros((d, LANE - N_GROUPS - N_EXPERTS), F32)], axis=1)
        b_r = jnp.concatenate([b_group_router[l], b_expert_router[l],
                               jnp.zeros((LANE - N_GROUPS - N_EXPERTS,), F32)]).reshape(1, LANE)
        z, info = _router(h, g_moe[l], w_r, b_r, 256)
        tb = 256
        eid = info[:, 0:2].astype(jnp.int32)
        blk_e, nvalid, tok, dest = _dispatch_plan(eid, n, tb)
        y = _expert_blocks(blk_e, nvalid, tok, z, w_exp_gate[l].astype(BF16), w_exp_up[l].astype(BF16),
                           w_exp_down[l].astype(BF16), tb)
        tm_c = min(256, n)
        dest_blk = dest.reshape(n // tm_c, tm_c, 2).transpose(0, 2, 1).reshape(n // tm_c, 1, 2 * tm_c)
        h = _combine(dest_blk, y, h, info, g_final, tm_c, final_norm=(l + 1 == depth))
    return h.reshape(b, s, d)
```

```python
import functools
import math

import jax
import jax.numpy as jnp
from jax import lax
from jax.experimental import pallas as pl
from jax.experimental.pallas import tpu as pltpu

F32 = jnp.float32
BF16 = jnp.bfloat16

LANE = 128
EPS = 1e-6
CHUNK = 64
FOX_HEADS = 16
MLA_HEADS = 16
MLA_NOPE = 128
MLA_ROPE = 64
MLA_RANK = 512
ROPE_THETA = 10000.0
MEM_HEADS = 4
N_GROUPS = 8
EXPERTS_PER_GROUP = 8
N_EXPERTS = N_GROUPS * EXPERTS_PER_GROUP
LOG2E = math.log2(math.e)
VMEM_LIMIT = 56 * 1024 * 1024
NEG_INF = float("-inf")
ATTN_TILE = 1024
ATTN_HEAD_GROUP = 2
PROJ_TM = 1024
PROJ_TN = 1024
MERGE_TM = 512
GATHER_UNROLL = 8


def _params(*sem):
    return pltpu.CompilerParams(dimension_semantics=sem, vmem_limit_bytes=VMEM_LIMIT)


def _rope_rotate(x, cos, sin_lo, sin_hi):
    return x * cos + pltpu.roll(x, 96, 1) * sin_lo + pltpu.roll(x, 32, 1) * sin_hi


def _norm_matmul_kernel(*refs, rope, blocked, nblk):
    if rope:
        x_ref, g_ref, w_ref, cos_ref, slo_ref, shi_ref, o_ref, xn_ref = refs
    else:
        x_ref, g_ref, w_ref, o_ref, xn_ref = refs

    @pl.when(pl.program_id(1) == 0)
    def _():
        x = x_ref[...].astype(F32)
        y = x * lax.rsqrt(jnp.mean(x * x, axis=-1, keepdims=True) + EPS)
        xn_ref[...] = (y * g_ref[...]).astype(BF16)

    acc = jnp.dot(xn_ref[...], w_ref[...], preferred_element_type=F32)
    if blocked:
        for c in range(nblk):
            blk = acc[:, c * LANE:(c + 1) * LANE]
            if rope and c % 2 == 1:
                blk = _rope_rotate(blk, cos_ref[...], slo_ref[...], shi_ref[...])
            o_ref[c] = blk.astype(o_ref.dtype)
    else:
        o_ref[...] = acc.astype(o_ref.dtype)


def _norm_matmul(x, x_col, k, g, w, *, blocked, out_dtype, tm, tn, rope_tabs=None, name=None):
    m = x.shape[0]
    nc = w.shape[1]
    tm = min(tm, m)
    tn = min(tn, nc)
    assert m % tm == 0 and nc % tn == 0 and tn % LANE == 0
    nblk = tn // LANE
    in_specs = [
        pl.BlockSpec((tm, k), lambda i, j: (i, x_col)),
        pl.BlockSpec((1, k), lambda i, j: (0, 0)),
        pl.BlockSpec((k, tn), lambda i, j: (0, j)),
    ]
    args = [x, g.reshape(1, k).astype(F32), w]
    if rope_tabs is not None:
        in_specs += [pl.BlockSpec((tm, LANE), lambda i, j: (i, 0))] * 3
        args += list(rope_tabs)
    if blocked:
        out_shape = jax.ShapeDtypeStruct((nc // LANE, m, LANE), out_dtype)
        out_spec = pl.BlockSpec((nblk, tm, LANE), lambda i, j: (j, i, 0))
    else:
        out_shape = jax.ShapeDtypeStruct((m, nc), out_dtype)
        out_spec = pl.BlockSpec((tm, tn), lambda i, j: (i, j))
    return pl.pallas_call(
        functools.partial(_norm_matmul_kernel, rope=rope_tabs is not None, blocked=blocked, nblk=nblk),
        grid=(m // tm, nc // tn),
        in_specs=in_specs,
        out_specs=out_spec,
        out_shape=out_shape,
        scratch_shapes=[pltpu.VMEM((tm, k), BF16)],
        compiler_params=_params("parallel", "arbitrary"),
        name=name,
    )(*args)


def _krope_kernel(x_ref, cos_ref, slo_ref, shi_ref, o_ref):
    o_ref[...] = _rope_rotate(x_ref[...], cos_ref[...], slo_ref[...], shi_ref[...]).astype(o_ref.dtype)


def _krope(small, col_blk, rope_tabs, tm):
    m = small.shape[0]
    tm = min(tm, m)
    return pl.pallas_call(
        _krope_kernel,
        grid=(m // tm,),
        in_specs=[pl.BlockSpec((tm, LANE), lambda i: (i, col_blk))] + [pl.BlockSpec((tm, LANE), lambda i: (i, 0))] * 3,
        out_specs=pl.BlockSpec((tm, LANE), lambda i: (i, 0)),
        out_shape=jax.ShapeDtypeStruct((m, LANE), BF16),
        compiler_params=_params("parallel"),
        name="k_rope",
    )(small, *rope_tabs)


N_SPLIT = 3


def _forget_bias_kernel(f_ref, b_ref, o_ref, carry_ref, *, ts, lane0, heads):
    @pl.when(pl.program_id(1) == 0)
    def _():
        carry_ref[...] = jnp.zeros_like(carry_ref)

    x = f_ref[...] + b_ref[...]
    log_f = jnp.minimum(x, 0.0) - jnp.log1p(jnp.exp(-jnp.abs(x)))
    row = lax.broadcasted_iota(jnp.int32, (ts, ts), 0)
    col = lax.broadcasted_iota(jnp.int32, (ts, ts), 1)
    lower = (col <= row).astype(F32)
    c = jnp.dot(lower, log_f, preferred_element_type=F32, precision=lax.Precision.HIGHEST) + carry_ref[0:1, :]
    carry_ref[...] = jnp.broadcast_to(c[ts - 1:ts, :], carry_ref.shape)
    rest = -(c * LOG2E)
    pieces = []
    for _ in range(N_SPLIT):
        piece = rest.astype(BF16).astype(F32)
        pieces.append(piece)
        rest = rest - piece
    lane = lax.broadcasted_iota(jnp.int32, (ts, LANE), 1)
    out = jnp.zeros((ts, LANE), F32)
    for hd in range(heads):
        for pi, piece in enumerate(pieces):
            out = jnp.where(lane == N_SPLIT * hd + pi, piece[:, lane0 + hd:lane0 + hd + 1], out)
    o_ref[...] = out.astype(o_ref.dtype)


def _forget_bias(small, col_blk, lane0, b_forget, b, s, ts):
    heads = b_forget.shape[0]
    assert N_SPLIT * heads <= LANE
    ts = min(ts, s)
    per_b = s // ts
    b_row = jnp.zeros((1, LANE), F32).at[0, lane0:lane0 + heads].set(b_forget.astype(F32))
    return pl.pallas_call(
        functools.partial(_forget_bias_kernel, ts=ts, lane0=lane0, heads=heads),
        grid=(b, per_b),
        in_specs=[pl.BlockSpec((ts, LANE), lambda bi, j: (bi * per_b + j, col_blk)),
                  pl.BlockSpec((1, LANE), lambda bi, j: (0, 0))],
        out_specs=pl.BlockSpec((ts, LANE), lambda bi, j: (bi * per_b + j, 0)),
        out_shape=jax.ShapeDtypeStruct((b * s, LANE), BF16),
        scratch_shapes=[pltpu.VMEM((8, LANE), F32)],
        compiler_params=_params("parallel", "arbitrary"),
        name="forget_bias",
    )(small, b_row)


def _norm_matmul_t_kernel(x_ref, g_ref, w_ref, o_ref, xn_ref):
    @pl.when(pl.program_id(1) == 0)
    def _():
        x = x_ref[...].astype(F32)
        y = x * lax.rsqrt(jnp.mean(x * x, axis=-1, keepdims=True) + EPS)
        xn_ref[...] = (y * g_ref[...]).astype(BF16)

    o_ref[...] = lax.dot_general(w_ref[...], xn_ref[...], (((1,), (1,)), ((), ())),
                                 preferred_element_type=F32).astype(o_ref.dtype)


def _norm_matmul_t(x, x_col, k, g, w_t, *, tm, tr, name=None):
    m = x.shape[0]
    r = w_t.shape[0]
    tm = min(tm, m)
    tr = min(tr, r)
    return pl.pallas_call(
        _norm_matmul_t_kernel,
        grid=(m // tm, r // tr),
        in_specs=[pl.BlockSpec((tm, k), lambda i, j: (i, x_col)), pl.BlockSpec((1, k), lambda i, j: (0, 0)),
                  pl.BlockSpec((tr, k), lambda i, j: (j, 0))],
        out_specs=pl.BlockSpec((tr, tm), lambda i, j: (j, i)),
        out_shape=jax.ShapeDtypeStruct((r, m), BF16),
        scratch_shapes=[pltpu.VMEM((tm, k), BF16)],
        compiler_params=_params("parallel", "arbitrary"),
        name=name,
    )(x, g.reshape(1, k).astype(F32), w_t)


N_ATTN_SCRATCH = 7


def _attn_kernel(*refs, t, unit, q_ones, group):
    n_in = 3 if q_ones else 4
    kb_ref = refs[group * n_in]
    o_ref = refs[group * n_in + 1]
    scratch = refs[group * n_in + 2:]
    i = pl.program_id(2)
    streams = []
    for g in range(group):
        head_refs = refs[g * n_in:(g + 1) * n_in]
        if q_ones:
            qa_ref, ka_ref, vt_ref = head_refs
            lane = lax.broadcasted_iota(jnp.int32, (t, LANE), 1)
            lo = N_SPLIT * (pl.program_id(1) * group + g)
            q_b = jnp.where((lane >= lo) & (lane < lo + N_SPLIT), 1.0, 0.0).astype(BF16)
        else:
            qa_ref, qb_ref, ka_ref, vt_ref = head_refs
            q_b = qb_ref[...]
        q = jnp.concatenate([qa_ref[...], q_b], axis=1)
        streams.append((q, ka_ref, vt_ref) + tuple(scratch[g * N_ATTN_SCRATCH:(g + 1) * N_ATTN_SCRATCH]))

    def stage(name, j=None):
        for st in streams:
            _attn_stream(st, kb_ref, j, t, unit, name)

    def body(tau, carry):
        stage("values", tau - 2)
        stage("softmax")
        stage("logits", tau)
        return carry

    stage("init")
    stage("logits", 0)
    lax.fori_loop(1, i + 1, body, 0)
    stage("values", i - 1)
    stage("softmax_masked")
    stage("values", i)
    for g, st in enumerate(streams):
        o_ref[g] = jnp.transpose(st[5][...] / st[4][...]).astype(o_ref.dtype)


def _attn_stream(st, kb_ref, j, t, unit, stage):
    q, ka_ref, vt_ref, m_scr, l_scr, acc_scr, s_ref, p_ref, a_ref, bmax_ref = st
    if stage == "init":
        m_scr[...] = jnp.full(m_scr.shape, NEG_INF, F32)
        l_scr[...] = jnp.zeros(l_scr.shape, F32)
        acc_scr[...] = jnp.zeros(acc_scr.shape, F32)
        p_ref[...] = jnp.zeros(p_ref.shape, BF16)
        a_ref[...] = jnp.ones(a_ref.shape, F32)
    elif stage == "logits":
        off = pl.multiple_of(j * t, t)
        k = jnp.concatenate([ka_ref[pl.ds(off, t), :], kb_ref[pl.ds(off, t), :]], axis=1)
        s = lax.dot_general(k, q, (((1,), (1,)), ((), ())), preferred_element_type=F32)
        s_ref[...] = s
        bmax_ref[...] = jnp.max(s, axis=0, keepdims=True)
    elif stage == "values":
        off = pl.multiple_of(jnp.maximum(j, 0) * t, t)
        pv = jnp.dot(vt_ref[:, pl.ds(off, t)], p_ref[...], preferred_element_type=F32)
        acc_scr[...] = a_ref[...] * acc_scr[...] + pv
    else:
        masked = stage == "softmax_masked"
        s = s_ref[...]
        if masked:
            key = lax.broadcasted_iota(jnp.int32, (t, t), 0)
            qry = lax.broadcasted_iota(jnp.int32, (t, t), 1)
            s = jnp.where((key // unit) <= (qry // unit), s, NEG_INF)
            block_max = jnp.max(s, axis=0, keepdims=True)
        else:
            block_max = bmax_ref[...]
        m_prev = m_scr[...]
        m_new = jnp.maximum(m_prev, block_max)
        alpha = jnp.exp2(m_prev - m_new)
        p = jnp.exp2(s - m_new)
        l_scr[...] = alpha * l_scr[...] + jnp.sum(p, axis=0, keepdims=True)
        m_scr[...] = m_new
        p_ref[...] = p.astype(BF16)
        a_ref[...] = alpha


def _causal_attention(q_a, qa_blk, q_b, qb_blk, k_a, ka_blk, k_b, v_t, heads, b, s, t, unit, name):
    t = min(t, s)
    group = ATTN_HEAD_GROUP
    assert t % unit == 0 and heads % group == 0
    nq = s // t
    once = pl.Buffered(1)
    in_specs, args = [], []
    for g in range(group):
        def q_spec(blk, g=g):
            return pl.BlockSpec((None, t, LANE), lambda bi, hi, i: (blk(hi * group + g), bi * nq + i, 0))

        in_specs.append(q_spec(qa_blk))
        args.append(q_a)
        if q_b is not None:
            in_specs.append(q_spec(qb_blk))
            args.append(q_b)
        in_specs.append(pl.BlockSpec((None, s, LANE), lambda bi, hi, i, g=g: (ka_blk(hi * group + g), bi, 0),
                                     pipeline_mode=once))
        in_specs.append(pl.BlockSpec((LANE, s), lambda bi, hi, i, g=g: (hi * group + g, bi), pipeline_mode=once))
        args += [k_a, v_t]
    in_specs.append(pl.BlockSpec((s, LANE), lambda bi, hi, i: (bi, 0), pipeline_mode=once))
    args.append(k_b)
    stat = pltpu.VMEM((1, t), F32)
    per_head_scratch = [stat, stat, pltpu.VMEM((LANE, t), F32), pltpu.VMEM((t, t), F32),
                        pltpu.VMEM((t, t), BF16), stat, stat]
    assert len(per_head_scratch) == N_ATTN_SCRATCH
    return pl.pallas_call(
        functools.partial(_attn_kernel, t=t, unit=unit, q_ones=q_b is None, group=group),
        grid=(b, heads // group, nq),
        in_specs=in_specs,
        out_specs=pl.BlockSpec((group, t, LANE), lambda bi, hi, i: (hi, bi * nq + i, 0)),
        out_shape=jax.ShapeDtypeStruct((heads, b * s, LANE), BF16),
        scratch_shapes=per_head_scratch * group,
        compiler_params=_params("parallel", "parallel", "arbitrary"),
        name=name,
    )(*args)


def _proj_residual_kernel(*refs, merge, nb):
    if merge:
        oa_ref, ob_ref, ga_ref, gb_ref, w_ref, res_ref, o_ref, a_scr = refs
    else:
        oa_ref, w_ref, res_ref, o_ref, a_scr = refs

    @pl.when(pl.program_id(1) == 0)
    def _():
        for c in range(nb):
            if merge:
                a = (jax.nn.sigmoid(ga_ref[c].astype(F32)) * oa_ref[c].astype(F32)
                     + jax.nn.sigmoid(gb_ref[c].astype(F32)) * ob_ref[c].astype(F32))
            else:
                a = oa_ref[c]
            a_scr[:, c * LANE:(c + 1) * LANE] = a.astype(BF16)

    o_ref[...] = res_ref[...] + jnp.dot(a_scr[...], w_ref[...], preferred_element_type=F32)


def _proj_residual(o_a, w, res, *, tm, tn, merge=None, name=None):
    nb, m, _ = o_a.shape
    k = nb * LANE
    nc = w.shape[1]
    tm = min(tm, m)
    tn = min(tn, nc)
    head_spec = pl.BlockSpec((nb, tm, LANE), lambda i, j: (0, i, 0))
    in_specs = [head_spec]
    args = [o_a]
    if merge is not None:
        o_b, gates, ga_grp, gb_grp = merge
        in_specs += [head_spec,
                     pl.BlockSpec((nb, tm, LANE), lambda i, j: (ga_grp, i, 0)),
                     pl.BlockSpec((nb, tm, LANE), lambda i, j: (gb_grp, i, 0))]
        args += [o_b, gates, gates]
    in_specs += [pl.BlockSpec((k, tn), lambda i, j: (0, j)), pl.BlockSpec((tm, tn), lambda i, j: (i, j))]
    args += [w, res]
    return pl.pallas_call(
        functools.partial(_proj_residual_kernel, merge=merge is not None, nb=nb),
        grid=(m // tm, nc // tn),
        in_specs=in_specs,
        out_specs=pl.BlockSpec((tm, tn), lambda i, j: (i, j)),
        out_shape=jax.ShapeDtypeStruct((m, nc), F32),
        scratch_shapes=[pltpu.VMEM((tm, k), BF16)],
        compiler_params=_params("parallel", "arbitrary"),
        name=name,
    )(*args)


def _cross_kernel(q_ref, mk_ref, mv_ref, o_ref, *, bph):
    for hh in range(MEM_HEADS):
        blocks = range(hh * bph, (hh + 1) * bph)
        q = jnp.concatenate([q_ref[c] for c in blocks], axis=1)
        k = jnp.concatenate([mk_ref[c] for c in blocks], axis=1)
        v = jnp.concatenate([mv_ref[c] for c in blocks], axis=1)
        s = lax.dot_general(q, k, (((1,), (1,)), ((), ())), preferred_element_type=F32)
        p = jnp.exp2(s - jnp.max(s, axis=1, keepdims=True))
        p = p / jnp.sum(p, axis=1, keepdims=True)
        o = jnp.dot(p.astype(BF16), v, preferred_element_type=F32)
        for ci, c in enumerate(blocks):
            o_ref[c] = o[:, ci * LANE:(ci + 1) * LANE].astype(o_ref.dtype)


def _cross_attention(q, mkv, b, s, n_mem, tm):
    nb, n, _ = q.shape
    tm = min(tm, s)
    per_b = s // tm
    return pl.pallas_call(
        functools.partial(_cross_kernel, bph=nb // MEM_HEADS),
        grid=(n // tm,),
        in_specs=[
            pl.BlockSpec((nb, tm, LANE), lambda i: (0, i, 0)),
            pl.BlockSpec((nb, n_mem, LANE), lambda i: (0, i // per_b, 0)),
            pl.BlockSpec((nb, n_mem, LANE), lambda i: (1, i // per_b, 0)),
        ],
        out_specs=pl.BlockSpec((nb, tm, LANE), lambda i: (0, i, 0)),
        out_shape=jax.ShapeDtypeStruct((nb, n, LANE), BF16),
        compiler_params=_params("parallel"),
        name="cross_attention",
    )(q, mkv, mkv)


def _router_kernel(h_ref, g_ref, w_ref, b_ref, z_ref, info_ref):
    x = h_ref[...]
    z = x * lax.rsqrt(jnp.mean(x * x, axis=-1, keepdims=True) + EPS) * g_ref[...]
    z_ref[...] = z
    logits = jnp.dot(z, w_ref[...], preferred_element_type=F32, precision=lax.Precision.HIGHEST) + b_ref[...]
    lane = lax.broadcasted_iota(jnp.int32, logits.shape, 1).astype(F32)
    big = float(LANE)

    def first_where(mask):
        return jnp.min(jnp.where(mask, lane, big), axis=1, keepdims=True)

    gmask = lane < N_GROUPS
    gl = jnp.where(gmask, logits, NEG_INF)
    gmax = jnp.max(gl, axis=1, keepdims=True)
    gsum = jnp.sum(jnp.exp(gl - gmax), axis=1, keepdims=True)
    g_idx = first_where(gl == gmax)
    g_w = 1.0 / gsum
    lo = N_GROUPS + g_idx * EXPERTS_PER_GROUP
    emask = (lane >= lo) & (lane < lo + EXPERTS_PER_GROUP)
    el = jnp.where(emask, logits, NEG_INF)
    emax = jnp.max(el, axis=1, keepdims=True)
    eexp = jnp.exp(el - emax)
    prob = jnp.where(emask, eexp / jnp.sum(eexp, axis=1, keepdims=True), -1.0)
    p1 = jnp.max(prob, axis=1, keepdims=True)
    i1 = first_where(prob == p1)
    prob2 = jnp.where(lane == i1, -1.0, prob)
    p2 = jnp.max(prob2, axis=1, keepdims=True)
    i2 = first_where(prob2 == p2)
    denom = p1 + p2
    w1 = g_w * p1 / denom
    w2 = g_w * p2 / denom
    info = jnp.where(lane == 0, i1 - N_GROUPS,
                     jnp.where(lane == 1, i2 - N_GROUPS, jnp.where(lane == 2, w1, jnp.where(lane == 3, w2, 0.0))))
    info_ref[...] = info


def _router(h, g, w_r, b_r, tm):
    n, d = h.shape
    tm = min(tm, n)
    return pl.pallas_call(
        _router_kernel,
        grid=(n // tm,),
        in_specs=[pl.BlockSpec((tm, d), lambda i: (i, 0)), pl.BlockSpec((1, d), lambda i: (0, 0)),
                  pl.BlockSpec((d, LANE), lambda i: (0, 0)), pl.BlockSpec((1, LANE), lambda i: (0, 0))],
        out_specs=[pl.BlockSpec((tm, d), lambda i: (i, 0)), pl.BlockSpec((tm, LANE), lambda i: (i, 0))],
        out_shape=[jax.ShapeDtypeStruct((n, d), F32), jax.ShapeDtypeStruct((n, LANE), F32)],
        compiler_params=_params("parallel"),
        name="moe_router",
    )(h, g.reshape(1, d).astype(F32), w_r, b_r)


def _row_copy(src_hbm, row, dst, dst_row, sem):
    return pltpu.make_async_copy(src_hbm.at[pl.ds(row, 1)], dst.at[pl.ds(dst_row, 1)], sem)


def _issue_rows(idx_ref, idx_off, src_hbm, dst, sem, n_rows):
    def issue(r, carry):
        _row_copy(src_hbm, idx_ref[0, idx_off + r], dst, r, sem).start()
        return carry

    lax.fori_loop(0, n_rows, issue, 0, unroll=GATHER_UNROLL)


def _wait_rows(src_hbm, dst, sem, n_rows):
    def wait(r, carry):
        _row_copy(src_hbm, 0, dst, r, sem).wait()
        return carry

    lax.fori_loop(0, n_rows, wait, 0, unroll=GATHER_UNROLL)


def _expert_kernel(blk_e_ref, nvalid_ref, tok0_ref, tok_next_ref, z_hbm, wg_ref, wu_ref, wd_ref, y_ref, xbuf, sem,
                   *, tb):
    i = pl.program_id(0)
    nvalid = nvalid_ref[0]
    slot = i % 2

    @pl.when((i == 0) & (nvalid > 0))
    def _():
        _issue_rows(tok0_ref, 0, z_hbm, xbuf.at[0], sem.at[0], tb)

    @pl.when(i + 1 < nvalid)
    def _():
        _issue_rows(tok_next_ref, 0, z_hbm, xbuf.at[1 - slot], sem.at[1 - slot], tb)

    @pl.when(i < nvalid)
    def _():
        _wait_rows(z_hbm, xbuf.at[slot], sem.at[slot], tb)
        x = xbuf[slot].astype(BF16)
        gate = jnp.dot(x, wg_ref[...], preferred_element_type=F32)
        up = jnp.dot(x, wu_ref[...], preferred_element_type=F32)
        hidden = (gate * jax.nn.sigmoid(gate) * up).astype(BF16)
        y_ref[...] = jnp.dot(hidden, wd_ref[...], preferred_element_type=F32)

    @pl.when(i >= nvalid)
    def _():
        y_ref[...] = jnp.zeros_like(y_ref)


def _expert_blocks(blk_e, nvalid, tok, z, wg, wu, wd, tb):
    nblk = tok.shape[0]
    d = z.shape[1]
    ff = wg.shape[2]
    grid_spec = pltpu.PrefetchScalarGridSpec(
        num_scalar_prefetch=2,
        grid=(nblk,),
        in_specs=[
            pl.BlockSpec((None, 1, tb), lambda i, be, nv: (0, 0, 0), memory_space=pltpu.SMEM),
            pl.BlockSpec((None, 1, tb), lambda i, be, nv: (jnp.minimum(i + 1, nblk - 1), 0, 0),
                         memory_space=pltpu.SMEM),
            pl.BlockSpec(memory_space=pl.ANY),
            pl.BlockSpec((None, d, ff), lambda i, be, nv: (be[i], 0, 0)),
            pl.BlockSpec((None, d, ff), lambda i, be, nv: (be[i], 0, 0)),
            pl.BlockSpec((None, ff, d), lambda i, be, nv: (be[i], 0, 0)),
        ],
        out_specs=pl.BlockSpec((tb, d), lambda i, be, nv: (i, 0)),
        scratch_shapes=[pltpu.VMEM((2, tb, d), F32), pltpu.SemaphoreType.DMA((2,))],
    )
    return pl.pallas_call(
        functools.partial(_expert_kernel, tb=tb),
        grid_spec=grid_spec,
        out_shape=jax.ShapeDtypeStruct((nblk * tb, d), F32),
        compiler_params=_params("arbitrary"),
        name="moe_experts",
    )(blk_e, nvalid, tok, tok, z, wg, wu, wd)


def _combine_kernel(dest0_ref, dest_next_ref, y_hbm, h_ref, info_ref, g_ref, o_ref, ybuf, sem, *, tm, final_norm):
    i = pl.program_id(0)
    slot = i % 2

    def issue(idx_ref, sl):
        for k in range(2):
            _issue_rows(idx_ref, k * tm, y_hbm, ybuf.at[sl, k], sem.at[sl], tm)

    @pl.when(i == 0)
    def _():
        issue(dest0_ref, 0)

    @pl.when(i + 1 < pl.num_programs(0))
    def _():
        issue(dest_next_ref, 1 - slot)

    for k in range(2):
        _wait_rows(y_hbm, ybuf.at[slot, k], sem.at[slot], tm)
    info = info_ref[...]
    moe = ybuf[slot, 0] * info[:, 2:3] + ybuf[slot, 1] * info[:, 3:4]
    hx = h_ref[...] + moe
    if final_norm:
        hx = hx * lax.rsqrt(jnp.mean(hx * hx, axis=-1, keepdims=True) + EPS) * g_ref[...]
    o_ref[...] = hx


def _combine(dest, y, h, info, g_final, tm, final_norm):
    n, d = h.shape
    n_tiles = n // tm
    return pl.pallas_call(
        functools.partial(_combine_kernel, tm=tm, final_norm=final_norm),
        grid=(n_tiles,),
        in_specs=[
            pl.BlockSpec((None, 1, 2 * tm), lambda i: (0, 0, 0), memory_space=pltpu.SMEM),
            pl.BlockSpec((None, 1, 2 * tm), lambda i: (jnp.minimum(i + 1, n_tiles - 1), 0, 0),
                         memory_space=pltpu.SMEM),
            pl.BlockSpec(memory_space=pl.ANY),
            pl.BlockSpec((tm, d), lambda i: (i, 0)),
            pl.BlockSpec((tm, LANE), lambda i: (i, 0)),
            pl.BlockSpec((1, d), lambda i: (0, 0)),
        ],
        out_specs=pl.BlockSpec((tm, d), lambda i: (i, 0)),
        out_shape=jax.ShapeDtypeStruct((n, d), F32),
        scratch_shapes=[pltpu.VMEM((2, 2, tm, d), F32), pltpu.SemaphoreType.DMA((2,))],
        compiler_params=_params("arbitrary"),
        name="moe_combine_final_norm",
    )(dest, dest, y, h, info, g_final.reshape(1, d).astype(F32))


def _dispatch_plan(eid, n, tb):
    n_slots = 2 * n
    nblk = n_slots // tb + N_EXPERTS
    slot_e = eid.reshape(-1)
    onehot = (slot_e[:, None] == jnp.arange(N_EXPERTS, dtype=jnp.int32)[None, :]).astype(jnp.int32)
    csum = jnp.cumsum(onehot, axis=0)
    rank = jnp.take_along_axis(csum, slot_e[:, None], axis=1)[:, 0] - 1
    counts = csum[-1]
    blocks_per_e = (counts + tb - 1) // tb
    blk_end = jnp.cumsum(blocks_per_e)
    blk_start = blk_end - blocks_per_e
    dest = blk_start[slot_e] * tb + rank
    slot_tok = jnp.arange(n_slots, dtype=jnp.int32) // 2
    tok = jnp.zeros((nblk * tb,), jnp.int32).at[dest].set(slot_tok)
    blk_e = jnp.minimum(jnp.searchsorted(blk_end, jnp.arange(nblk, dtype=jnp.int32), side="right"),
                        N_EXPERTS - 1).astype(jnp.int32)
    nvalid = blk_end[-1:].astype(jnp.int32)
    return blk_e, nvalid, tok.reshape(nblk, 1, tb), dest.reshape(n, 2)


def _rope_tables(positions):
    half = MLA_ROPE // 2
    inv_freq = ROPE_THETA ** (-jnp.arange(half, dtype=F32) / half)
    ang = positions.astype(F32).reshape(-1, 1) * inv_freq[None, :]
    cos, sin = jnp.cos(ang), jnp.sin(ang)
    zero = jnp.zeros_like(cos)
    cos_t = jnp.concatenate([cos, cos, zero, zero], axis=1)
    sin_lo = jnp.concatenate([-sin, zero, zero, zero], axis=1)
    sin_hi = jnp.concatenate([zero, sin, zero, zero], axis=1)
    return cos_t, sin_lo, sin_hi


def kernel(x, mem, positions, g_mix, w_in, b_forget, g_q_latent, g_kv_latent, w_q_up, w_k_up, w_v_up, w_out, g_cross, g_mem, w_cross_q, w_cross_kv, w_cross_out, g_moe, w_group_router, b_group_router, w_expert_router, b_expert_router, w_exp_gate, w_exp_up, w_exp_down, g_final):
    b, s, d = x.shape
    n = b * s
    n_mem = mem.shape[1]
    depth = w_in.shape[0]
    fw = FOX_HEADS * LANE
    h = x.reshape(n, d)
    rope_tabs = _rope_tables(positions)

    for l in range(depth):
        wi = w_in[l]
        fox_scale = (LANE ** -0.5) * LOG2E
        o_f = 3 * fw
        o_cq = o_f + FOX_HEADS
        o_kr = o_cq + 2 * MLA_RANK
        o_g = o_kr + MLA_ROPE
        w_big = jnp.concatenate([wi[:, :fw] * fox_scale, wi[:, fw:2 * fw], wi[:, o_g:]], axis=1).astype(BF16)
        w_fox_v_t = wi[:, 2 * fw:3 * fw].T.astype(BF16)
        w_small = jnp.concatenate(
            [wi[:, o_cq:o_kr], wi[:, o_kr:o_g], wi[:, o_f:o_cq],
             jnp.zeros((d, LANE - MLA_ROPE - FOX_HEADS), F32)], axis=1).astype(BF16)
        mla_scale = ((MLA_NOPE + MLA_ROPE) ** -0.5) * LOG2E
        wq = (w_q_up[l] * mla_scale).reshape(MLA_RANK, MLA_HEADS, MLA_NOPE + MLA_ROPE)
        wq = jnp.pad(wq, ((0, 0), (0, 0), (0, 2 * LANE - MLA_NOPE - MLA_ROPE))).reshape(MLA_RANK, -1).astype(BF16)
        wk = w_k_up[l].astype(BF16)
        w_mla_v_t = w_v_up[l].T.astype(BF16)

        big = _norm_matmul(h, 0, d, g_mix[l], w_big, blocked=True, out_dtype=BF16, tm=PROJ_TM, tn=PROJ_TN,
                           name="in_proj")
        small = _norm_matmul(h, 0, d, g_mix[l], w_small, blocked=False, out_dtype=F32, tm=PROJ_TM, tn=384,
                             name="in_proj_small")
        nh = FOX_HEADS
        fox_v_t = _norm_matmul_t(h, 0, d, g_mix[l], w_fox_v_t, tm=PROJ_TM, tr=PROJ_TN, name="fox_v_t")
        bias = _forget_bias(small, 2 * MLA_RANK // LANE, MLA_ROPE, b_forget[l], b, s, 512)
        o_a = _causal_attention(big, lambda hd: hd, None, None, big, lambda hd: nh + hd, bias,
                                fox_v_t, nh, b, s, ATTN_TILE, 1, "fox_attention")
        qm = _norm_matmul(small, 0, MLA_RANK, g_q_latent[l], wq, blocked=True, out_dtype=BF16, tm=PROJ_TM,
                          tn=PROJ_TN, rope_tabs=rope_tabs, name="mla_q_up")
        k_nope = _norm_matmul(small, 1, MLA_RANK, g_kv_latent[l], wk, blocked=True, out_dtype=BF16, tm=PROJ_TM,
                              tn=PROJ_TN, name="mla_k_up")
        mla_v_t = _norm_matmul_t(small, 1, MLA_RANK, g_kv_latent[l], w_mla_v_t, tm=PROJ_TM, tr=PROJ_TN,
                                 name="mla_v_t")
        k_rope = _krope(small, 2 * MLA_RANK // LANE, rope_tabs, 512)
        o_b = _causal_attention(qm, lambda hd: 2 * hd, qm, lambda hd: 2 * hd + 1, k_nope, lambda hd: hd,
                                k_rope, mla_v_t, MLA_HEADS, b, s, ATTN_TILE, CHUNK, "mla_attention")
        h = _proj_residual(o_a, w_out[l].astype(BF16), h, tm=MERGE_TM, tn=PROJ_TN, merge=(o_b, big, 2, 3),
                           name="out_proj")
        cross_scale = ((d // MEM_HEADS) ** -0.5) * LOG2E
        q_c = _norm_matmul(h, 0, d, g_cross[l], (w_cross_q[l] * cross_scale).astype(BF16), blocked=True,
                           out_dtype=BF16, tm=PROJ_TM, tn=PROJ_TN, name="cross_q")
        mkv = _norm_matmul(mem.reshape(b * n_mem, d), 0, d, g_mem[l], w_cross_kv[l].astype(BF16), blocked=True,
                           out_dtype=BF16, tm=PROJ_TM, tn=PROJ_TN, name="cross_mem_kv")
        o_c = _cross_attention(q_c, mkv, b, s, n_mem, 512)
        h = _proj_residual(o_c, w_cross_out[l].astype(BF16), h, tm=PROJ_TM, tn=PROJ_TN, name="cross_out")
        w_r = jnp.concatenate([w_group_router[l], w_expert_router[l],
                               jnp.zeros((d, LANE - N_GROUPS - N_EXPERTS), F32)], axis=1)
        b_r = jnp.concatenate([b_group_router[l], b_expert_router[l],
                               jnp.zeros((LANE - N_GROUPS - N_EXPERTS,), F32)]).reshape(1, LANE)
        z, info = _router(h, g_moe[l], w_r, b_r, 256)
        tb = 256
        eid = info[:, 0:2].astype(jnp.int32)
        blk_e, nvalid, tok, dest = _dispatch_plan(eid, n, tb)
        y = _expert_blocks(blk_e, nvalid, tok, z, w_exp_gate[l].astype(BF16), w_exp_up[l].astype(BF16),
                           w_exp_down[l].astype(BF16), tb)
        tm_c = min(256, n)
        dest_blk = dest.reshape(n // tm_c, tm_c, 2).transpose(0, 2, 1).reshape(n // tm_c, 1, 2 * tm_c)
        h = _combine(dest_blk, y, h, info, g_final, tm_c, final_norm=(l + 1 == depth))
    return h.reshape(b, s, d)
```

```python
import functools
import math

import jax
import jax.numpy as jnp
from jax import lax
from jax.experimental import pallas as pl
from jax.experimental.pallas import tpu as pltpu

F32 = jnp.float32
BF16 = jnp.bfloat16

LANE = 128
EPS = 1e-6
CHUNK = 64
FOX_HEADS = 16
MLA_HEADS = 16
MLA_NOPE = 128
MLA_ROPE = 64
MLA_RANK = 512
ROPE_THETA = 10000.0
MEM_HEADS = 4
N_GROUPS = 8
EXPERTS_PER_GROUP = 8
N_EXPERTS = N_GROUPS * EXPERTS_PER_GROUP
LOG2E = math.log2(math.e)
VMEM_LIMIT = 56 * 1024 * 1024
NEG_INF = float("-inf")
ATTN_TILE = 1024
ATTN_HEAD_GROUP = 2
PROJ_TM = 1024
PROJ_TN = 1024
MERGE_TM = 512
GATHER_UNROLL = 8


def _params(*sem):
    return pltpu.CompilerParams(dimension_semantics=sem, vmem_limit_bytes=VMEM_LIMIT)


def _resident(whole):
    return pl.Buffered(1) if whole else None


def _rope_rotate(x, cos, sin_lo, sin_hi):
    return x * cos + pltpu.roll(x, 96, 1) * sin_lo + pltpu.roll(x, 32, 1) * sin_hi


def _norm_matmul_kernel(*refs, rope, blocked, nblk):
    if rope:
        x_ref, g_ref, w_ref, cos_ref, slo_ref, shi_ref, o_ref, xn_ref = refs
    else:
        x_ref, g_ref, w_ref, o_ref, xn_ref = refs

    @pl.when(pl.program_id(1) == 0)
    def _():
        x = x_ref[...].astype(F32)
        y = x * lax.rsqrt(jnp.mean(x * x, axis=-1, keepdims=True) + EPS)
        xn_ref[...] = (y * g_ref[...]).astype(BF16)

    acc = jnp.dot(xn_ref[...], w_ref[...], preferred_element_type=F32)
    if blocked:
        for c in range(nblk):
            blk = acc[:, c * LANE:(c + 1) * LANE]
            if rope and c % 2 == 1:
                blk = _rope_rotate(blk, cos_ref[...], slo_ref[...], shi_ref[...])
            o_ref[c] = blk.astype(o_ref.dtype)
    else:
        o_ref[...] = acc.astype(o_ref.dtype)


def _norm_matmul(x, x_col, k, g, w, *, blocked, out_dtype, tm, tn, rope_tabs=None, name=None):
    m = x.shape[0]
    nc = w.shape[1]
    tm = min(tm, m)
    tn = min(tn, nc)
    assert m % tm == 0 and nc % tn == 0 and tn % LANE == 0
    nblk = tn // LANE
    in_specs = [
        pl.BlockSpec((tm, k), lambda i, j: (i, x_col)),
        pl.BlockSpec((1, k), lambda i, j: (0, 0)),
        pl.BlockSpec((k, tn), lambda i, j: (0, j), pipeline_mode=_resident(nc == tn)),
    ]
    args = [x, g.reshape(1, k).astype(F32), w]
    if rope_tabs is not None:
        in_specs += [pl.BlockSpec((tm, LANE), lambda i, j: (i, 0))] * 3
        args += list(rope_tabs)
    if blocked:
        out_shape = jax.ShapeDtypeStruct((nc // LANE, m, LANE), out_dtype)
        out_spec = pl.BlockSpec((nblk, tm, LANE), lambda i, j: (j, i, 0))
    else:
        out_shape = jax.ShapeDtypeStruct((m, nc), out_dtype)
        out_spec = pl.BlockSpec((tm, tn), lambda i, j: (i, j))
    return pl.pallas_call(
        functools.partial(_norm_matmul_kernel, rope=rope_tabs is not None, blocked=blocked, nblk=nblk),
        grid=(m // tm, nc // tn),
        in_specs=in_specs,
        out_specs=out_spec,
        out_shape=out_shape,
        scratch_shapes=[pltpu.VMEM((tm, k), BF16)],
        compiler_params=_params("parallel", "arbitrary"),
        name=name,
    )(*args)


def _krope_kernel(x_ref, cos_ref, slo_ref, shi_ref, o_ref):
    o_ref[...] = _rope_rotate(x_ref[...], cos_ref[...], slo_ref[...], shi_ref[...]).astype(o_ref.dtype)


def _krope(small, col_blk, rope_tabs, tm):
    m = small.shape[0]
    tm = min(tm, m)
    return pl.pallas_call(
        _krope_kernel,
        grid=(m // tm,),
        in_specs=[pl.BlockSpec((tm, LANE), lambda i: (i, col_blk))] + [pl.BlockSpec((tm, LANE), lambda i: (i, 0))] * 3,
        out_specs=pl.BlockSpec((tm, LANE), lambda i: (i, 0)),
        out_shape=jax.ShapeDtypeStruct((m, LANE), BF16),
        compiler_params=_params("parallel"),
        name="k_rope",
    )(small, *rope_tabs)


N_SPLIT = 3


def _forget_bias_kernel(f_ref, b_ref, o_ref, carry_ref, *, ts, lane0, heads):
    @pl.when(pl.program_id(1) == 0)
    def _():
        carry_ref[...] = jnp.zeros_like(carry_ref)

    x = f_ref[...] + b_ref[...]
    log_f = jnp.minimum(x, 0.0) - jnp.log1p(jnp.exp(-jnp.abs(x)))
    row = lax.broadcasted_iota(jnp.int32, (ts, ts), 0)
    col = lax.broadcasted_iota(jnp.int32, (ts, ts), 1)
    lower = (col <= row).astype(F32)
    c = jnp.dot(lower, log_f, preferred_element_type=F32, precision=lax.Precision.HIGHEST) + carry_ref[0:1, :]
    carry_ref[...] = jnp.broadcast_to(c[ts - 1:ts, :], carry_ref.shape)
    rest = -(c * LOG2E)
    pieces = []
    for _ in range(N_SPLIT):
        piece = rest.astype(BF16).astype(F32)
        pieces.append(piece)
        rest = rest - piece
    lane = lax.broadcasted_iota(jnp.int32, (ts, LANE), 1)
    out = jnp.zeros((ts, LANE), F32)
    for hd in range(heads):
        for pi, piece in enumerate(pieces):
            out = jnp.where(lane == N_SPLIT * hd + pi, piece[:, lane0 + hd:lane0 + hd + 1], out)
    o_ref[...] = out.astype(o_ref.dtype)


def _forget_bias(small, col_blk, lane0, b_forget, b, s, ts):
    heads = b_forget.shape[0]
    assert N_SPLIT * heads <= LANE
    ts = min(ts, s)
    per_b = s // ts
    b_row = jnp.zeros((1, LANE), F32).at[0, lane0:lane0 + heads].set(b_forget.astype(F32))
    return pl.pallas_call(
        functools.partial(_forget_bias_kernel, ts=ts, lane0=lane0, heads=heads),
        grid=(b, per_b),
        in_specs=[pl.BlockSpec((ts, LANE), lambda bi, j: (bi * per_b + j, col_blk)),
                  pl.BlockSpec((1, LANE), lambda bi, j: (0, 0))],
        out_specs=pl.BlockSpec((ts, LANE), lambda bi, j: (bi * per_b + j, 0)),
        out_shape=jax.ShapeDtypeStruct((b * s, LANE), BF16),
        scratch_shapes=[pltpu.VMEM((8, LANE), F32)],
        compiler_params=_params("parallel", "arbitrary"),
        name="forget_bias",
    )(small, b_row)


def _norm_matmul_t_kernel(x_ref, g_ref, w_ref, o_ref, xn_ref):
    @pl.when(pl.program_id(1) == 0)
    def _():
        x = x_ref[...].astype(F32)
        y = x * lax.rsqrt(jnp.mean(x * x, axis=-1, keepdims=True) + EPS)
        xn_ref[...] = (y * g_ref[...]).astype(BF16)

    o_ref[...] = lax.dot_general(w_ref[...], xn_ref[...], (((1,), (1,)), ((), ())),
                                 preferred_element_type=F32).astype(o_ref.dtype)


def _norm_matmul_t(x, x_col, k, g, w_t, *, tm, tr, name=None):
    m = x.shape[0]
    r = w_t.shape[0]
    tm = min(tm, m)
    tr = min(tr, r)
    return pl.pallas_call(
        _norm_matmul_t_kernel,
        grid=(m // tm, r // tr),
        in_specs=[pl.BlockSpec((tm, k), lambda i, j: (i, x_col)), pl.BlockSpec((1, k), lambda i, j: (0, 0)),
                  pl.BlockSpec((tr, k), lambda i, j: (j, 0))],
        out_specs=pl.BlockSpec((tr, tm), lambda i, j: (j, i)),
        out_shape=jax.ShapeDtypeStruct((r, m), BF16),
        scratch_shapes=[pltpu.VMEM((tm, k), BF16)],
        compiler_params=_params("parallel", "arbitrary"),
        name=name,
    )(x, g.reshape(1, k).astype(F32), w_t)


N_ATTN_SCRATCH = 7


def _attn_kernel(*refs, t, unit, q_ones, group):
    n_in = 3 if q_ones else 4
    kb_ref = refs[group * n_in]
    o_ref = refs[group * n_in + 1]
    scratch = refs[group * n_in + 2:]
    i = pl.program_id(2)
    streams = []
    for g in range(group):
        head_refs = refs[g * n_in:(g + 1) * n_in]
        if q_ones:
            qa_ref, ka_ref, vt_ref = head_refs
            lane = lax.broadcasted_iota(jnp.int32, (t, LANE), 1)
            lo = N_SPLIT * (pl.program_id(1) * group + g)
            q_b = jnp.where((lane >= lo) & (lane < lo + N_SPLIT), 1.0, 0.0).astype(BF16)
        else:
            qa_ref, qb_ref, ka_ref, vt_ref = head_refs
            q_b = qb_ref[...]
        q = jnp.concatenate([qa_ref[...], q_b], axis=1)
        streams.append((q, ka_ref, vt_ref) + tuple(scratch[g * N_ATTN_SCRATCH:(g + 1) * N_ATTN_SCRATCH]))

    def stage(name, j=None):
        for st in streams:
            _attn_stream(st, kb_ref, j, t, unit, name)

    def body(tau, carry):
        stage("values", tau - 2)
        stage("softmax")
        stage("logits", tau)
        return carry

    stage("init")
    stage("logits", 0)
    lax.fori_loop(1, i + 1, body, 0)
    stage("values", i - 1)
    stage("softmax_masked")
    stage("values", i)
    for g, st in enumerate(streams):
        o_ref[g] = jnp.transpose(st[5][...] / st[4][...]).astype(o_ref.dtype)


def _attn_stream(st, kb_ref, j, t, unit, stage):
    q, ka_ref, vt_ref, m_scr, l_scr, acc_scr, s_ref, p_ref, a_ref, bmax_ref = st
    if stage == "init":
        m_scr[...] = jnp.full(m_scr.shape, NEG_INF, F32)
        l_scr[...] = jnp.zeros(l_scr.shape, F32)
        acc_scr[...] = jnp.zeros(acc_scr.shape, F32)
        p_ref[...] = jnp.zeros(p_ref.shape, BF16)
        a_ref[...] = jnp.ones(a_ref.shape, F32)
    elif stage == "logits":
        off = pl.multiple_of(j * t, t)
        k = jnp.concatenate([ka_ref[pl.ds(off, t), :], kb_ref[pl.ds(off, t), :]], axis=1)
        s = lax.dot_general(k, q, (((1,), (1,)), ((), ())), preferred_element_type=F32)
        s_ref[...] = s
        bmax_ref[...] = jnp.max(s, axis=0, keepdims=True)
    elif stage == "values":
        off = pl.multiple_of(jnp.maximum(j, 0) * t, t)
        pv = jnp.dot(vt_ref[:, pl.ds(off, t)], p_ref[...], preferred_element_type=F32)
        acc_scr[...] = a_ref[...] * acc_scr[...] + pv
    else:
        masked = stage == "softmax_masked"
        s = s_ref[...]
        if masked:
            key = lax.broadcasted_iota(jnp.int32, (t, t), 0)
            qry = lax.broadcasted_iota(jnp.int32, (t, t), 1)
            s = jnp.where((key // unit) <= (qry // unit), s, NEG_INF)
            block_max = jnp.max(s, axis=0, keepdims=True)
        else:
            block_max = bmax_ref[...]
        m_prev = m_scr[...]
        m_new = jnp.maximum(m_prev, block_max)
        alpha = jnp.exp2(m_prev - m_new)
        p = jnp.exp2(s - m_new)
        l_scr[...] = alpha * l_scr[...] + jnp.sum(p, axis=0, keepdims=True)
        m_scr[...] = m_new
        p_ref[...] = p.astype(BF16)
        a_ref[...] = alpha


def _causal_attention(q_a, qa_blk, q_b, qb_blk, k_a, ka_blk, k_b, v_t, heads, b, s, t, unit, name):
    t = min(t, s)
    group = ATTN_HEAD_GROUP
    assert t % unit == 0 and heads % group == 0
    nq = s // t
    once = pl.Buffered(1)
    in_specs, args = [], []
    for g in range(group):
        def q_spec(blk, g=g):
            return pl.BlockSpec((None, t, LANE), lambda bi, hi, i: (blk(hi * group + g), bi * nq + i, 0))

        in_specs.append(q_spec(qa_blk))
        args.append(q_a)
        if q_b is not None:
            in_specs.append(q_spec(qb_blk))
            args.append(q_b)
        in_specs.append(pl.BlockSpec((None, s, LANE), lambda bi, hi, i, g=g: (ka_blk(hi * group + g), bi, 0),
                                     pipeline_mode=once))
        in_specs.append(pl.BlockSpec((LANE, s), lambda bi, hi, i, g=g: (hi * group + g, bi), pipeline_mode=once))
        args += [k_a, v_t]
    in_specs.append(pl.BlockSpec((s, LANE), lambda bi, hi, i: (bi, 0), pipeline_mode=once))
    args.append(k_b)
    stat = pltpu.VMEM((1, t), F32)
    per_head_scratch = [stat, stat, pltpu.VMEM((LANE, t), F32), pltpu.VMEM((t, t), F32),
                        pltpu.VMEM((t, t), BF16), stat, stat]
    assert len(per_head_scratch) == N_ATTN_SCRATCH
    return pl.pallas_call(
        functools.partial(_attn_kernel, t=t, unit=unit, q_ones=q_b is None, group=group),
        grid=(b, heads // group, nq),
        in_specs=in_specs,
        out_specs=pl.BlockSpec((group, t, LANE), lambda bi, hi, i: (hi, bi * nq + i, 0)),
        out_shape=jax.ShapeDtypeStruct((heads, b * s, LANE), BF16),
        scratch_shapes=per_head_scratch * group,
        compiler_params=_params("parallel", "parallel", "arbitrary"),
        name=name,
    )(*args)


def _proj_residual_kernel(*refs, merge, nb):
    if merge:
        oa_ref, ob_ref, ga_ref, gb_ref, w_ref, res_ref, o_ref, a_scr = refs
    else:
        oa_ref, w_ref, res_ref, o_ref, a_scr = refs

    @pl.when(pl.program_id(1) == 0)
    def _():
        for c in range(nb):
            if merge:
                a = (jax.nn.sigmoid(ga_ref[c].astype(F32)) * oa_ref[c].astype(F32)
                     + jax.nn.sigmoid(gb_ref[c].astype(F32)) * ob_ref[c].astype(F32))
            else:
                a = oa_ref[c]
            a_scr[:, c * LANE:(c + 1) * LANE] = a.astype(BF16)

    o_ref[...] = res_ref[...] + jnp.dot(a_scr[...], w_ref[...], preferred_element_type=F32)


def _proj_residual(o_a, w, res, *, tm, tn, merge=None, name=None):
    nb, m, _ = o_a.shape
    k = nb * LANE
    nc = w.shape[1]
    tm = min(tm, m)
    tn = min(tn, nc)
    head_spec = pl.BlockSpec((nb, tm, LANE), lambda i, j: (0, i, 0))
    in_specs = [head_spec]
    args = [o_a]
    if merge is not None:
        o_b, gates, ga_grp, gb_grp = merge
        in_specs += [head_spec,
                     pl.BlockSpec((nb, tm, LANE), lambda i, j: (ga_grp, i, 0)),
                     pl.BlockSpec((nb, tm, LANE), lambda i, j: (gb_grp, i, 0))]
        args += [o_b, gates, gates]
    in_specs += [pl.BlockSpec((k, tn), lambda i, j: (0, j), pipeline_mode=_resident(nc == tn)),
                 pl.BlockSpec((tm, tn), lambda i, j: (i, j))]
    args += [w, res]
    return pl.pallas_call(
        functools.partial(_proj_residual_kernel, merge=merge is not None, nb=nb),
        grid=(m // tm, nc // tn),
        in_specs=in_specs,
        out_specs=pl.BlockSpec((tm, tn), lambda i, j: (i, j)),
        out_shape=jax.ShapeDtypeStruct((m, nc), F32),
        scratch_shapes=[pltpu.VMEM((tm, k), BF16)],
        compiler_params=_params("parallel", "arbitrary"),
        name=name,
    )(*args)


def _cross_kernel(q_ref, mk_ref, mv_ref, o_ref, *, bph):
    for hh in range(MEM_HEADS):
        blocks = range(hh * bph, (hh + 1) * bph)
        q = jnp.concatenate([q_ref[c] for c in blocks], axis=1)
        k = jnp.concatenate([mk_ref[c] for c in blocks], axis=1)
        v = jnp.concatenate([mv_ref[c] for c in blocks], axis=1)
        s = lax.dot_general(q, k, (((1,), (1,)), ((), ())), preferred_element_type=F32)
        p = jnp.exp2(s - jnp.max(s, axis=1, keepdims=True))
        p = p / jnp.sum(p, axis=1, keepdims=True)
        o = jnp.dot(p.astype(BF16), v, preferred_element_type=F32)
        for ci, c in enumerate(blocks):
            o_ref[c] = o[:, ci * LANE:(ci + 1) * LANE].astype(o_ref.dtype)


def _cross_attention(q, mkv, b, s, n_mem, tm):
    nb, n, _ = q.shape
    tm = min(tm, s)
    per_b = s // tm
    return pl.pallas_call(
        functools.partial(_cross_kernel, bph=nb // MEM_HEADS),
        grid=(n // tm,),
        in_specs=[
            pl.BlockSpec((nb, tm, LANE), lambda i: (0, i, 0)),
            pl.BlockSpec((nb, n_mem, LANE), lambda i: (0, i // per_b, 0)),
            pl.BlockSpec((nb, n_mem, LANE), lambda i: (1, i // per_b, 0)),
        ],
        out_specs=pl.BlockSpec((nb, tm, LANE), lambda i: (0, i, 0)),
        out_shape=jax.ShapeDtypeStruct((nb, n, LANE), BF16),
        compiler_params=_params("parallel"),
        name="cross_attention",
    )(q, mkv, mkv)


def _router_kernel(h_ref, g_ref, w_ref, b_ref, z_ref, info_ref, count_ref):
    x = h_ref[...]
    z = x * lax.rsqrt(jnp.mean(x * x, axis=-1, keepdims=True) + EPS) * g_ref[...]
    z_ref[...] = z
    logits = jnp.dot(z, w_ref[...], preferred_element_type=F32, precision=lax.Precision.HIGHEST) + b_ref[...]
    lane = lax.broadcasted_iota(jnp.int32, logits.shape, 1).astype(F32)
    big = float(LANE)

    def first_where(mask):
        return jnp.min(jnp.where(mask, lane, big), axis=1, keepdims=True)

    gmask = lane < N_GROUPS
    gl = jnp.where(gmask, logits, NEG_INF)
    gmax = jnp.max(gl, axis=1, keepdims=True)
    gsum = jnp.sum(jnp.exp(gl - gmax), axis=1, keepdims=True)
    g_idx = first_where(gl == gmax)
    g_w = 1.0 / gsum
    lo = N_GROUPS + g_idx * EXPERTS_PER_GROUP
    emask = (lane >= lo) & (lane < lo + EXPERTS_PER_GROUP)
    el = jnp.where(emask, logits, NEG_INF)
    emax = jnp.max(el, axis=1, keepdims=True)
    eexp = jnp.exp(el - emax)
    prob = jnp.where(emask, eexp / jnp.sum(eexp, axis=1, keepdims=True), -1.0)
    p1 = jnp.max(prob, axis=1, keepdims=True)
    i1 = first_where(prob == p1)
    prob2 = jnp.where(lane == i1, -1.0, prob)
    p2 = jnp.max(prob2, axis=1, keepdims=True)
    i2 = first_where(prob2 == p2)
    denom = p1 + p2
    w1 = g_w * p1 / denom
    w2 = g_w * p2 / denom
    @pl.when(pl.program_id(0) == 0)
    def _():
        count_ref[...] = jnp.zeros_like(count_ref)

    tm = logits.shape[0]
    hot1 = jnp.where(lane == i1, 1.0, 0.0)
    hot2 = jnp.where(lane == i2, 1.0, 0.0)
    hot = hot1 + hot2
    row = lax.broadcasted_iota(jnp.int32, (tm, tm), 0)
    col = lax.broadcasted_iota(jnp.int32, (tm, tm), 1)
    before = jnp.where(col < row, 1.0, 0.0).astype(BF16)
    prior = jnp.dot(before, hot.astype(BF16), preferred_element_type=F32) + count_ref[0:1, :]
    r1 = jnp.sum(prior * hot1, axis=1, keepdims=True)
    r2 = jnp.sum(prior * hot2, axis=1, keepdims=True)
    count_ref[...] = count_ref[...] + jnp.sum(hot, axis=0, keepdims=True)
    info = jnp.zeros_like(logits)
    for k, val in enumerate((i1 - N_GROUPS, i2 - N_GROUPS, w1, w2, r1, r2)):
        info = jnp.where(lane == k, val, info)
    info_ref[...] = info


def _router(h, g, w_r, b_r, tm):
    n, d = h.shape
    tm = min(tm, n)
    return pl.pallas_call(
        _router_kernel,
        grid=(n // tm,),
        in_specs=[pl.BlockSpec((tm, d), lambda i: (i, 0)), pl.BlockSpec((1, d), lambda i: (0, 0)),
                  pl.BlockSpec((d, LANE), lambda i: (0, 0)), pl.BlockSpec((1, LANE), lambda i: (0, 0))],
        out_specs=[pl.BlockSpec((tm, d), lambda i: (i, 0)), pl.BlockSpec((tm, LANE), lambda i: (i, 0)),
                   pl.BlockSpec((8, LANE), lambda i: (0, 0))],
        out_shape=[jax.ShapeDtypeStruct((n, d), F32), jax.ShapeDtypeStruct((n, LANE), F32),
                   jax.ShapeDtypeStruct((8, LANE), F32)],
        compiler_params=_params("arbitrary"),
        name="moe_router",
    )(h, g.reshape(1, d).astype(F32), w_r, b_r)


def _row_copy(src_hbm, row, dst, dst_row, sem):
    return pltpu.make_async_copy(src_hbm.at[pl.ds(row, 1)], dst.at[pl.ds(dst_row, 1)], sem)


def _issue_rows(idx_ref, idx_off, src_hbm, dst, sem, n_rows):
    def issue(r, carry):
        _row_copy(src_hbm, idx_ref[0, idx_off + r], dst, r, sem).start()
        return carry

    lax.fori_loop(0, n_rows, issue, 0, unroll=GATHER_UNROLL)


def _wait_rows(src_hbm, dst, sem, n_rows):
    def wait(r, carry):
        _row_copy(src_hbm, 0, dst, r, sem).wait()
        return carry

    lax.fori_loop(0, n_rows, wait, 0, unroll=GATHER_UNROLL)


def _expert_kernel(blk_e_ref, nvalid_ref, tok0_ref, tok_next_ref, z_hbm, wg_ref, wu_ref, wd_ref, y_ref, xbuf, sem,
                   *, tb):
    i = pl.program_id(0)
    nvalid = nvalid_ref[0]
    slot = i % 2

    @pl.when((i == 0) & (nvalid > 0))
    def _():
        _issue_rows(tok0_ref, 0, z_hbm, xbuf.at[0], sem.at[0], tb)

    @pl.when(i + 1 < nvalid)
    def _():
        _issue_rows(tok_next_ref, 0, z_hbm, xbuf.at[1 - slot], sem.at[1 - slot], tb)

    @pl.when(i < nvalid)
    def _():
        _wait_rows(z_hbm, xbuf.at[slot], sem.at[slot], tb)
        x = xbuf[slot].astype(BF16)
        gate = jnp.dot(x, wg_ref[...], preferred_element_type=F32)
        up = jnp.dot(x, wu_ref[...], preferred_element_type=F32)
        hidden = (gate * jax.nn.sigmoid(gate) * up).astype(BF16)
        y_ref[...] = jnp.dot(hidden, wd_ref[...], preferred_element_type=F32)

    @pl.when(i >= nvalid)
    def _():
        y_ref[...] = jnp.zeros_like(y_ref)


def _expert_blocks(blk_e, nvalid, tok, z, wg, wu, wd, tb):
    nblk = tok.shape[0]
    d = z.shape[1]
    ff = wg.shape[2]
    grid_spec = pltpu.PrefetchScalarGridSpec(
        num_scalar_prefetch=2,
        grid=(nblk,),
        in_specs=[
            pl.BlockSpec((None, 1, tb), lambda i, be, nv: (0, 0, 0), memory_space=pltpu.SMEM),
            pl.BlockSpec((None, 1, tb), lambda i, be, nv: (jnp.minimum(i + 1, nblk - 1), 0, 0),
                         memory_space=pltpu.SMEM),
            pl.BlockSpec(memory_space=pl.ANY),
            pl.BlockSpec((None, d, ff), lambda i, be, nv: (be[i], 0, 0)),
            pl.BlockSpec((None, d, ff), lambda i, be, nv: (be[i], 0, 0)),
            pl.BlockSpec((None, ff, d), lambda i, be, nv: (be[i], 0, 0)),
        ],
        out_specs=pl.BlockSpec((tb, d), lambda i, be, nv: (i, 0)),
        scratch_shapes=[pltpu.VMEM((2, tb, d), F32), pltpu.SemaphoreType.DMA((2,))],
    )
    return pl.pallas_call(
        functools.partial(_expert_kernel, tb=tb),
        grid_spec=grid_spec,
        out_shape=jax.ShapeDtypeStruct((nblk * tb, d), F32),
        compiler_params=_params("arbitrary"),
        name="moe_experts",
    )(blk_e, nvalid, tok, tok, z, wg, wu, wd)


def _combine_kernel(dest0_ref, dest_next_ref, y_hbm, h_ref, info_ref, g_ref, o_ref, ybuf, sem, *, tm, final_norm):
    i = pl.program_id(0)
    slot = i % 2

    def issue(idx_ref, sl):
        for k in range(2):
            _issue_rows(idx_ref, k * tm, y_hbm, ybuf.at[sl, k], sem.at[sl], tm)

    @pl.when(i == 0)
    def _():
        issue(dest0_ref, 0)

    @pl.when(i + 1 < pl.num_programs(0))
    def _():
        issue(dest_next_ref, 1 - slot)

    for k in range(2):
        _wait_rows(y_hbm, ybuf.at[slot, k], sem.at[slot], tm)
    info = info_ref[...]
    moe = ybuf[slot, 0] * info[:, 2:3] + ybuf[slot, 1] * info[:, 3:4]
    hx = h_ref[...] + moe
    if final_norm:
        hx = hx * lax.rsqrt(jnp.mean(hx * hx, axis=-1, keepdims=True) + EPS) * g_ref[...]
    o_ref[...] = hx


def _combine(dest, y, h, info, g_final, tm, final_norm):
    n, d = h.shape
    n_tiles = n // tm
    return pl.pallas_call(
        functools.partial(_combine_kernel, tm=tm, final_norm=final_norm),
        grid=(n_tiles,),
        in_specs=[
            pl.BlockSpec((None, 1, 2 * tm), lambda i: (0, 0, 0), memory_space=pltpu.SMEM),
            pl.BlockSpec((None, 1, 2 * tm), lambda i: (jnp.minimum(i + 1, n_tiles - 1), 0, 0),
                         memory_space=pltpu.SMEM),
            pl.BlockSpec(memory_space=pl.ANY),
            pl.BlockSpec((tm, d), lambda i: (i, 0)),
            pl.BlockSpec((tm, LANE), lambda i: (i, 0)),
            pl.BlockSpec((1, d), lambda i: (0, 0)),
        ],
        out_specs=pl.BlockSpec((tm, d), lambda i: (i, 0)),
        out_shape=jax.ShapeDtypeStruct((n, d), F32),
        scratch_shapes=[pltpu.VMEM((2, 2, tm, d), F32), pltpu.SemaphoreType.DMA((2,))],
        compiler_params=_params("arbitrary"),
        name="moe_combine_final_norm",
    )(dest, dest, y, h, info, g_final.reshape(1, d).astype(F32))


def _dispatch_plan(eid, rank, counts, n, tb):
    n_slots = 2 * n
    nblk = n_slots // tb + N_EXPERTS
    slot_e = eid.reshape(-1)
    rank = rank.reshape(-1)
    blocks_per_e = (counts + tb - 1) // tb
    blk_end = jnp.cumsum(blocks_per_e)
    blk_start = blk_end - blocks_per_e
    dest = blk_start[slot_e] * tb + rank
    slot_tok = jnp.arange(n_slots, dtype=jnp.int32) // 2
    tok = jnp.zeros((nblk * tb,), jnp.int32).at[dest].set(slot_tok)
    blk_e = jnp.minimum(jnp.searchsorted(blk_end, jnp.arange(nblk, dtype=jnp.int32), side="right"),
                        N_EXPERTS - 1).astype(jnp.int32)
    nvalid = blk_end[-1:].astype(jnp.int32)
    return blk_e, nvalid, tok.reshape(nblk, 1, tb), dest.reshape(n, 2)


def _rope_tables(positions):
    half = MLA_ROPE // 2
    inv_freq = ROPE_THETA ** (-jnp.arange(half, dtype=F32) / half)
    ang = positions.astype(F32).reshape(-1, 1) * inv_freq[None, :]
    cos, sin = jnp.cos(ang), jnp.sin(ang)
    zero = jnp.zeros_like(cos)
    cos_t = jnp.concatenate([cos, cos, zero, zero], axis=1)
    sin_lo = jnp.concatenate([-sin, zero, zero, zero], axis=1)
    sin_hi = jnp.concatenate([zero, sin, zero, zero], axis=1)
    return cos_t, sin_lo, sin_hi


def kernel(x, mem, positions, g_mix, w_in, b_forget, g_q_latent, g_kv_latent, w_q_up, w_k_up, w_v_up, w_out, g_cross, g_mem, w_cross_q, w_cross_kv, w_cross_out, g_moe, w_group_router, b_group_router, w_expert_router, b_expert_router, w_exp_gate, w_exp_up, w_exp_down, g_final):
    b, s, d = x.shape
    n = b * s
    n_mem = mem.shape[1]
    depth = w_in.shape[0]
    fw = FOX_HEADS * LANE
    h = x.reshape(n, d)
    rope_tabs = _rope_tables(positions)

    for l in range(depth):
        wi = w_in[l]
        fox_scale = (LANE ** -0.5) * LOG2E
        o_f = 3 * fw
        o_cq = o_f + FOX_HEADS
        o_kr = o_cq + 2 * MLA_RANK
        o_g = o_kr + MLA_ROPE
        w_big = jnp.concatenate([wi[:, :fw] * fox_scale, wi[:, fw:2 * fw], wi[:, o_g:]], axis=1).astype(BF16)
        w_fox_v_t = wi[:, 2 * fw:3 * fw].T.astype(BF16)
        w_small = jnp.concatenate(
            [wi[:, o_cq:o_kr], wi[:, o_kr:o_g], wi[:, o_f:o_cq],
             jnp.zeros((d, LANE - MLA_ROPE - FOX_HEADS), F32)], axis=1).astype(BF16)
        mla_scale = ((MLA_NOPE + MLA_ROPE) ** -0.5) * LOG2E
        wq = (w_q_up[l] * mla_scale).reshape(MLA_RANK, MLA_HEADS, MLA_NOPE + MLA_ROPE)
        wq = jnp.pad(wq, ((0, 0), (0, 0), (0, 2 * LANE - MLA_NOPE - MLA_ROPE))).reshape(MLA_RANK, -1).astype(BF16)
        wk = w_k_up[l].astype(BF16)
        w_mla_v_t = w_v_up[l].T.astype(BF16)

        big = _norm_matmul(h, 0, d, g_mix[l], w_big, blocked=True, out_dtype=BF16, tm=PROJ_TM, tn=PROJ_TN,
                           name="in_proj")
        small = _norm_matmul(h, 0, d, g_mix[l], w_small, blocked=False, out_dtype=F32, tm=PROJ_TM, tn=384,
                             name="in_proj_small")
        nh = FOX_HEADS
        fox_v_t = _norm_matmul_t(h, 0, d, g_mix[l], w_fox_v_t, tm=PROJ_TM, tr=PROJ_TN, name="fox_v_t")
        bias = _forget_bias(small, 2 * MLA_RANK // LANE, MLA_ROPE, b_forget[l], b, s, 512)
        o_a = _causal_attention(big, lambda hd: hd, None, None, big, lambda hd: nh + hd, bias,
                                fox_v_t, nh, b, s, ATTN_TILE, 1, "fox_attention")
        qm = _norm_matmul(small, 0, MLA_RANK, g_q_latent[l], wq, blocked=True, out_dtype=BF16, tm=PROJ_TM,
                          tn=PROJ_TN, rope_tabs=rope_tabs, name="mla_q_up")
        k_nope = _norm_matmul(small, 1, MLA_RANK, g_kv_latent[l], wk, blocked=True, out_dtype=BF16, tm=PROJ_TM,
                              tn=PROJ_TN, name="mla_k_up")
        mla_v_t = _norm_matmul_t(small, 1, MLA_RANK, g_kv_latent[l], w_mla_v_t, tm=PROJ_TM, tr=PROJ_TN,
                                 name="mla_v_t")
        k_rope = _krope(small, 2 * MLA_RANK // LANE, rope_tabs, 512)
        o_b = _causal_attention(qm, lambda hd: 2 * hd, qm, lambda hd: 2 * hd + 1, k_nope, lambda hd: hd,
                                k_rope, mla_v_t, MLA_HEADS, b, s, ATTN_TILE, CHUNK, "mla_attention")
        h = _proj_residual(o_a, w_out[l].astype(BF16), h, tm=MERGE_TM, tn=d, merge=(o_b, big, 2, 3),
                           name="out_proj")
        cross_scale = ((d // MEM_HEADS) ** -0.5) * LOG2E
        q_c = _norm_matmul(h, 0, d, g_cross[l], (w_cross_q[l] * cross_scale).astype(BF16), blocked=True,
                           out_dtype=BF16, tm=PROJ_TM, tn=d, name="cross_q")
        mkv = _norm_matmul(mem.reshape(b * n_mem, d), 0, d, g_mem[l], w_cross_kv[l].astype(BF16), blocked=True,
                           out_dtype=BF16, tm=PROJ_TM, tn=PROJ_TN, name="cross_mem_kv")
        o_c = _cross_attention(q_c, mkv, b, s, n_mem, 512)
        h = _proj_residual(o_c, w_cross_out[l].astype(BF16), h, tm=MERGE_TM, tn=d, name="cross_out")
        w_r = jnp.concatenate([w_group_router[l], w_expert_router[l],
                               jnp.zeros((d, LANE - N_GROUPS - N_EXPERTS), F32)], axis=1)
        b_r = jnp.concatenate([b_group_router[l], b_expert_router[l],
                               jnp.zeros((LANE - N_GROUPS - N_EXPERTS,), F32)]).reshape(1, LANE)
        z, info, counts = _router(h, g_moe[l], w_r, b_r, 256)
        tb = 256
        eid = info[:, 0:2].astype(jnp.int32)
        rank = info[:, 4:6].astype(jnp.int32)
        counts = counts[0, N_GROUPS:N_GROUPS + N_EXPERTS].astype(jnp.int32)
        blk_e, nvalid, tok, dest = _dispatch_plan(eid, rank, counts, n, tb)
        y = _expert_blocks(blk_e, nvalid, tok, z, w_exp_gate[l].astype(BF16), w_exp_up[l].astype(BF16),
                           w_exp_down[l].astype(BF16), tb)
        tm_c = min(256, n)
        dest_blk = dest.reshape(n // tm_c, tm_c, 2).transpose(0, 2, 1).reshape(n // tm_c, 1, 2 * tm_c)
        h = _combine(dest_blk, y, h, info, g_final, tm_c, final_norm=(l + 1 == depth))
    return h.reshape(b, s, d)
```

```python
import functools
import math

import jax
import jax.numpy as jnp
from jax import lax
from jax.experimental import pallas as pl
from jax.experimental.pallas import tpu as pltpu

F32 = jnp.float32
BF16 = jnp.bfloat16

LANE = 128
EPS = 1e-6
CHUNK = 64
FOX_HEADS = 16
MLA_HEADS = 16
MLA_NOPE = 128
MLA_ROPE = 64
MLA_RANK = 512
ROPE_THETA = 10000.0
MEM_HEADS = 4
N_GROUPS = 8
EXPERTS_PER_GROUP = 8
N_EXPERTS = N_GROUPS * EXPERTS_PER_GROUP
LOG2E = math.log2(math.e)
VMEM_LIMIT = 56 * 1024 * 1024
NEG_INF = float("-inf")
ATTN_TILE = 1024
ATTN_HEAD_GROUP = 2
PROJ_TM = 1024
PROJ_TN = 1024
MERGE_TM = 512
GATHER_UNROLL = 8


def _params(*sem):
    return pltpu.CompilerParams(dimension_semantics=sem, vmem_limit_bytes=VMEM_LIMIT)


def _resident(whole):
    return pl.Buffered(1) if whole else None


def _rope_rotate(x, cos, sin_lo, sin_hi):
    return x * cos + pltpu.roll(x, 96, 1) * sin_lo + pltpu.roll(x, 32, 1) * sin_hi


def _norm_matmul_kernel(*refs, rope, blocked, nblk):
    if rope:
        x_ref, g_ref, w_ref, cos_ref, slo_ref, shi_ref, o_ref, xn_ref = refs
    else:
        x_ref, g_ref, w_ref, o_ref, xn_ref = refs

    @pl.when(pl.program_id(1) == 0)
    def _():
        x = x_ref[...].astype(F32)
        y = x * lax.rsqrt(jnp.mean(x * x, axis=-1, keepdims=True) + EPS)
        xn_ref[...] = (y * g_ref[...]).astype(BF16)

    acc = jnp.dot(xn_ref[...], w_ref[...], preferred_element_type=F32)
    if blocked:
        for c in range(nblk):
            blk = acc[:, c * LANE:(c + 1) * LANE]
            if rope and c % 2 == 1:
                blk = _rope_rotate(blk, cos_ref[...], slo_ref[...], shi_ref[...])
            o_ref[c] = blk.astype(o_ref.dtype)
    else:
        o_ref[...] = acc.astype(o_ref.dtype)


def _norm_matmul(x, x_col, k, g, w, *, blocked, out_dtype, tm, tn, rope_tabs=None, name=None):
    m = x.shape[0]
    nc = w.shape[1]
    tm = min(tm, m)
    tn = min(tn, nc)
    assert m % tm == 0 and nc % tn == 0 and tn % LANE == 0
    nblk = tn // LANE
    in_specs = [
        pl.BlockSpec((tm, k), lambda i, j: (i, x_col)),
        pl.BlockSpec((1, k), lambda i, j: (0, 0)),
        pl.BlockSpec((k, tn), lambda i, j: (0, j), pipeline_mode=_resident(nc == tn)),
    ]
    args = [x, g.reshape(1, k).astype(F32), w]
    if rope_tabs is not None:
        in_specs += [pl.BlockSpec((tm, LANE), lambda i, j: (i, 0))] * 3
        args += list(rope_tabs)
    if blocked:
        out_shape = jax.ShapeDtypeStruct((nc // LANE, m, LANE), out_dtype)
        out_spec = pl.BlockSpec((nblk, tm, LANE), lambda i, j: (j, i, 0))
    else:
        out_shape = jax.ShapeDtypeStruct((m, nc), out_dtype)
        out_spec = pl.BlockSpec((tm, tn), lambda i, j: (i, j))
    return pl.pallas_call(
        functools.partial(_norm_matmul_kernel, rope=rope_tabs is not None, blocked=blocked, nblk=nblk),
        grid=(m // tm, nc // tn),
        in_specs=in_specs,
        out_specs=out_spec,
        out_shape=out_shape,
        scratch_shapes=[pltpu.VMEM((tm, k), BF16)],
        compiler_params=_params("parallel", "arbitrary"),
        name=name,
    )(*args)


def _krope_kernel(x_ref, cos_ref, slo_ref, shi_ref, o_ref):
    o_ref[...] = _rope_rotate(x_ref[...], cos_ref[...], slo_ref[...], shi_ref[...]).astype(o_ref.dtype)


def _krope(small, col_blk, rope_tabs, tm):
    m = small.shape[0]
    tm = min(tm, m)
    return pl.pallas_call(
        _krope_kernel,
        grid=(m // tm,),
        in_specs=[pl.BlockSpec((tm, LANE), lambda i: (i, col_blk))] + [pl.BlockSpec((tm, LANE), lambda i: (i, 0))] * 3,
        out_specs=pl.BlockSpec((tm, LANE), lambda i: (i, 0)),
        out_shape=jax.ShapeDtypeStruct((m, LANE), BF16),
        compiler_params=_params("parallel"),
        name="k_rope",
    )(small, *rope_tabs)


N_SPLIT = 3


def _forget_bias_kernel(f_ref, b_ref, o_ref, carry_ref, *, ts, lane0, heads):
    @pl.when(pl.program_id(1) == 0)
    def _():
        carry_ref[...] = jnp.zeros_like(carry_ref)

    x = f_ref[...] + b_ref[...]
    log_f = jnp.minimum(x, 0.0) - jnp.log1p(jnp.exp(-jnp.abs(x)))
    row = lax.broadcasted_iota(jnp.int32, (ts, ts), 0)
    col = lax.broadcasted_iota(jnp.int32, (ts, ts), 1)
    lower = (col <= row).astype(F32)
    c = jnp.dot(lower, log_f, preferred_element_type=F32, precision=lax.Precision.HIGHEST) + carry_ref[0:1, :]
    carry_ref[...] = jnp.broadcast_to(c[ts - 1:ts, :], carry_ref.shape)
    rest = -(c * LOG2E)
    pieces = []
    for _ in range(N_SPLIT):
        piece = rest.astype(BF16).astype(F32)
        pieces.append(piece)
        rest = rest - piece
    lane = lax.broadcasted_iota(jnp.int32, (ts, LANE), 1)
    out = jnp.zeros((ts, LANE), F32)
    for hd in range(heads):
        for pi, piece in enumerate(pieces):
            out = jnp.where(lane == N_SPLIT * hd + pi, piece[:, lane0 + hd:lane0 + hd + 1], out)
    o_ref[...] = out.astype(o_ref.dtype)


def _forget_bias(small, col_blk, lane0, b_forget, b, s, ts):
    heads = b_forget.shape[0]
    assert N_SPLIT * heads <= LANE
    ts = min(ts, s)
    per_b = s // ts
    b_row = jnp.zeros((1, LANE), F32).at[0, lane0:lane0 + heads].set(b_forget.astype(F32))
    return pl.pallas_call(
        functools.partial(_forget_bias_kernel, ts=ts, lane0=lane0, heads=heads),
        grid=(b, per_b),
        in_specs=[pl.BlockSpec((ts, LANE), lambda bi, j: (bi * per_b + j, col_blk)),
                  pl.BlockSpec((1, LANE), lambda bi, j: (0, 0))],
        out_specs=pl.BlockSpec((ts, LANE), lambda bi, j: (bi * per_b + j, 0)),
        out_shape=jax.ShapeDtypeStruct((b * s, LANE), BF16),
        scratch_shapes=[pltpu.VMEM((8, LANE), F32)],
        compiler_params=_params("parallel", "arbitrary"),
        name="forget_bias",
    )(small, b_row)


def _norm_matmul_t_kernel(x_ref, g_ref, w_ref, o_ref, xn_ref):
    @pl.when(pl.program_id(1) == 0)
    def _():
        x = x_ref[...].astype(F32)
        y = x * lax.rsqrt(jnp.mean(x * x, axis=-1, keepdims=True) + EPS)
        xn_ref[...] = (y * g_ref[...]).astype(BF16)

    o_ref[...] = lax.dot_general(w_ref[...], xn_ref[...], (((1,), (1,)), ((), ())),
                                 preferred_element_type=F32).astype(o_ref.dtype)


def _norm_matmul_t(x, x_col, k, g, w_t, *, tm, tr, name=None):
    m = x.shape[0]
    r = w_t.shape[0]
    tm = min(tm, m)
    tr = min(tr, r)
    return pl.pallas_call(
        _norm_matmul_t_kernel,
        grid=(m // tm, r // tr),
        in_specs=[pl.BlockSpec((tm, k), lambda i, j: (i, x_col)), pl.BlockSpec((1, k), lambda i, j: (0, 0)),
                  pl.BlockSpec((tr, k), lambda i, j: (j, 0))],
        out_specs=pl.BlockSpec((tr, tm), lambda i, j: (j, i)),
        out_shape=jax.ShapeDtypeStruct((r, m), BF16),
        scratch_shapes=[pltpu.VMEM((tm, k), BF16)],
        compiler_params=_params("parallel", "arbitrary"),
        name=name,
    )(x, g.reshape(1, k).astype(F32), w_t)


N_ATTN_SCRATCH = 7


def _attn_kernel(*refs, t, unit, q_ones, group):
    n_in = 3 if q_ones else 4
    kb_ref = refs[group * n_in]
    o_ref = refs[group * n_in + 1]
    scratch = refs[group * n_in + 2:]
    i = pl.program_id(2)
    streams = []
    for g in range(group):
        head_refs = refs[g * n_in:(g + 1) * n_in]
        if q_ones:
            qa_ref, ka_ref, vt_ref = head_refs
            lane = lax.broadcasted_iota(jnp.int32, (t, LANE), 1)
            lo = N_SPLIT * (pl.program_id(1) * group + g)
            q_b = jnp.where((lane >= lo) & (lane < lo + N_SPLIT), 1.0, 0.0).astype(BF16)
        else:
            qa_ref, qb_ref, ka_ref, vt_ref = head_refs
            q_b = qb_ref[...]
        q = jnp.concatenate([qa_ref[...], q_b], axis=1)
        streams.append((q, ka_ref, vt_ref) + tuple(scratch[g * N_ATTN_SCRATCH:(g + 1) * N_ATTN_SCRATCH]))

    def stage(name, j=None):
        for st in streams:
            _attn_stream(st, kb_ref, j, t, unit, name)

    def body(tau, carry):
        stage("values", tau - 2)
        stage("softmax")
        stage("logits", tau)
        return carry

    stage("init")
    stage("logits", 0)
    lax.fori_loop(1, i + 1, body, 0)
    stage("values", i - 1)
    stage("softmax_masked")
    stage("values", i)
    for g, st in enumerate(streams):
        o_ref[g] = jnp.transpose(st[5][...] / st[4][...]).astype(o_ref.dtype)


def _attn_stream(st, kb_ref, j, t, unit, stage):
    q, ka_ref, vt_ref, m_scr, l_scr, acc_scr, s_ref, p_ref, a_ref, bmax_ref = st
    if stage == "init":
        m_scr[...] = jnp.full(m_scr.shape, NEG_INF, F32)
        l_scr[...] = jnp.zeros(l_scr.shape, F32)
        acc_scr[...] = jnp.zeros(acc_scr.shape, F32)
        p_ref[...] = jnp.zeros(p_ref.shape, BF16)
        a_ref[...] = jnp.ones(a_ref.shape, F32)
    elif stage == "logits":
        off = pl.multiple_of(j * t, t)
        k = jnp.concatenate([ka_ref[pl.ds(off, t), :], kb_ref[pl.ds(off, t), :]], axis=1)
        s = lax.dot_general(k, q, (((1,), (1,)), ((), ())), preferred_element_type=F32)
        s_ref[...] = s
        bmax_ref[...] = jnp.max(s, axis=0, keepdims=True)
    elif stage == "values":
        off = pl.multiple_of(jnp.maximum(j, 0) * t, t)
        pv = jnp.dot(vt_ref[:, pl.ds(off, t)], p_ref[...], preferred_element_type=F32)
        acc_scr[...] = a_ref[...] * acc_scr[...] + pv
    else:
        masked = stage == "softmax_masked"
        s = s_ref[...]
        if masked:
            key = lax.broadcasted_iota(jnp.int32, (t, t), 0)
            qry = lax.broadcasted_iota(jnp.int32, (t, t), 1)
            s = jnp.where((key // unit) <= (qry // unit), s, NEG_INF)
            block_max = jnp.max(s, axis=0, keepdims=True)
        else:
            block_max = bmax_ref[...]
        m_prev = m_scr[...]
        m_new = jnp.maximum(m_prev, block_max)
        alpha = jnp.exp2(m_prev - m_new)
        p = jnp.exp2(s - m_new)
        l_scr[...] = alpha * l_scr[...] + jnp.sum(p, axis=0, keepdims=True)
        m_scr[...] = m_new
        p_ref[...] = p.astype(BF16)
        a_ref[...] = alpha


def _causal_attention(q_a, qa_blk, q_b, qb_blk, k_a, ka_blk, k_b, v_t, heads, b, s, t, unit, name):
    t = min(t, s)
    group = ATTN_HEAD_GROUP
    assert t % unit == 0 and heads % group == 0
    nq = s // t
    once = pl.Buffered(1)
    in_specs, args = [], []
    for g in range(group):
        def q_spec(blk, g=g):
            return pl.BlockSpec((None, t, LANE), lambda bi, hi, i: (blk(hi * group + g), bi * nq + i, 0))

        in_specs.append(q_spec(qa_blk))
        args.append(q_a)
        if q_b is not None:
            in_specs.append(q_spec(qb_blk))
            args.append(q_b)
        in_specs.append(pl.BlockSpec((None, s, LANE), lambda bi, hi, i, g=g: (ka_blk(hi * group + g), bi, 0),
                                     pipeline_mode=once))
        in_specs.append(pl.BlockSpec((LANE, s), lambda bi, hi, i, g=g: (hi * group + g, bi), pipeline_mode=once))
        args += [k_a, v_t]
    in_specs.append(pl.BlockSpec((s, LANE), lambda bi, hi, i: (bi, 0), pipeline_mode=once))
    args.append(k_b)
    stat = pltpu.VMEM((1, t), F32)
    per_head_scratch = [stat, stat, pltpu.VMEM((LANE, t), F32), pltpu.VMEM((t, t), F32),
                        pltpu.VMEM((t, t), BF16), stat, stat]
    assert len(per_head_scratch) == N_ATTN_SCRATCH
    return pl.pallas_call(
        functools.partial(_attn_kernel, t=t, unit=unit, q_ones=q_b is None, group=group),
        grid=(b, heads // group, nq),
        in_specs=in_specs,
        out_specs=pl.BlockSpec((group, t, LANE), lambda bi, hi, i: (hi, bi * nq + i, 0)),
        out_shape=jax.ShapeDtypeStruct((heads, b * s, LANE), BF16),
        scratch_shapes=per_head_scratch * group,
        compiler_params=_params("parallel", "parallel", "arbitrary"),
        name=name,
    )(*args)


def _proj_residual_kernel(*refs, merge, nb):
    if merge:
        oa_ref, ob_ref, ga_ref, gb_ref, w_ref, res_ref, o_ref, a_scr = refs
    else:
        oa_ref, w_ref, res_ref, o_ref, a_scr = refs

    @pl.when(pl.program_id(1) == 0)
    def _():
        for c in range(nb):
            if merge:
                a = (jax.nn.sigmoid(ga_ref[c].astype(F32)) * oa_ref[c].astype(F32)
                     + jax.nn.sigmoid(gb_ref[c].astype(F32)) * ob_ref[c].astype(F32))
            else:
                a = oa_ref[c]
            a_scr[:, c * LANE:(c + 1) * LANE] = a.astype(BF16)

    o_ref[...] = res_ref[...] + jnp.dot(a_scr[...], w_ref[...], preferred_element_type=F32)


def _proj_residual(o_a, w, res, *, tm, tn, merge=None, name=None):
    nb, m, _ = o_a.shape
    k = nb * LANE
    nc = w.shape[1]
    tm = min(tm, m)
    tn = min(tn, nc)
    head_spec = pl.BlockSpec((nb, tm, LANE), lambda i, j: (0, i, 0))
    in_specs = [head_spec]
    args = [o_a]
    if merge is not None:
        o_b, gates, ga_grp, gb_grp = merge
        in_specs += [head_spec,
                     pl.BlockSpec((nb, tm, LANE), lambda i, j: (ga_grp, i, 0)),
                     pl.BlockSpec((nb, tm, LANE), lambda i, j: (gb_grp, i, 0))]
        args += [o_b, gates, gates]
    in_specs += [pl.BlockSpec((k, tn), lambda i, j: (0, j), pipeline_mode=_resident(nc == tn)),
                 pl.BlockSpec((tm, tn), lambda i, j: (i, j))]
    args += [w, res]
    return pl.pallas_call(
        functools.partial(_proj_residual_kernel, merge=merge is not None, nb=nb),
        grid=(m // tm, nc // tn),
        in_specs=in_specs,
        out_specs=pl.BlockSpec((tm, tn), lambda i, j: (i, j)),
        out_shape=jax.ShapeDtypeStruct((m, nc), F32),
        scratch_shapes=[pltpu.VMEM((tm, k), BF16)],
        compiler_params=_params("parallel", "arbitrary"),
        name=name,
    )(*args)


def _cross_kernel(q_ref, mk_ref, mv_ref, o_ref, *, bph):
    for hh in range(MEM_HEADS):
        blocks = range(hh * bph, (hh + 1) * bph)
        q = jnp.concatenate([q_ref[c] for c in blocks], axis=1)
        k = jnp.concatenate([mk_ref[c] for c in blocks], axis=1)
        v = jnp.concatenate([mv_ref[c] for c in blocks], axis=1)
        s = lax.dot_general(q, k, (((1,), (1,)), ((), ())), preferred_element_type=F32)
        p = jnp.exp2(s - jnp.max(s, axis=1, keepdims=True))
        p = p / jnp.sum(p, axis=1, keepdims=True)
        o = jnp.dot(p.astype(BF16), v, preferred_element_type=F32)
        for ci, c in enumerate(blocks):
            o_ref[c] = o[:, ci * LANE:(ci + 1) * LANE].astype(o_ref.dtype)


def _cross_attention(q, mkv, b, s, n_mem, tm):
    nb, n, _ = q.shape
    tm = min(tm, s)
    per_b = s // tm
    return pl.pallas_call(
        functools.partial(_cross_kernel, bph=nb // MEM_HEADS),
        grid=(n // tm,),
        in_specs=[
            pl.BlockSpec((nb, tm, LANE), lambda i: (0, i, 0)),
            pl.BlockSpec((nb, n_mem, LANE), lambda i: (0, i // per_b, 0)),
            pl.BlockSpec((nb, n_mem, LANE), lambda i: (1, i // per_b, 0)),
        ],
        out_specs=pl.BlockSpec((nb, tm, LANE), lambda i: (0, i, 0)),
        out_shape=jax.ShapeDtypeStruct((nb, n, LANE), BF16),
        compiler_params=_params("parallel"),
        name="cross_attention",
    )(q, mkv, mkv)


def _router_kernel(h_ref, g_ref, w_ref, b_ref, z_ref, info_ref, count_ref):
    x = h_ref[...]
    z = x * lax.rsqrt(jnp.mean(x * x, axis=-1, keepdims=True) + EPS) * g_ref[...]
    z_ref[...] = z
    logits = jnp.dot(z, w_ref[...], preferred_element_type=F32, precision=lax.Precision.HIGHEST) + b_ref[...]
    lane = lax.broadcasted_iota(jnp.int32, logits.shape, 1).astype(F32)
    big = float(LANE)

    def first_where(mask):
        return jnp.min(jnp.where(mask, lane, big), axis=1, keepdims=True)

    gmask = lane < N_GROUPS
    gl = jnp.where(gmask, logits, NEG_INF)
    gmax = jnp.max(gl, axis=1, keepdims=True)
    gsum = jnp.sum(jnp.exp(gl - gmax), axis=1, keepdims=True)
    g_idx = first_where(gl == gmax)
    g_w = 1.0 / gsum
    lo = N_GROUPS + g_idx * EXPERTS_PER_GROUP
    emask = (lane >= lo) & (lane < lo + EXPERTS_PER_GROUP)
    el = jnp.where(emask, logits, NEG_INF)
    emax = jnp.max(el, axis=1, keepdims=True)
    eexp = jnp.exp(el - emax)
    prob = jnp.where(emask, eexp / jnp.sum(eexp, axis=1, keepdims=True), -1.0)
    p1 = jnp.max(prob, axis=1, keepdims=True)
    i1 = first_where(prob == p1)
    prob2 = jnp.where(lane == i1, -1.0, prob)
    p2 = jnp.max(prob2, axis=1, keepdims=True)
    i2 = first_where(prob2 == p2)
    denom = p1 + p2
    w1 = g_w * p1 / denom
    w2 = g_w * p2 / denom
    @pl.when(pl.program_id(0) == 0)
    def _():
        count_ref[...] = jnp.zeros_like(count_ref)

    tm = logits.shape[0]
    hot1 = jnp.where(lane == i1, 1.0, 0.0)
    hot2 = jnp.where(lane == i2, 1.0, 0.0)
    hot = hot1 + hot2
    row = lax.broadcasted_iota(jnp.int32, (tm, tm), 0)
    col = lax.broadcasted_iota(jnp.int32, (tm, tm), 1)
    before = jnp.where(col < row, 1.0, 0.0).astype(BF16)
    prior = jnp.dot(before, hot.astype(BF16), preferred_element_type=F32) + count_ref[0:1, :]
    r1 = jnp.sum(prior * hot1, axis=1, keepdims=True)
    r2 = jnp.sum(prior * hot2, axis=1, keepdims=True)
    count_ref[...] = count_ref[...] + jnp.sum(hot, axis=0, keepdims=True)
    info = jnp.zeros_like(logits)
    for k, val in enumerate((i1 - N_GROUPS, i2 - N_GROUPS, w1, w2, r1, r2)):
        info = jnp.where(lane == k, val, info)
    info_ref[...] = info


def _router(h, g, w_r, b_r, tm):
    n, d = h.shape
    tm = min(tm, n)
    return pl.pallas_call(
        _router_kernel,
        grid=(n // tm,),
        in_specs=[pl.BlockSpec((tm, d), lambda i: (i, 0)), pl.BlockSpec((1, d), lambda i: (0, 0)),
                  pl.BlockSpec((d, LANE), lambda i: (0, 0)), pl.BlockSpec((1, LANE), lambda i: (0, 0))],
        out_specs=[pl.BlockSpec((tm, d), lambda i: (i, 0)), pl.BlockSpec((tm, LANE), lambda i: (i, 0)),
                   pl.BlockSpec((8, LANE), lambda i: (0, 0))],
        out_shape=[jax.ShapeDtypeStruct((n, d), F32), jax.ShapeDtypeStruct((n, LANE), F32),
                   jax.ShapeDtypeStruct((8, LANE), F32)],
        compiler_params=_params("arbitrary"),
        name="moe_router",
    )(h, g.reshape(1, d).astype(F32), w_r, b_r)


def _row_copy(src_hbm, row, dst, dst_row, sem):
    return pltpu.make_async_copy(src_hbm.at[pl.ds(row, 1)], dst.at[pl.ds(dst_row, 1)], sem)


def _issue_rows(idx_ref, idx_off, src_hbm, dst, sem, n_rows):
    def issue(r, carry):
        _row_copy(src_hbm, idx_ref[0, idx_off + r], dst, r, sem).start()
        return carry

    lax.fori_loop(0, n_rows, issue, 0, unroll=GATHER_UNROLL)


def _wait_rows(src_hbm, dst, sem, n_rows):
    def wait(r, carry):
        _row_copy(src_hbm, 0, dst, r, sem).wait()
        return carry

    lax.fori_loop(0, n_rows, wait, 0, unroll=GATHER_UNROLL)


def _expert_kernel(blk_e_ref, nvalid_ref, tok0_ref, tok_next_ref, z_hbm, wg_ref, wu_ref, wd_ref, y_ref, xbuf, sem,
                   *, tb):
    i = pl.program_id(0)
    nvalid = nvalid_ref[0]
    slot = i % 2

    @pl.when((i == 0) & (nvalid > 0))
    def _():
        _issue_rows(tok0_ref, 0, z_hbm, xbuf.at[0], sem.at[0], tb)

    @pl.when(i < nvalid)
    def _():
        _wait_rows(z_hbm, xbuf.at[slot], sem.at[slot], tb)
        x = xbuf[slot].astype(BF16)
        for r in range(tb):
            _row_copy(z_hbm, tok_next_ref[0, r], xbuf.at[1 - slot], r, sem.at[1 - slot]).start()
        gate = jnp.dot(x, wg_ref[...], preferred_element_type=F32)
        up = jnp.dot(x, wu_ref[...], preferred_element_type=F32)
        hidden = (gate * jax.nn.sigmoid(gate) * up).astype(BF16)
        y_ref[...] = jnp.dot(hidden, wd_ref[...], preferred_element_type=F32)

    @pl.when(i == nvalid - 1)
    def _():
        _wait_rows(z_hbm, xbuf.at[1 - slot], sem.at[1 - slot], tb)

    @pl.when(i >= nvalid)
    def _():
        y_ref[...] = jnp.zeros_like(y_ref)


def _expert_blocks(blk_e, nvalid, tok, z, wg, wu, wd, tb):
    nblk = tok.shape[0]
    d = z.shape[1]
    ff = wg.shape[2]
    grid_spec = pltpu.PrefetchScalarGridSpec(
        num_scalar_prefetch=2,
        grid=(nblk,),
        in_specs=[
            pl.BlockSpec((None, 1, tb), lambda i, be, nv: (0, 0, 0), memory_space=pltpu.SMEM),
            pl.BlockSpec((None, 1, tb), lambda i, be, nv: (jnp.clip(i + 1, 0, jnp.maximum(nv[0] - 1, 0)), 0, 0),
                         memory_space=pltpu.SMEM),
            pl.BlockSpec(memory_space=pl.ANY),
            pl.BlockSpec((None, d, ff), lambda i, be, nv: (be[i], 0, 0)),
            pl.BlockSpec((None, d, ff), lambda i, be, nv: (be[i], 0, 0)),
            pl.BlockSpec((None, ff, d), lambda i, be, nv: (be[i], 0, 0)),
        ],
        out_specs=pl.BlockSpec((tb, d), lambda i, be, nv: (i, 0)),
        scratch_shapes=[pltpu.VMEM((2, tb, d), F32), pltpu.SemaphoreType.DMA((2,))],
    )
    return pl.pallas_call(
        functools.partial(_expert_kernel, tb=tb),
        grid_spec=grid_spec,
        out_shape=jax.ShapeDtypeStruct((nblk * tb, d), F32),
        compiler_params=_params("arbitrary"),
        name="moe_experts",
    )(blk_e, nvalid, tok, tok, z, wg, wu, wd)


def _combine_kernel(dest0_ref, dest_next_ref, y_hbm, h_ref, info_ref, g_ref, o_ref, ybuf, sem, *, tm, final_norm):
    i = pl.program_id(0)
    slot = i % 2

    def issue(idx_ref, sl):
        for k in range(2):
            _issue_rows(idx_ref, k * tm, y_hbm, ybuf.at[sl, k], sem.at[sl], tm)

    @pl.when(i == 0)
    def _():
        issue(dest0_ref, 0)

    @pl.when(i + 1 < pl.num_programs(0))
    def _():
        issue(dest_next_ref, 1 - slot)

    for k in range(2):
        _wait_rows(y_hbm, ybuf.at[slot, k], sem.at[slot], tm)
    info = info_ref[...]
    moe = ybuf[slot, 0] * info[:, 2:3] + ybuf[slot, 1] * info[:, 3:4]
    hx = h_ref[...] + moe
    if final_norm:
        hx = hx * lax.rsqrt(jnp.mean(hx * hx, axis=-1, keepdims=True) + EPS) * g_ref[...]
    o_ref[...] = hx


def _combine(dest, y, h, info, g_final, tm, final_norm):
    n, d = h.shape
    n_tiles = n // tm
    return pl.pallas_call(
        functools.partial(_combine_kernel, tm=tm, final_norm=final_norm),
        grid=(n_tiles,),
        in_specs=[
            pl.BlockSpec((None, 1, 2 * tm), lambda i: (0, 0, 0), memory_space=pltpu.SMEM),
            pl.BlockSpec((None, 1, 2 * tm), lambda i: (jnp.minimum(i + 1, n_tiles - 1), 0, 0),
                         memory_space=pltpu.SMEM),
            pl.BlockSpec(memory_space=pl.ANY),
            pl.BlockSpec((tm, d), lambda i: (i, 0)),
            pl.BlockSpec((tm, LANE), lambda i: (i, 0)),
            pl.BlockSpec((1, d), lambda i: (0, 0)),
        ],
        out_specs=pl.BlockSpec((tm, d), lambda i: (i, 0)),
        out_shape=jax.ShapeDtypeStruct((n, d), F32),
        scratch_shapes=[pltpu.VMEM((2, 2, tm, d), F32), pltpu.SemaphoreType.DMA((2,))],
        compiler_params=_params("arbitrary"),
        name="moe_combine_final_norm",
    )(dest, dest, y, h, info, g_final.reshape(1, d).astype(F32))


def _dispatch_plan(eid, rank, counts, n, tb):
    n_slots = 2 * n
    nblk = n_slots // tb + N_EXPERTS
    slot_e = eid.reshape(-1)
    rank = rank.reshape(-1)
    blocks_per_e = (counts + tb - 1) // tb
    blk_end = jnp.cumsum(blocks_per_e)
    blk_start = blk_end - blocks_per_e
    dest = blk_start[slot_e] * tb + rank
    slot_tok = jnp.arange(n_slots, dtype=jnp.int32) // 2
    tok = jnp.zeros((nblk * tb,), jnp.int32).at[dest].set(slot_tok)
    blk_ids = jnp.arange(nblk, dtype=jnp.int32)
    blk_e = jnp.minimum(jnp.sum(blk_end[None, :] <= blk_ids[:, None], axis=1), N_EXPERTS - 1).astype(jnp.int32)
    nvalid = blk_end[-1:].astype(jnp.int32)
    return blk_e, nvalid, tok.reshape(nblk, 1, tb), dest.reshape(n, 2)


def _rope_tables(positions):
    half = MLA_ROPE // 2
    inv_freq = ROPE_THETA ** (-jnp.arange(half, dtype=F32) / half)
    ang = positions.astype(F32).reshape(-1, 1) * inv_freq[None, :]
    cos, sin = jnp.cos(ang), jnp.sin(ang)
    zero = jnp.zeros_like(cos)
    cos_t = jnp.concatenate([cos, cos, zero, zero], axis=1)
    sin_lo = jnp.concatenate([-sin, zero, zero, zero], axis=1)
    sin_hi = jnp.concatenate([zero, sin, zero, zero], axis=1)
    return cos_t, sin_lo, sin_hi


def kernel(x, mem, positions, g_mix, w_in, b_forget, g_q_latent, g_kv_latent, w_q_up, w_k_up, w_v_up, w_out, g_cross, g_mem, w_cross_q, w_cross_kv, w_cross_out, g_moe, w_group_router, b_group_router, w_expert_router, b_expert_router, w_exp_gate, w_exp_up, w_exp_down, g_final):
    b, s, d = x.shape
    n = b * s
    n_mem = mem.shape[1]
    depth = w_in.shape[0]
    fw = FOX_HEADS * LANE
    h = x.reshape(n, d)
    rope_tabs = _rope_tables(positions)

    for l in range(depth):
        wi = w_in[l]
        fox_scale = (LANE ** -0.5) * LOG2E
        o_f = 3 * fw
        o_cq = o_f + FOX_HEADS
        o_kr = o_cq + 2 * MLA_RANK
        o_g = o_kr + MLA_ROPE
        w_big = jnp.concatenate([wi[:, :fw] * fox_scale, wi[:, fw:2 * fw], wi[:, o_g:]], axis=1).astype(BF16)
        w_fox_v_t = wi[:, 2 * fw:3 * fw].T.astype(BF16)
        w_small = jnp.concatenate(
            [wi[:, o_cq:o_kr], wi[:, o_kr:o_g], wi[:, o_f:o_cq],
             jnp.zeros((d, LANE - MLA_ROPE - FOX_HEADS), F32)], axis=1).astype(BF16)
        mla_scale = ((MLA_NOPE + MLA_ROPE) ** -0.5) * LOG2E
        wq = (w_q_up[l] * mla_scale).reshape(MLA_RANK, MLA_HEADS, MLA_NOPE + MLA_ROPE)
        wq = jnp.pad(wq, ((0, 0), (0, 0), (0, 2 * LANE - MLA_NOPE - MLA_ROPE))).reshape(MLA_RANK, -1).astype(BF16)
        wk = w_k_up[l].astype(BF16)
        w_mla_v_t = w_v_up[l].T.astype(BF16)

        big = _norm_matmul(h, 0, d, g_mix[l], w_big, blocked=True, out_dtype=BF16, tm=PROJ_TM, tn=PROJ_TN,
                           name="in_proj")
        small = _norm_matmul(h, 0, d, g_mix[l], w_small, blocked=False, out_dtype=F32, tm=PROJ_TM, tn=384,
                             name="in_proj_small")
        nh = FOX_HEADS
        fox_v_t = _norm_matmul_t(h, 0, d, g_mix[l], w_fox_v_t, tm=PROJ_TM, tr=PROJ_TN, name="fox_v_t")
        bias = _forget_bias(small, 2 * MLA_RANK // LANE, MLA_ROPE, b_forget[l], b, s, 512)
        o_a = _causal_attention(big, lambda hd: hd, None, None, big, lambda hd: nh + hd, bias,
                                fox_v_t, nh, b, s, ATTN_TILE, 1, "fox_attention")
        qm = _norm_matmul(small, 0, MLA_RANK, g_q_latent[l], wq, blocked=True, out_dtype=BF16, tm=PROJ_TM,
                          tn=PROJ_TN, rope_tabs=rope_tabs, name="mla_q_up")
        k_nope = _norm_matmul(small, 1, MLA_RANK, g_kv_latent[l], wk, blocked=True, out_dtype=BF16, tm=PROJ_TM,
                              tn=PROJ_TN, name="mla_k_up")
        mla_v_t = _norm_matmul_t(small, 1, MLA_RANK, g_kv_latent[l], w_mla_v_t, tm=PROJ_TM, tr=PROJ_TN,
                                 name="mla_v_t")
        k_rope = _krope(small, 2 * MLA_RANK // LANE, rope_tabs, 512)
        o_b = _causal_attention(qm, lambda hd: 2 * hd, qm, lambda hd: 2 * hd + 1, k_nope, lambda hd: hd,
                                k_rope, mla_v_t, MLA_HEADS, b, s, ATTN_TILE, CHUNK, "mla_attention")
        h = _proj_residual(o_a, w_out[l].astype(BF16), h, tm=MERGE_TM, tn=d, merge=(o_b, big, 2, 3),
                           name="out_proj")
        cross_scale = ((d // MEM_HEADS) ** -0.5) * LOG2E
        q_c = _norm_matmul(h, 0, d, g_cross[l], (w_cross_q[l] * cross_scale).astype(BF16), blocked=True,
                           out_dtype=BF16, tm=PROJ_TM, tn=d, name="cross_q")
        mkv = _norm_matmul(mem.reshape(b * n_mem, d), 0, d, g_mem[l], w_cross_kv[l].astype(BF16), blocked=True,
                           out_dtype=BF16, tm=PROJ_TM, tn=PROJ_TN, name="cross_mem_kv")
        o_c = _cross_attention(q_c, mkv, b, s, n_mem, 512)
        h = _proj_residual(o_c, w_cross_out[l].astype(BF16), h, tm=MERGE_TM, tn=d, name="cross_out")
        w_r = jnp.concatenate([w_group_router[l], w_expert_router[l],
                               jnp.zeros((d, LANE - N_GROUPS - N_EXPERTS), F32)], axis=1)
        b_r = jnp.concatenate([b_group_router[l], b_expert_router[l],
                               jnp.zeros((LANE - N_GROUPS - N_EXPERTS,), F32)]).reshape(1, LANE)
        z, info, counts = _router(h, g_moe[l], w_r, b_r, 256)
        tb = 256
        eid = info[:, 0:2].astype(jnp.int32)
        rank = info[:, 4:6].astype(jnp.int32)
        counts = counts[0, N_GROUPS:N_GROUPS + N_EXPERTS].astype(jnp.int32)
        blk_e, nvalid, tok, dest = _dispatch_plan(eid, rank, counts, n, tb)
        y = _expert_blocks(blk_e, nvalid, tok, z, w_exp_gate[l].astype(BF16), w_exp_up[l].astype(BF16),
                           w_exp_down[l].astype(BF16), tb)
        tm_c = min(256, n)
        dest_blk = dest.reshape(n // tm_c, tm_c, 2).transpose(0, 2, 1).reshape(n // tm_c, 1, 2 * tm_c)
        h = _combine(dest_blk, y, h, info, g_final, tm_c, final_norm=(l + 1 == depth))
    return h.reshape(b, s, d)
```

```python
import functools
import math

import jax
import jax.numpy as jnp
from jax import lax
from jax.experimental import pallas as pl
from jax.experimental.pallas import tpu as pltpu

F32 = jnp.float32
BF16 = jnp.bfloat16

LANE = 128
BF16_SUBLANES = 16
EPS = 1e-6
CHUNK = 64
FOX_HEADS = 16
MLA_HEADS = 16
MLA_NOPE = 128
MLA_ROPE = 64
MLA_RANK = 512
ROPE_THETA = 10000.0
MEM_HEADS = 4
N_GROUPS = 8
EXPERTS_PER_GROUP = 8
N_EXPERTS = N_GROUPS * EXPERTS_PER_GROUP
LOG2E = math.log2(math.e)
VMEM_LIMIT = 56 * 1024 * 1024
NEG_INF = float("-inf")
ATTN_TILE = 1024
ATTN_HEAD_GROUP = 2
PROJ_TM = 1024
PROJ_TN = 1024
MERGE_TM = 512
GATHER_UNROLL = 8


def _params(*sem):
    return pltpu.CompilerParams(dimension_semantics=sem, vmem_limit_bytes=VMEM_LIMIT)


def _resident(whole):
    return pl.Buffered(1) if whole else None


def _rope_rotate(x, cos, sin_lo, sin_hi):
    return x * cos + pltpu.roll(x, 96, 1) * sin_lo + pltpu.roll(x, 32, 1) * sin_hi


def _norm_matmul_kernel(*refs, rope, blocked, nblk):
    if rope:
        x_ref, g_ref, w_ref, cos_ref, slo_ref, shi_ref, o_ref, xn_ref = refs
    else:
        x_ref, g_ref, w_ref, o_ref, xn_ref = refs

    @pl.when(pl.program_id(1) == 0)
    def _():
        x = x_ref[...].astype(F32)
        y = x * lax.rsqrt(jnp.mean(x * x, axis=-1, keepdims=True) + EPS)
        xn_ref[...] = (y * g_ref[...]).astype(BF16)

    acc = jnp.dot(xn_ref[...], w_ref[...], preferred_element_type=F32)
    if blocked:
        for c in range(nblk):
            blk = acc[:, c * LANE:(c + 1) * LANE]
            if rope and c % 2 == 1:
                blk = _rope_rotate(blk, cos_ref[...], slo_ref[...], shi_ref[...])
            o_ref[c] = blk.astype(o_ref.dtype)
    else:
        o_ref[...] = acc.astype(o_ref.dtype)


def _norm_matmul(x, x_col, k, g, w, *, blocked, out_dtype, tm, tn, rope_tabs=None, name=None):
    m = x.shape[0]
    nc = w.shape[1]
    tm = min(tm, m)
    tn = min(tn, nc)
    assert m % tm == 0 and nc % tn == 0 and tn % LANE == 0
    nblk = tn // LANE
    in_specs = [
        pl.BlockSpec((tm, k), lambda i, j: (i, x_col)),
        pl.BlockSpec((1, k), lambda i, j: (0, 0)),
        pl.BlockSpec((k, tn), lambda i, j: (0, j), pipeline_mode=_resident(nc == tn)),
    ]
    args = [x, g.reshape(1, k).astype(F32), w]
    if rope_tabs is not None:
        in_specs += [pl.BlockSpec((tm, LANE), lambda i, j: (i, 0))] * 3
        args += list(rope_tabs)
    if blocked:
        out_shape = jax.ShapeDtypeStruct((nc // LANE, m, LANE), out_dtype)
        out_spec = pl.BlockSpec((nblk, tm, LANE), lambda i, j: (j, i, 0))
    else:
        out_shape = jax.ShapeDtypeStruct((m, nc), out_dtype)
        out_spec = pl.BlockSpec((tm, tn), lambda i, j: (i, j))
    return pl.pallas_call(
        functools.partial(_norm_matmul_kernel, rope=rope_tabs is not None, blocked=blocked, nblk=nblk),
        grid=(m // tm, nc // tn),
        in_specs=in_specs,
        out_specs=out_spec,
        out_shape=out_shape,
        scratch_shapes=[pltpu.VMEM((tm, k), BF16)],
        compiler_params=_params("parallel", "arbitrary"),
        name=name,
    )(*args)


def _krope_kernel(x_ref, cos_ref, slo_ref, shi_ref, o_ref):
    o_ref[...] = _rope_rotate(x_ref[...], cos_ref[...], slo_ref[...], shi_ref[...]).astype(o_ref.dtype)


def _krope(small, col_blk, rope_tabs, tm):
    m = small.shape[0]
    tm = min(tm, m)
    return pl.pallas_call(
        _krope_kernel,
        grid=(m // tm,),
        in_specs=[pl.BlockSpec((tm, LANE), lambda i: (i, col_blk))] + [pl.BlockSpec((tm, LANE), lambda i: (i, 0))] * 3,
        out_specs=pl.BlockSpec((tm, LANE), lambda i: (i, 0)),
        out_shape=jax.ShapeDtypeStruct((m, LANE), BF16),
        compiler_params=_params("parallel"),
        name="k_rope",
    )(small, *rope_tabs)


N_SPLIT = 3


def _forget_bias_kernel(f_ref, b_ref, o_ref, carry_ref, *, ts, lane0, heads):
    @pl.when(pl.program_id(1) == 0)
    def _():
        carry_ref[...] = jnp.zeros_like(carry_ref)

    x = f_ref[...] + b_ref[...]
    log_f = jnp.minimum(x, 0.0) - jnp.log1p(jnp.exp(-jnp.abs(x)))
    row = lax.broadcasted_iota(jnp.int32, (ts, ts), 0)
    col = lax.broadcasted_iota(jnp.int32, (ts, ts), 1)
    lower = (col <= row).astype(F32)
    c = jnp.dot(lower, log_f, preferred_element_type=F32, precision=lax.Precision.HIGHEST) + carry_ref[0:1, :]
    carry_ref[...] = jnp.broadcast_to(c[ts - 1:ts, :], carry_ref.shape)
    rest = -(c * LOG2E)
    pieces = []
    for _ in range(N_SPLIT):
        piece = rest.astype(BF16).astype(F32)
        pieces.append(piece)
        rest = rest - piece
    lane = lax.broadcasted_iota(jnp.int32, (ts, LANE), 1)
    out = jnp.zeros((ts, LANE), F32)
    for hd in range(heads):
        for pi, piece in enumerate(pieces):
            out = jnp.where(lane == N_SPLIT * hd + pi, piece[:, lane0 + hd:lane0 + hd + 1], out)
    o_ref[...] = out.astype(o_ref.dtype)


def _forget_bias(small, col_blk, lane0, b_forget, b, s, ts):
    heads = b_forget.shape[0]
    assert N_SPLIT * heads <= LANE
    ts = min(ts, s)
    per_b = s // ts
    b_row = jnp.zeros((1, LANE), F32).at[0, lane0:lane0 + heads].set(b_forget.astype(F32))
    return pl.pallas_call(
        functools.partial(_forget_bias_kernel, ts=ts, lane0=lane0, heads=heads),
        grid=(b, per_b),
        in_specs=[pl.BlockSpec((ts, LANE), lambda bi, j: (bi * per_b + j, col_blk)),
                  pl.BlockSpec((1, LANE), lambda bi, j: (0, 0))],
        out_specs=pl.BlockSpec((ts, LANE), lambda bi, j: (bi * per_b + j, 0)),
        out_shape=jax.ShapeDtypeStruct((b * s, LANE), BF16),
        scratch_shapes=[pltpu.VMEM((8, LANE), F32)],
        compiler_params=_params("parallel", "arbitrary"),
        name="forget_bias",
    )(small, b_row)


def _norm_matmul_t_kernel(x_ref, g_ref, w_ref, o_ref, xn_ref):
    @pl.when(pl.program_id(1) == 0)
    def _():
        x = x_ref[...].astype(F32)
        y = x * lax.rsqrt(jnp.mean(x * x, axis=-1, keepdims=True) + EPS)
        xn_ref[...] = (y * g_ref[...]).astype(BF16)

    o_ref[...] = lax.dot_general(w_ref[...], xn_ref[...], (((1,), (1,)), ((), ())),
                                 preferred_element_type=F32).astype(o_ref.dtype)


def _norm_matmul_t(x, x_col, k, g, w_t, *, tm, tr, name=None):
    m = x.shape[0]
    r = w_t.shape[0]
    tm = min(tm, m)
    tr = min(tr, r)
    return pl.pallas_call(
        _norm_matmul_t_kernel,
        grid=(m // tm, r // tr),
        in_specs=[pl.BlockSpec((tm, k), lambda i, j: (i, x_col)), pl.BlockSpec((1, k), lambda i, j: (0, 0)),
                  pl.BlockSpec((tr, k), lambda i, j: (j, 0))],
        out_specs=pl.BlockSpec((tr, tm), lambda i, j: (j, i)),
        out_shape=jax.ShapeDtypeStruct((r, m), BF16),
        scratch_shapes=[pltpu.VMEM((tm, k), BF16)],
        compiler_params=_params("parallel", "arbitrary"),
        name=name,
    )(x, g.reshape(1, k).astype(F32), w_t)


N_ATTN_SCRATCH = 7


def _attn_kernel(*refs, t, unit, q_ones, group, n_cast):
    n_in = 3 if q_ones else 4
    kb_ref = refs[group * n_in]
    cast_src = refs[group * n_in + 1:group * n_in + 1 + n_cast]
    n_inputs = group * n_in + 1 + n_cast
    o_ref = refs[n_inputs]
    cast_dst = refs[n_inputs + 1:n_inputs + 1 + n_cast]
    scratch = refs[n_inputs + 1 + n_cast:]
    i = pl.program_id(2)
    streams = []
    for g in range(group):
        head_refs = refs[g * n_in:(g + 1) * n_in]
        if q_ones:
            qa_ref, ka_ref, vt_ref = head_refs
            lane = lax.broadcasted_iota(jnp.int32, (t, LANE), 1)
            lo = N_SPLIT * (pl.program_id(1) * group + g)
            q_b = jnp.where((lane >= lo) & (lane < lo + N_SPLIT), 1.0, 0.0).astype(BF16)
        else:
            qa_ref, qb_ref, ka_ref, vt_ref = head_refs
            q_b = qb_ref[...]
        q = jnp.concatenate([qa_ref[...], q_b], axis=1)
        streams.append((q, ka_ref, vt_ref) + tuple(scratch[g * N_ATTN_SCRATCH:(g + 1) * N_ATTN_SCRATCH]))

    def stage(name, j=None):
        for st in streams:
            _attn_stream(st, kb_ref, j, t, unit, name)

    def body(tau, carry):
        stage("values", tau - 2)
        stage("softmax")
        stage("logits", tau)
        return carry

    stage("init")
    stage("logits", 0)
    for src_ref, dst_ref in zip(cast_src, cast_dst):
        dst_ref[...] = src_ref[...].astype(dst_ref.dtype)
    lax.fori_loop(1, i + 1, body, 0)
    stage("values", i - 1)
    stage("softmax_masked")
    stage("values", i)
    for g, st in enumerate(streams):
        o_ref[g] = jnp.transpose(st[5][...] / st[4][...]).astype(o_ref.dtype)


def _attn_stream(st, kb_ref, j, t, unit, stage):
    q, ka_ref, vt_ref, m_scr, l_scr, acc_scr, s_ref, p_ref, a_ref, bmax_ref = st
    if stage == "init":
        m_scr[...] = jnp.full(m_scr.shape, NEG_INF, F32)
        l_scr[...] = jnp.zeros(l_scr.shape, F32)
        acc_scr[...] = jnp.zeros(acc_scr.shape, F32)
        p_ref[...] = jnp.zeros(p_ref.shape, BF16)
        a_ref[...] = jnp.ones(a_ref.shape, F32)
    elif stage == "logits":
        off = pl.multiple_of(j * t, t)
        k = jnp.concatenate([ka_ref[pl.ds(off, t), :], kb_ref[pl.ds(off, t), :]], axis=1)
        s = lax.dot_general(k, q, (((1,), (1,)), ((), ())), preferred_element_type=F32)
        s_ref[...] = s
        bmax_ref[...] = jnp.max(s, axis=0, keepdims=True)
    elif stage == "values":
        off = pl.multiple_of(jnp.maximum(j, 0) * t, t)
        pv = jnp.dot(vt_ref[:, pl.ds(off, t)], p_ref[...], preferred_element_type=F32)
        acc_scr[...] = a_ref[...] * acc_scr[...] + pv
    else:
        masked = stage == "softmax_masked"
        s = s_ref[...]
        if masked:
            key = lax.broadcasted_iota(jnp.int32, (t, t), 0)
            qry = lax.broadcasted_iota(jnp.int32, (t, t), 1)
            s = jnp.where((key // unit) <= (qry // unit), s, NEG_INF)
            block_max = jnp.max(s, axis=0, keepdims=True)
        else:
            block_max = bmax_ref[...]
        m_prev = m_scr[...]
        m_new = jnp.maximum(m_prev, block_max)
        alpha = jnp.exp2(m_prev - m_new)
        p = jnp.exp2(s - m_new)
        l_scr[...] = alpha * l_scr[...] + jnp.sum(p, axis=0, keepdims=True)
        m_scr[...] = m_new
        p_ref[...] = p.astype(BF16)
        a_ref[...] = alpha


def _attn_steps(heads, b, s, t):
    return b * (heads // ATTN_HEAD_GROUP) * (s // min(t, s))


def _cast_chunks(steps, n_slices, row_counts):
    chunks, rem = divmod(steps, n_slices)
    ok = rem == 0 and chunks > 0 and all(r % (chunks * BF16_SUBLANES) == 0 for r in row_counts)
    return chunks if ok else None


def _causal_attention(q_a, qa_blk, q_b, qb_blk, k_a, ka_blk, k_b, v_t, heads, b, s, t, unit, name, side_cast=None):
    t = min(t, s)
    group = ATTN_HEAD_GROUP
    assert t % unit == 0 and heads % group == 0
    nq = s // t
    once = pl.Buffered(1)
    in_specs, args = [], []
    for g in range(group):
        def q_spec(blk, g=g):
            return pl.BlockSpec((None, t, LANE), lambda bi, hi, i: (blk(hi * group + g), bi * nq + i, 0))

        in_specs.append(q_spec(qa_blk))
        args.append(q_a)
        if q_b is not None:
            in_specs.append(q_spec(qb_blk))
            args.append(q_b)
        in_specs.append(pl.BlockSpec((None, s, LANE), lambda bi, hi, i, g=g: (ka_blk(hi * group + g), bi, 0),
                                     pipeline_mode=once))
        in_specs.append(pl.BlockSpec((LANE, s), lambda bi, hi, i, g=g: (hi * group + g, bi), pipeline_mode=once))
        args += [k_a, v_t]
    in_specs.append(pl.BlockSpec((s, LANE), lambda bi, hi, i: (bi, 0), pipeline_mode=once))
    args.append(k_b)
    stat = pltpu.VMEM((1, t), F32)
    per_head_scratch = [stat, stat, pltpu.VMEM((LANE, t), F32), pltpu.VMEM((t, t), F32),
                        pltpu.VMEM((t, t), BF16), stat, stat]
    assert len(per_head_scratch) == N_ATTN_SCRATCH
    out_specs = [pl.BlockSpec((group, t, LANE), lambda bi, hi, i: (hi, bi * nq + i, 0))]
    out_shape = [jax.ShapeDtypeStruct((heads, b * s, LANE), BF16)]
    n_cast = 0
    if side_cast is not None:
        layer, chunks, items = side_cast
        n_cast = len(items)

        def step_of(bi, hi, i):
            return (bi * (heads // group) + hi) * nq + i

        for src, (rows, cols), (row_half, col_half) in items:
            rb = rows // chunks
            in_specs.append(pl.BlockSpec(
                (None, None, rb, cols),
                lambda bi, hi, i, rh=row_half, ch=col_half: (layer, step_of(bi, hi, i) // chunks,
                                                             rh * chunks + step_of(bi, hi, i) % chunks, ch)))
            args.append(src)
            out_specs.append(pl.BlockSpec(
                (None, rb, cols), lambda bi, hi, i: (step_of(bi, hi, i) // chunks, step_of(bi, hi, i) % chunks, 0)))
            out_shape.append(jax.ShapeDtypeStruct((src.shape[1], rows, cols), BF16))
    return pl.pallas_call(
        functools.partial(_attn_kernel, t=t, unit=unit, q_ones=q_b is None, group=group, n_cast=n_cast),
        grid=(b, heads // group, nq),
        in_specs=in_specs,
        out_specs=out_specs,
        out_shape=out_shape,
        scratch_shapes=per_head_scratch * group,
        compiler_params=_params("parallel", "parallel", "arbitrary"),
        name=name,
    )(*args)


def _proj_residual_kernel(*refs, merge, nb):
    if merge:
        oa_ref, ob_ref, ga_ref, gb_ref, w_ref, res_ref, o_ref, a_scr = refs
    else:
        oa_ref, w_ref, res_ref, o_ref, a_scr = refs

    @pl.when(pl.program_id(1) == 0)
    def _():
        for c in range(nb):
            if merge:
                a = (jax.nn.sigmoid(ga_ref[c].astype(F32)) * oa_ref[c].astype(F32)
                     + jax.nn.sigmoid(gb_ref[c].astype(F32)) * ob_ref[c].astype(F32))
            else:
                a = oa_ref[c]
            a_scr[:, c * LANE:(c + 1) * LANE] = a.astype(BF16)

    o_ref[...] = res_ref[...] + jnp.dot(a_scr[...], w_ref[...], preferred_element_type=F32)


def _proj_residual(o_a, w, res, *, tm, tn, merge=None, name=None):
    nb, m, _ = o_a.shape
    k = nb * LANE
    nc = w.shape[1]
    tm = min(tm, m)
    tn = min(tn, nc)
    head_spec = pl.BlockSpec((nb, tm, LANE), lambda i, j: (0, i, 0))
    in_specs = [head_spec]
    args = [o_a]
    if merge is not None:
        o_b, gates, ga_grp, gb_grp = merge
        in_specs += [head_spec,
                     pl.BlockSpec((nb, tm, LANE), lambda i, j: (ga_grp, i, 0)),
                     pl.BlockSpec((nb, tm, LANE), lambda i, j: (gb_grp, i, 0))]
        args += [o_b, gates, gates]
    in_specs += [pl.BlockSpec((k, tn), lambda i, j: (0, j), pipeline_mode=_resident(nc == tn)),
                 pl.BlockSpec((tm, tn), lambda i, j: (i, j))]
    args += [w, res]
    return pl.pallas_call(
        functools.partial(_proj_residual_kernel, merge=merge is not None, nb=nb),
        grid=(m // tm, nc // tn),
        in_specs=in_specs,
        out_specs=pl.BlockSpec((tm, tn), lambda i, j: (i, j)),
        out_shape=jax.ShapeDtypeStruct((m, nc), F32),
        scratch_shapes=[pltpu.VMEM((tm, k), BF16)],
        compiler_params=_params("parallel", "arbitrary"),
        name=name,
    )(*args)


def _cross_kernel(q_ref, mk_ref, mv_ref, o_ref, *, bph):
    for hh in range(MEM_HEADS):
        blocks = range(hh * bph, (hh + 1) * bph)
        q = jnp.concatenate([q_ref[c] for c in blocks], axis=1)
        k = jnp.concatenate([mk_ref[c] for c in blocks], axis=1)
        v = jnp.concatenate([mv_ref[c] for c in blocks], axis=1)
        s = lax.dot_general(q, k, (((1,), (1,)), ((), ())), preferred_element_type=F32)
        p = jnp.exp2(s - jnp.max(s, axis=1, keepdims=True))
        p = p / jnp.sum(p, axis=1, keepdims=True)
        o = jnp.dot(p.astype(BF16), v, preferred_element_type=F32)
        for ci, c in enumerate(blocks):
            o_ref[c] = o[:, ci * LANE:(ci + 1) * LANE].astype(o_ref.dtype)


def _cross_attention(q, mkv, b, s, n_mem, tm):
    nb, n, _ = q.shape
    tm = min(tm, s)
    per_b = s // tm
    return pl.pallas_call(
        functools.partial(_cross_kernel, bph=nb // MEM_HEADS),
        grid=(n // tm,),
        in_specs=[
            pl.BlockSpec((nb, tm, LANE), lambda i: (0, i, 0)),
            pl.BlockSpec((nb, n_mem, LANE), lambda i: (0, i // per_b, 0)),
            pl.BlockSpec((nb, n_mem, LANE), lambda i: (1, i // per_b, 0)),
        ],
        out_specs=pl.BlockSpec((nb, tm, LANE), lambda i: (0, i, 0)),
        out_shape=jax.ShapeDtypeStruct((nb, n, LANE), BF16),
        compiler_params=_params("parallel"),
        name="cross_attention",
    )(q, mkv, mkv)


def _router_kernel(h_ref, g_ref, w_ref, b_ref, z_ref, info_ref, count_ref):
    x = h_ref[...]
    z = x * lax.rsqrt(jnp.mean(x * x, axis=-1, keepdims=True) + EPS) * g_ref[...]
    z_ref[...] = z
    logits = jnp.dot(z, w_ref[...], preferred_element_type=F32, precision=lax.Precision.HIGHEST) + b_ref[...]
    lane = lax.broadcasted_iota(jnp.int32, logits.shape, 1).astype(F32)
    big = float(LANE)

    def first_where(mask):
        return jnp.min(jnp.where(mask, lane, big), axis=1, keepdims=True)

    gmask = lane < N_GROUPS
    gl = jnp.where(gmask, logits, NEG_INF)
    gmax = jnp.max(gl, axis=1, keepdims=True)
    gsum = jnp.sum(jnp.exp(gl - gmax), axis=1, keepdims=True)
    g_idx = first_where(gl == gmax)
    g_w = 1.0 / gsum
    lo = N_GROUPS + g_idx * EXPERTS_PER_GROUP
    emask = (lane >= lo) & (lane < lo + EXPERTS_PER_GROUP)
    el = jnp.where(emask, logits, NEG_INF)
    emax = jnp.max(el, axis=1, keepdims=True)
    eexp = jnp.exp(el - emax)
    prob = jnp.where(emask, eexp / jnp.sum(eexp, axis=1, keepdims=True), -1.0)
    p1 = jnp.max(prob, axis=1, keepdims=True)
    i1 = first_where(prob == p1)
    prob2 = jnp.where(lane == i1, -1.0, prob)
    p2 = jnp.max(prob2, axis=1, keepdims=True)
    i2 = first_where(prob2 == p2)
    denom = p1 + p2
    w1 = g_w * p1 / denom
    w2 = g_w * p2 / denom
    @pl.when(pl.program_id(0) == 0)
    def _():
        count_ref[...] = jnp.zeros_like(count_ref)

    tm = logits.shape[0]
    hot1 = jnp.where(lane == i1, 1.0, 0.0)
    hot2 = jnp.where(lane == i2, 1.0, 0.0)
    hot = hot1 + hot2
    row = lax.broadcasted_iota(jnp.int32, (tm, tm), 0)
    col = lax.broadcasted_iota(jnp.int32, (tm, tm), 1)
    before = jnp.where(col < row, 1.0, 0.0).astype(BF16)
    prior = jnp.dot(before, hot.astype(BF16), preferred_element_type=F32) + count_ref[0:1, :]
    r1 = jnp.sum(prior * hot1, axis=1, keepdims=True)
    r2 = jnp.sum(prior * hot2, axis=1, keepdims=True)
    count_ref[...] = count_ref[...] + jnp.sum(hot, axis=0, keepdims=True)
    info = jnp.zeros_like(logits)
    for k, val in enumerate((i1 - N_GROUPS, i2 - N_GROUPS, w1, w2, r1, r2)):
        info = jnp.where(lane == k, val, info)
    info_ref[...] = info


def _router(h, g, w_r, b_r, tm):
    n, d = h.shape
    tm = min(tm, n)
    return pl.pallas_call(
        _router_kernel,
        grid=(n // tm,),
        in_specs=[pl.BlockSpec((tm, d), lambda i: (i, 0)), pl.BlockSpec((1, d), lambda i: (0, 0)),
                  pl.BlockSpec((d, LANE), lambda i: (0, 0)), pl.BlockSpec((1, LANE), lambda i: (0, 0))],
        out_specs=[pl.BlockSpec((tm, d), lambda i: (i, 0)), pl.BlockSpec((tm, LANE), lambda i: (i, 0)),
                   pl.BlockSpec((8, LANE), lambda i: (0, 0))],
        out_shape=[jax.ShapeDtypeStruct((n, d), F32), jax.ShapeDtypeStruct((n, LANE), F32),
                   jax.ShapeDtypeStruct((8, LANE), F32)],
        compiler_params=_params("arbitrary"),
        name="moe_router",
    )(h, g.reshape(1, d).astype(F32), w_r, b_r)


def _row_copy(src_hbm, row, dst, dst_row, sem):
    return pltpu.make_async_copy(src_hbm.at[pl.ds(row, 1)], dst.at[pl.ds(dst_row, 1)], sem)


def _issue_rows(idx_ref, idx_off, src_hbm, dst, sem, n_rows):
    def issue(r, carry):
        _row_copy(src_hbm, idx_ref[0, idx_off + r], dst, r, sem).start()
        return carry

    lax.fori_loop(0, n_rows, issue, 0, unroll=GATHER_UNROLL)


def _wait_rows(src_hbm, dst, sem, n_rows):
    def wait(r, carry):
        _row_copy(src_hbm, 0, dst, r, sem).wait()
        return carry

    lax.fori_loop(0, n_rows, wait, 0, unroll=GATHER_UNROLL)


def _expert_kernel(blk_e_ref, nvalid_ref, tok0_ref, tok_next_ref, z_hbm, wg_lo, wu_lo, wd_lo, wg_hi, wu_hi, wd_hi,
                   y_ref, xbuf, sem, *, tb):
    i = pl.program_id(0)
    nvalid = nvalid_ref[0]
    slot = i % 2

    @pl.when((i == 0) & (nvalid > 0))
    def _():
        _issue_rows(tok0_ref, 0, z_hbm, xbuf.at[0], sem.at[0], tb)

    @pl.when(i + 1 < nvalid)
    def _():
        _issue_rows(tok_next_ref, 0, z_hbm, xbuf.at[1 - slot], sem.at[1 - slot], tb)

    @pl.when(i < nvalid)
    def _():
        _wait_rows(z_hbm, xbuf.at[slot], sem.at[slot], tb)
        x = xbuf[slot].astype(BF16)
        half = x.shape[1] // 2
        x_lo, x_hi = x[:, :half], x[:, half:]
        gate = (jnp.dot(x_lo, wg_lo[...], preferred_element_type=F32)
                + jnp.dot(x_hi, wg_hi[...], preferred_element_type=F32))
        up = (jnp.dot(x_lo, wu_lo[...], preferred_element_type=F32)
              + jnp.dot(x_hi, wu_hi[...], preferred_element_type=F32))
        hidden = (gate * jax.nn.sigmoid(gate) * up).astype(BF16)
        y_ref[:, :half] = jnp.dot(hidden, wd_lo[...], preferred_element_type=F32)
        y_ref[:, half:] = jnp.dot(hidden, wd_hi[...], preferred_element_type=F32)

    @pl.when(i >= nvalid)
    def _():
        y_ref[...] = jnp.zeros_like(y_ref)


def _expert_blocks(blk_e, nvalid, tok, z, w_lo, w_hi, tb):
    nblk = tok.shape[0]
    d = z.shape[1]
    ff = w_lo[0].shape[2]
    rows_half = pl.BlockSpec((None, d // 2, ff), lambda i, be, nv: (be[i], 0, 0))
    cols_half = pl.BlockSpec((None, ff, d // 2), lambda i, be, nv: (be[i], 0, 0))
    grid_spec = pltpu.PrefetchScalarGridSpec(
        num_scalar_prefetch=2,
        grid=(nblk,),
        in_specs=[
            pl.BlockSpec((None, 1, tb), lambda i, be, nv: (0, 0, 0), memory_space=pltpu.SMEM),
            pl.BlockSpec((None, 1, tb), lambda i, be, nv: (jnp.minimum(i + 1, nblk - 1), 0, 0),
                         memory_space=pltpu.SMEM),
            pl.BlockSpec(memory_space=pl.ANY),
            rows_half, rows_half, cols_half, rows_half, rows_half, cols_half,
        ],
        out_specs=pl.BlockSpec((tb, d), lambda i, be, nv: (i, 0)),
        scratch_shapes=[pltpu.VMEM((2, tb, d), F32), pltpu.SemaphoreType.DMA((2,))],
    )
    return pl.pallas_call(
        functools.partial(_expert_kernel, tb=tb),
        grid_spec=grid_spec,
        out_shape=jax.ShapeDtypeStruct((nblk * tb, d), F32),
        compiler_params=_params("arbitrary"),
        name="moe_experts",
    )(blk_e, nvalid, tok, tok, z, *w_lo, *w_hi)


def _combine_kernel(dest0_ref, dest_next_ref, y_hbm, h_ref, info_ref, g_ref, o_ref, ybuf, sem, *, tm, final_norm):
    i = pl.program_id(0)
    slot = i % 2

    def issue(idx_ref, sl):
        for k in range(2):
            _issue_rows(idx_ref, k * tm, y_hbm, ybuf.at[sl, k], sem.at[sl], tm)

    @pl.when(i == 0)
    def _():
        issue(dest0_ref, 0)

    @pl.when(i + 1 < pl.num_programs(0))
    def _():
        issue(dest_next_ref, 1 - slot)

    for k in range(2):
        _wait_rows(y_hbm, ybuf.at[slot, k], sem.at[slot], tm)
    info = info_ref[...]
    moe = ybuf[slot, 0] * info[:, 2:3] + ybuf[slot, 1] * info[:, 3:4]
    hx = h_ref[...] + moe
    if final_norm:
        hx = hx * lax.rsqrt(jnp.mean(hx * hx, axis=-1, keepdims=True) + EPS) * g_ref[...]
    o_ref[...] = hx


def _combine(dest, y, h, info, g_final, tm, final_norm):
    n, d = h.shape
    n_tiles = n // tm
    return pl.pallas_call(
        functools.partial(_combine_kernel, tm=tm, final_norm=final_norm),
        grid=(n_tiles,),
        in_specs=[
            pl.BlockSpec((None, 1, 2 * tm), lambda i: (0, 0, 0), memory_space=pltpu.SMEM),
            pl.BlockSpec((None, 1, 2 * tm), lambda i: (jnp.minimum(i + 1, n_tiles - 1), 0, 0),
                         memory_space=pltpu.SMEM),
            pl.BlockSpec(memory_space=pl.ANY),
            pl.BlockSpec((tm, d), lambda i: (i, 0)),
            pl.BlockSpec((tm, LANE), lambda i: (i, 0)),
            pl.BlockSpec((1, d), lambda i: (0, 0)),
        ],
        out_specs=pl.BlockSpec((tm, d), lambda i: (i, 0)),
        out_shape=jax.ShapeDtypeStruct((n, d), F32),
        scratch_shapes=[pltpu.VMEM((2, 2, tm, d), F32), pltpu.SemaphoreType.DMA((2,))],
        compiler_params=_params("arbitrary"),
        name="moe_combine_final_norm",
    )(dest, dest, y, h, info, g_final.reshape(1, d).astype(F32))


def _dispatch_plan(eid, rank, counts, n, tb):
    n_slots = 2 * n
    nblk = n_slots // tb + N_EXPERTS
    slot_e = eid.reshape(-1)
    rank = rank.reshape(-1)
    blocks_per_e = (counts + tb - 1) // tb
    blk_end = jnp.cumsum(blocks_per_e)
    blk_start = blk_end - blocks_per_e
    dest = blk_start[slot_e] * tb + rank
    slot_tok = jnp.arange(n_slots, dtype=jnp.int32) // 2
    tok = jnp.zeros((nblk * tb,), jnp.int32).at[dest].set(slot_tok)
    blk_ids = jnp.arange(nblk, dtype=jnp.int32)
    blk_e = jnp.minimum(jnp.sum(blk_end[None, :] <= blk_ids[:, None], axis=1), N_EXPERTS - 1).astype(jnp.int32)
    nvalid = blk_end[-1:].astype(jnp.int32)
    return blk_e, nvalid, tok.reshape(nblk, 1, tb), dest.reshape(n, 2)


def _rope_tables(positions):
    half = MLA_ROPE // 2
    inv_freq = ROPE_THETA ** (-jnp.arange(half, dtype=F32) / half)
    ang = positions.astype(F32).reshape(-1, 1) * inv_freq[None, :]
    cos, sin = jnp.cos(ang), jnp.sin(ang)
    zero = jnp.zeros_like(cos)
    cos_t = jnp.concatenate([cos, cos, zero, zero], axis=1)
    sin_lo = jnp.concatenate([-sin, zero, zero, zero], axis=1)
    sin_hi = jnp.concatenate([zero, sin, zero, zero], axis=1)
    return cos_t, sin_lo, sin_hi


def kernel(x, mem, positions, g_mix, w_in, b_forget, g_q_latent, g_kv_latent, w_q_up, w_k_up, w_v_up, w_out, g_cross, g_mem, w_cross_q, w_cross_kv, w_cross_out, g_moe, w_group_router, b_group_router, w_expert_router, b_expert_router, w_exp_gate, w_exp_up, w_exp_down, g_final):
    b, s, d = x.shape
    n = b * s
    n_mem = mem.shape[1]
    depth = w_in.shape[0]
    fw = FOX_HEADS * LANE
    h = x.reshape(n, d)
    rope_tabs = _rope_tables(positions)

    for l in range(depth):
        wi = w_in[l]
        fox_scale = (LANE ** -0.5) * LOG2E
        o_f = 3 * fw
        o_cq = o_f + FOX_HEADS
        o_kr = o_cq + 2 * MLA_RANK
        o_g = o_kr + MLA_ROPE
        w_big = jnp.concatenate([wi[:, :fw] * fox_scale, wi[:, fw:2 * fw], wi[:, o_g:]], axis=1).astype(BF16)
        w_fox_v_t = wi[:, 2 * fw:3 * fw].T.astype(BF16)
        w_small = jnp.concatenate(
            [wi[:, o_cq:o_kr], wi[:, o_kr:o_g], wi[:, o_f:o_cq],
             jnp.zeros((d, LANE - MLA_ROPE - FOX_HEADS), F32)], axis=1).astype(BF16)
        mla_scale = ((MLA_NOPE + MLA_ROPE) ** -0.5) * LOG2E
        wq = (w_q_up[l] * mla_scale).reshape(MLA_RANK, MLA_HEADS, MLA_NOPE + MLA_ROPE)
        wq = jnp.pad(wq, ((0, 0), (0, 0), (0, 2 * LANE - MLA_NOPE - MLA_ROPE))).reshape(MLA_RANK, -1).astype(BF16)
        wk = w_k_up[l].astype(BF16)
        w_mla_v_t = w_v_up[l].T.astype(BF16)

        big = _norm_matmul(h, 0, d, g_mix[l], w_big, blocked=True, out_dtype=BF16, tm=PROJ_TM, tn=PROJ_TN,
                           name="in_proj")
        small = _norm_matmul(h, 0, d, g_mix[l], w_small, blocked=False, out_dtype=F32, tm=PROJ_TM, tn=384,
                             name="in_proj_small")
        nh = FOX_HEADS
        fox_v_t = _norm_matmul_t(h, 0, d, g_mix[l], w_fox_v_t, tm=PROJ_TM, tr=PROJ_TN, name="fox_v_t")
        bias = _forget_bias(small, 2 * MLA_RANK // LANE, MLA_ROPE, b_forget[l], b, s, 512)
        ff = w_exp_gate.shape[3]
        steps = _attn_steps(nh, b, s, ATTN_TILE)
        chunks = _cast_chunks(steps, N_EXPERTS, (d // 2, ff))
        if steps != _attn_steps(MLA_HEADS, b, s, ATTN_TILE):
            chunks = None

        def cast_half(half):
            if chunks is None:
                return None
            return (l, chunks, [(w_exp_gate, (d // 2, ff), (half, 0)), (w_exp_up, (d // 2, ff), (half, 0)),
                                (w_exp_down, (ff, d // 2), (0, half))])

        o_a, *expert_lo = _causal_attention(big, lambda hd: hd, None, None, big, lambda hd: nh + hd, bias,
                                            fox_v_t, nh, b, s, ATTN_TILE, 1, "fox_attention", cast_half(0))
        qm = _norm_matmul(small, 0, MLA_RANK, g_q_latent[l], wq, blocked=True, out_dtype=BF16, tm=PROJ_TM,
                          tn=PROJ_TN, rope_tabs=rope_tabs, name="mla_q_up")
        k_nope = _norm_matmul(small, 1, MLA_RANK, g_kv_latent[l], wk, blocked=True, out_dtype=BF16, tm=PROJ_TM,
                              tn=PROJ_TN, name="mla_k_up")
        mla_v_t = _norm_matmul_t(small, 1, MLA_RANK, g_kv_latent[l], w_mla_v_t, tm=PROJ_TM, tr=PROJ_TN,
                                 name="mla_v_t")
        k_rope = _krope(small, 2 * MLA_RANK // LANE, rope_tabs, 512)
        o_b, *expert_hi = _causal_attention(qm, lambda hd: 2 * hd, qm, lambda hd: 2 * hd + 1, k_nope, lambda hd: hd,
                                            k_rope, mla_v_t, MLA_HEADS, b, s, ATTN_TILE, CHUNK, "mla_attention",
                                            cast_half(1))
        if chunks is None:
            hd2 = d // 2
            expert_lo = [w_exp_gate[l, :, :hd2].astype(BF16), w_exp_up[l, :, :hd2].astype(BF16),
                         w_exp_down[l, :, :, :hd2].astype(BF16)]
            expert_hi = [w_exp_gate[l, :, hd2:].astype(BF16), w_exp_up[l, :, hd2:].astype(BF16),
                         w_exp_down[l, :, :, hd2:].astype(BF16)]
        h = _proj_residual(o_a, w_out[l].astype(BF16), h, tm=MERGE_TM, tn=d, merge=(o_b, big, 2, 3),
                           name="out_proj")
        cross_scale = ((d // MEM_HEADS) ** -0.5) * LOG2E
        q_c = _norm_matmul(h, 0, d, g_cross[l], (w_cross_q[l] * cross_scale).astype(BF16), blocked=True,
                           out_dtype=BF16, tm=PROJ_TM, tn=d, name="cross_q")
        mkv = _norm_matmul(mem.reshape(b * n_mem, d), 0, d, g_mem[l], w_cross_kv[l].astype(BF16), blocked=True,
                           out_dtype=BF16, tm=PROJ_TM, tn=PROJ_TN, name="cross_mem_kv")
        o_c = _cross_attention(q_c, mkv, b, s, n_mem, 512)
        h = _proj_residual(o_c, w_cross_out[l].astype(BF16), h, tm=MERGE_TM, tn=d, name="cross_out")
        w_r = jnp.concatenate([w_group_router[l], w_expert_router[l],
                               jnp.zeros((d, LANE - N_GROUPS - N_EXPERTS), F32)], axis=1)
        b_r = jnp.concatenate([b_group_router[l], b_expert_router[l],
                               jnp.zeros((LANE - N_GROUPS - N_EXPERTS,), F32)]).reshape(1, LANE)
        z, info, counts = _router(h, g_moe[l], w_r, b_r, 256)
        tb = 256
        eid = info[:, 0:2].astype(jnp.int32)
        rank = info[:, 4:6].astype(jnp.int32)
        counts = counts[0, N_GROUPS:N_GROUPS + N_EXPERTS].astype(jnp.int32)
        blk_e, nvalid, tok, dest = _dispatch_plan(eid, rank, counts, n, tb)
        y = _expert_blocks(blk_e, nvalid, tok, z, expert_lo, expert_hi, tb)
        tm_c = min(256, n)
        dest_blk = dest.reshape(n // tm_c, tm_c, 2).transpose(0, 2, 1).reshape(n // tm_c, 1, 2 * tm_c)
        h = _combine(dest_blk, y, h, info, g_final, tm_c, final_norm=(l + 1 == depth))
    return h.reshape(b, s, d)
```

```python
import functools
import math

import jax
import jax.numpy as jnp
from jax import lax
from jax.experimental import pallas as pl
from jax.experimental.pallas import tpu as pltpu

F32 = jnp.float32
BF16 = jnp.bfloat16

LANE = 128
BF16_SUBLANES = 16
EPS = 1e-6
CHUNK = 64
FOX_HEADS = 16
MLA_HEADS = 16
MLA_NOPE = 128
MLA_ROPE = 64
MLA_RANK = 512
ROPE_THETA = 10000.0
MEM_HEADS = 4
N_GROUPS = 8
EXPERTS_PER_GROUP = 8
N_EXPERTS = N_GROUPS * EXPERTS_PER_GROUP
LOG2E = math.log2(math.e)
VMEM_LIMIT = 56 * 1024 * 1024
NEG_INF = float("-inf")
ATTN_TILE = 1024
ATTN_HEAD_GROUP = 2
PROJ_TM = 1024
PROJ_TN = 1024
MERGE_TM = 512
GATHER_UNROLL = 8


def _params(*sem):
    return pltpu.CompilerParams(dimension_semantics=sem, vmem_limit_bytes=VMEM_LIMIT)


def _resident(whole):
    return pl.Buffered(1) if whole else None


def _rope_rotate(x, cos, sin_lo, sin_hi):
    return x * cos + pltpu.roll(x, 96, 1) * sin_lo + pltpu.roll(x, 32, 1) * sin_hi


def _norm_matmul_kernel(*refs, rope, blocked, nblk, extra):
    x_ref, g_ref, w_ref = refs[:3]
    rest = list(refs[3:])
    if rope:
        cos_ref, slo_ref, shi_ref = rest[:3]
        rest = rest[3:]
    if extra:
        we_ref, o_ref, oe_ref, xn_ref = rest
    else:
        o_ref, xn_ref = rest

    @pl.when(pl.program_id(1) == 0)
    def _():
        x = x_ref[...].astype(F32)
        y = x * lax.rsqrt(jnp.mean(x * x, axis=-1, keepdims=True) + EPS)
        xn_ref[...] = (y * g_ref[...]).astype(BF16)
        if extra:
            oe_ref[...] = jnp.dot(xn_ref[...], we_ref[...], preferred_element_type=F32)

    acc = jnp.dot(xn_ref[...], w_ref[...], preferred_element_type=F32)
    if blocked:
        for c in range(nblk):
            blk = acc[:, c * LANE:(c + 1) * LANE]
            if rope and c % 2 == 1:
                blk = _rope_rotate(blk, cos_ref[...], slo_ref[...], shi_ref[...])
            o_ref[c] = blk.astype(o_ref.dtype)
    else:
        o_ref[...] = acc.astype(o_ref.dtype)


def _norm_matmul(x, x_col, k, g, w, *, blocked, out_dtype, tm, tn, rope_tabs=None, w_extra=None, name=None):
    m = x.shape[0]
    nc = w.shape[1]
    tm = min(tm, m)
    tn = min(tn, nc)
    assert m % tm == 0 and nc % tn == 0 and tn % LANE == 0
    nblk = tn // LANE
    in_specs = [
        pl.BlockSpec((tm, k), lambda i, j: (i, x_col)),
        pl.BlockSpec((1, k), lambda i, j: (0, 0)),
        pl.BlockSpec((k, tn), lambda i, j: (0, j), pipeline_mode=_resident(nc == tn)),
    ]
    args = [x, g.reshape(1, k).astype(F32), w]
    if rope_tabs is not None:
        in_specs += [pl.BlockSpec((tm, LANE), lambda i, j: (i, 0))] * 3
        args += list(rope_tabs)
    if blocked:
        out_shape = jax.ShapeDtypeStruct((nc // LANE, m, LANE), out_dtype)
        out_spec = pl.BlockSpec((nblk, tm, LANE), lambda i, j: (j, i, 0))
    else:
        out_shape = jax.ShapeDtypeStruct((m, nc), out_dtype)
        out_spec = pl.BlockSpec((tm, tn), lambda i, j: (i, j))
    if w_extra is not None:
        ne = w_extra.shape[1]
        in_specs.append(pl.BlockSpec((k, ne), lambda i, j: (0, 0), pipeline_mode=_resident(True)))
        args.append(w_extra)
        out_spec = [out_spec, pl.BlockSpec((tm, ne), lambda i, j: (i, 0))]
        out_shape = [out_shape, jax.ShapeDtypeStruct((m, ne), F32)]
    return pl.pallas_call(
        functools.partial(_norm_matmul_kernel, rope=rope_tabs is not None, blocked=blocked, nblk=nblk,
                          extra=w_extra is not None),
        grid=(m // tm, nc // tn),
        in_specs=in_specs,
        out_specs=out_spec,
        out_shape=out_shape,
        scratch_shapes=[pltpu.VMEM((tm, k), BF16)],
        compiler_params=_params("parallel", "arbitrary"),
        name=name,
    )(*args)


def _krope_kernel(x_ref, cos_ref, slo_ref, shi_ref, o_ref):
    o_ref[...] = _rope_rotate(x_ref[...], cos_ref[...], slo_ref[...], shi_ref[...]).astype(o_ref.dtype)


def _krope(small, col_blk, rope_tabs, tm):
    m = small.shape[0]
    tm = min(tm, m)
    return pl.pallas_call(
        _krope_kernel,
        grid=(m // tm,),
        in_specs=[pl.BlockSpec((tm, LANE), lambda i: (i, col_blk))] + [pl.BlockSpec((tm, LANE), lambda i: (i, 0))] * 3,
        out_specs=pl.BlockSpec((tm, LANE), lambda i: (i, 0)),
        out_shape=jax.ShapeDtypeStruct((m, LANE), BF16),
        compiler_params=_params("parallel"),
        name="k_rope",
    )(small, *rope_tabs)


N_SPLIT = 3


def _forget_bias_kernel(f_ref, b_ref, o_ref, carry_ref, *, ts, lane0, heads):
    @pl.when(pl.program_id(1) == 0)
    def _():
        carry_ref[...] = jnp.zeros_like(carry_ref)

    x = f_ref[...] + b_ref[...]
    log_f = jnp.minimum(x, 0.0) - jnp.log1p(jnp.exp(-jnp.abs(x)))
    row = lax.broadcasted_iota(jnp.int32, (ts, ts), 0)
    col = lax.broadcasted_iota(jnp.int32, (ts, ts), 1)
    lower = (col <= row).astype(F32)
    c = jnp.dot(lower, log_f, preferred_element_type=F32, precision=lax.Precision.HIGHEST) + carry_ref[0:1, :]
    carry_ref[...] = jnp.broadcast_to(c[ts - 1:ts, :], carry_ref.shape)
    rest = -(c * LOG2E)
    src = lax.broadcasted_iota(jnp.int32, (LANE, LANE), 0) - lane0
    dst = lax.broadcasted_iota(jnp.int32, (LANE, LANE), 1)
    out = jnp.zeros((ts, LANE), F32)
    for pi in range(N_SPLIT):
        piece = rest.astype(BF16)
        rest = rest - piece.astype(F32)
        place = jnp.where((src >= 0) & (src < heads) & (dst == N_SPLIT * src + pi), 1.0, 0.0).astype(BF16)
        out = out + jnp.dot(piece, place, preferred_element_type=F32)
    o_ref[...] = out.astype(o_ref.dtype)


def _forget_bias(small, col_blk, lane0, b_forget, b, s, ts):
    heads = b_forget.shape[0]
    assert N_SPLIT * heads <= LANE
    ts = min(ts, s)
    per_b = s // ts
    b_row = jnp.zeros((1, LANE), F32).at[0, lane0:lane0 + heads].set(b_forget.astype(F32))
    return pl.pallas_call(
        functools.partial(_forget_bias_kernel, ts=ts, lane0=lane0, heads=heads),
        grid=(b, per_b),
        in_specs=[pl.BlockSpec((ts, LANE), lambda bi, j: (bi * per_b + j, col_blk)),
                  pl.BlockSpec((1, LANE), lambda bi, j: (0, 0))],
        out_specs=pl.BlockSpec((ts, LANE), lambda bi, j: (bi * per_b + j, 0)),
        out_shape=jax.ShapeDtypeStruct((b * s, LANE), BF16),
        scratch_shapes=[pltpu.VMEM((8, LANE), F32)],
        compiler_params=_params("parallel", "arbitrary"),
        name="forget_bias",
    )(small, b_row)


def _norm_matmul_t_kernel(x_ref, g_ref, w_ref, o_ref, xn_ref):
    @pl.when(pl.program_id(1) == 0)
    def _():
        x = x_ref[...].astype(F32)
        y = x * lax.rsqrt(jnp.mean(x * x, axis=-1, keepdims=True) + EPS)
        xn_ref[...] = (y * g_ref[...]).astype(BF16)

    o_ref[...] = lax.dot_general(w_ref[...], xn_ref[...], (((1,), (1,)), ((), ())),
                                 preferred_element_type=F32).astype(o_ref.dtype)


def _norm_matmul_t(x, x_col, k, g, w_t, *, tm, tr, name=None):
    m = x.shape[0]
    r = w_t.shape[0]
    tm = min(tm, m)
    tr = min(tr, r)
    return pl.pallas_call(
        _norm_matmul_t_kernel,
        grid=(m // tm, r // tr),
        in_specs=[pl.BlockSpec((tm, k), lambda i, j: (i, x_col)), pl.BlockSpec((1, k), lambda i, j: (0, 0)),
                  pl.BlockSpec((tr, k), lambda i, j: (j, 0))],
        out_specs=pl.BlockSpec((tr, tm), lambda i, j: (j, i)),
        out_shape=jax.ShapeDtypeStruct((r, m), BF16),
        scratch_shapes=[pltpu.VMEM((tm, k), BF16)],
        compiler_params=_params("parallel", "arbitrary"),
        name=name,
    )(x, g.reshape(1, k).astype(F32), w_t)


N_ATTN_SCRATCH = 7


def _attn_kernel(*refs, t, unit, q_ones, group, n_cast):
    n_in = 3 if q_ones else 4
    kb_ref = refs[group * n_in]
    cast_src = refs[group * n_in + 1:group * n_in + 1 + n_cast]
    n_inputs = group * n_in + 1 + n_cast
    o_ref = refs[n_inputs]
    cast_dst = refs[n_inputs + 1:n_inputs + 1 + n_cast]
    scratch = refs[n_inputs + 1 + n_cast:]
    i = pl.program_id(2)
    streams = []
    for g in range(group):
        head_refs = refs[g * n_in:(g + 1) * n_in]
        if q_ones:
            qa_ref, ka_ref, vt_ref = head_refs
            lane = lax.broadcasted_iota(jnp.int32, (t, LANE), 1)
            lo = N_SPLIT * (pl.program_id(1) * group + g)
            q_b = jnp.where((lane >= lo) & (lane < lo + N_SPLIT), 1.0, 0.0).astype(BF16)
        else:
            qa_ref, qb_ref, ka_ref, vt_ref = head_refs
            q_b = qb_ref[...]
        q = jnp.concatenate([qa_ref[...], q_b], axis=1)
        streams.append((q, ka_ref, vt_ref) + tuple(scratch[g * N_ATTN_SCRATCH:(g + 1) * N_ATTN_SCRATCH]))

    def stage(name, j=None):
        for st in streams:
            _attn_stream(st, kb_ref, j, t, unit, name)

    def body(tau, carry):
        stage("values", tau - 2)
        stage("softmax")
        stage("logits", tau)
        return carry

    stage("init")
    stage("logits", 0)
    for src_ref, dst_ref in zip(cast_src, cast_dst):
        dst_ref[...] = src_ref[...].astype(dst_ref.dtype)
    lax.fori_loop(1, i + 1, body, 0)
    stage("values", i - 1)
    stage("softmax_masked")
    stage("values", i)
    for g, st in enumerate(streams):
        o_ref[g] = jnp.transpose(st[5][...] / st[4][...]).astype(o_ref.dtype)


def _attn_stream(st, kb_ref, j, t, unit, stage):
    q, ka_ref, vt_ref, m_scr, l_scr, acc_scr, s_ref, p_ref, a_ref, bmax_ref = st
    if stage == "init":
        m_scr[...] = jnp.full(m_scr.shape, NEG_INF, F32)
        l_scr[...] = jnp.zeros(l_scr.shape, F32)
        acc_scr[...] = jnp.zeros(acc_scr.shape, F32)
        p_ref[...] = jnp.zeros(p_ref.shape, BF16)
        a_ref[...] = jnp.ones(a_ref.shape, F32)
    elif stage == "logits":
        off = pl.multiple_of(j * t, t)
        k = jnp.concatenate([ka_ref[pl.ds(off, t), :], kb_ref[pl.ds(off, t), :]], axis=1)
        s = lax.dot_general(k, q, (((1,), (1,)), ((), ())), preferred_element_type=F32)
        s_ref[...] = s
        bmax_ref[...] = jnp.max(s, axis=0, keepdims=True)
    elif stage == "values":
        off = pl.multiple_of(jnp.maximum(j, 0) * t, t)
        pv = jnp.dot(vt_ref[:, pl.ds(off, t)], p_ref[...], preferred_element_type=F32)
        acc_scr[...] = a_ref[...] * acc_scr[...] + pv
    else:
        masked = stage == "softmax_masked"
        s = s_ref[...]
        if masked:
            key = lax.broadcasted_iota(jnp.int32, (t, t), 0)
            qry = lax.broadcasted_iota(jnp.int32, (t, t), 1)
            s = jnp.where((key // unit) <= (qry // unit), s, NEG_INF)
            block_max = jnp.max(s, axis=0, keepdims=True)
        else:
            block_max = bmax_ref[...]
        m_prev = m_scr[...]
        m_new = jnp.maximum(m_prev, block_max)
        alpha = jnp.exp2(m_prev - m_new)
        p = jnp.exp2(s - m_new)
        l_scr[...] = alpha * l_scr[...] + jnp.sum(p, axis=0, keepdims=True)
        m_scr[...] = m_new
        p_ref[...] = p.astype(BF16)
        a_ref[...] = alpha


def _attn_steps(heads, b, s, t):
    return b * (heads // ATTN_HEAD_GROUP) * (s // min(t, s))


def _cast_chunks(steps, n_slices, row_counts):
    chunks, rem = divmod(steps, n_slices)
    ok = rem == 0 and chunks > 0 and all(r % (chunks * BF16_SUBLANES) == 0 for r in row_counts)
    return chunks if ok else None


def _causal_attention(q_a, qa_blk, q_b, qb_blk, k_a, ka_blk, k_b, v_t, heads, b, s, t, unit, name, side_cast=None):
    t = min(t, s)
    group = ATTN_HEAD_GROUP
    assert t % unit == 0 and heads % group == 0
    nq = s // t
    once = pl.Buffered(1)
    in_specs, args = [], []
    for g in range(group):
        def q_spec(blk, g=g):
            return pl.BlockSpec((None, t, LANE), lambda bi, hi, i: (blk(hi * group + g), bi * nq + i, 0))

        in_specs.append(q_spec(qa_blk))
        args.append(q_a)
        if q_b is not None:
            in_specs.append(q_spec(qb_blk))
            args.append(q_b)
        in_specs.append(pl.BlockSpec((None, s, LANE), lambda bi, hi, i, g=g: (ka_blk(hi * group + g), bi, 0),
                                     pipeline_mode=once))
        in_specs.append(pl.BlockSpec((LANE, s), lambda bi, hi, i, g=g: (hi * group + g, bi), pipeline_mode=once))
        args += [k_a, v_t]
    in_specs.append(pl.BlockSpec((s, LANE), lambda bi, hi, i: (bi, 0), pipeline_mode=once))
    args.append(k_b)
    stat = pltpu.VMEM((1, t), F32)
    per_head_scratch = [stat, stat, pltpu.VMEM((LANE, t), F32), pltpu.VMEM((t, t), F32),
                        pltpu.VMEM((t, t), BF16), stat, stat]
    assert len(per_head_scratch) == N_ATTN_SCRATCH
    out_specs = [pl.BlockSpec((group, t, LANE), lambda bi, hi, i: (hi, bi * nq + i, 0))]
    out_shape = [jax.ShapeDtypeStruct((heads, b * s, LANE), BF16)]
    n_cast = 0
    if side_cast is not None:
        layer, chunks, items = side_cast
        n_cast = len(items)

        def step_of(bi, hi, i):
            return (bi * (heads // group) + hi) * nq + i

        for src, (rows, cols), (row_half, col_half) in items:
            rb = rows // chunks
            in_specs.append(pl.BlockSpec(
                (None, None, rb, cols),
                lambda bi, hi, i, rh=row_half, ch=col_half: (layer, step_of(bi, hi, i) // chunks,
                                                             rh * chunks + step_of(bi, hi, i) % chunks, ch)))
            args.append(src)
            out_specs.append(pl.BlockSpec(
                (None, rb, cols), lambda bi, hi, i: (step_of(bi, hi, i) // chunks, step_of(bi, hi, i) % chunks, 0)))
            out_shape.append(jax.ShapeDtypeStruct((src.shape[1], rows, cols), BF16))
    return pl.pallas_call(
        functools.partial(_attn_kernel, t=t, unit=unit, q_ones=q_b is None, group=group, n_cast=n_cast),
        grid=(b, heads // group, nq),
        in_specs=in_specs,
        out_specs=out_specs,
        out_shape=out_shape,
        scratch_shapes=per_head_scratch * group,
        compiler_params=_params("parallel", "parallel", "arbitrary"),
        name=name,
    )(*args)


def _proj_residual_kernel(*refs, merge, nb):
    if merge:
        oa_ref, ob_ref, ga_ref, gb_ref, w_ref, res_ref, o_ref, a_scr = refs
    else:
        oa_ref, w_ref, res_ref, o_ref, a_scr = refs

    @pl.when(pl.program_id(1) == 0)
    def _():
        for c in range(nb):
            if merge:
                a = (jax.nn.sigmoid(ga_ref[c].astype(F32)) * oa_ref[c].astype(F32)
                     + jax.nn.sigmoid(gb_ref[c].astype(F32)) * ob_ref[c].astype(F32))
            else:
                a = oa_ref[c]
            a_scr[:, c * LANE:(c + 1) * LANE] = a.astype(BF16)

    o_ref[...] = res_ref[...] + jnp.dot(a_scr[...], w_ref[...], preferred_element_type=F32)


def _proj_residual(o_a, w, res, *, tm, tn, merge=None, name=None):
    nb, m, _ = o_a.shape
    k = nb * LANE
    nc = w.shape[1]
    tm = min(tm, m)
    tn = min(tn, nc)
    head_spec = pl.BlockSpec((nb, tm, LANE), lambda i, j: (0, i, 0))
    in_specs = [head_spec]
    args = [o_a]
    if merge is not None:
        o_b, gates, ga_grp, gb_grp = merge
        in_specs += [head_spec,
                     pl.BlockSpec((nb, tm, LANE), lambda i, j: (ga_grp, i, 0)),
                     pl.BlockSpec((nb, tm, LANE), lambda i, j: (gb_grp, i, 0))]
        args += [o_b, gates, gates]
    in_specs += [pl.BlockSpec((k, tn), lambda i, j: (0, j), pipeline_mode=_resident(nc == tn)),
                 pl.BlockSpec((tm, tn), lambda i, j: (i, j))]
    args += [w, res]
    return pl.pallas_call(
        functools.partial(_proj_residual_kernel, merge=merge is not None, nb=nb),
        grid=(m // tm, nc // tn),
        in_specs=in_specs,
        out_specs=pl.BlockSpec((tm, tn), lambda i, j: (i, j)),
        out_shape=jax.ShapeDtypeStruct((m, nc), F32),
        scratch_shapes=[pltpu.VMEM((tm, k), BF16)],
        compiler_params=_params("parallel", "arbitrary"),
        name=name,
    )(*args)


def _cross_kernel(q_ref, mk_ref, mv_ref, o_ref, *, bph):
    for hh in range(MEM_HEADS):
        blocks = range(hh * bph, (hh + 1) * bph)
        q = jnp.concatenate([q_ref[c] for c in blocks], axis=1)
        k = jnp.concatenate([mk_ref[c] for c in blocks], axis=1)
        v = jnp.concatenate([mv_ref[c] for c in blocks], axis=1)
        s = lax.dot_general(q, k, (((1,), (1,)), ((), ())), preferred_element_type=F32)
        p = jnp.exp2(s - jnp.max(s, axis=1, keepdims=True))
        p = p / jnp.sum(p, axis=1, keepdims=True)
        o = jnp.dot(p.astype(BF16), v, preferred_element_type=F32)
        for ci, c in enumerate(blocks):
            o_ref[c] = o[:, ci * LANE:(ci + 1) * LANE].astype(o_ref.dtype)


def _cross_attention(q, mkv, b, s, n_mem, tm):
    nb, n, _ = q.shape
    tm = min(tm, s)
    per_b = s // tm
    return pl.pallas_call(
        functools.partial(_cross_kernel, bph=nb // MEM_HEADS),
        grid=(n // tm,),
        in_specs=[
            pl.BlockSpec((nb, tm, LANE), lambda i: (0, i, 0)),
            pl.BlockSpec((nb, n_mem, LANE), lambda i: (0, i // per_b, 0)),
            pl.BlockSpec((nb, n_mem, LANE), lambda i: (1, i // per_b, 0)),
        ],
        out_specs=pl.BlockSpec((nb, tm, LANE), lambda i: (0, i, 0)),
        out_shape=jax.ShapeDtypeStruct((nb, n, LANE), BF16),
        compiler_params=_params("parallel"),
        name="cross_attention",
    )(q, mkv, mkv)


def _router_kernel(h_ref, g_ref, w_ref, b_ref, z_ref, info_ref, count_ref):
    x = h_ref[...]
    z = x * lax.rsqrt(jnp.mean(x * x, axis=-1, keepdims=True) + EPS) * g_ref[...]
    z_ref[...] = z
    logits = jnp.dot(z, w_ref[...], preferred_element_type=F32, precision=lax.Precision.HIGHEST) + b_ref[...]
    lane = lax.broadcasted_iota(jnp.int32, logits.shape, 1).astype(F32)
    big = float(LANE)

    def first_where(mask):
        return jnp.min(jnp.where(mask, lane, big), axis=1, keepdims=True)

    gmask = lane < N_GROUPS
    gl = jnp.where(gmask, logits, NEG_INF)
    gmax = jnp.max(gl, axis=1, keepdims=True)
    gsum = jnp.sum(jnp.exp(gl - gmax), axis=1, keepdims=True)
    g_idx = first_where(gl == gmax)
    g_w = 1.0 / gsum
    lo = N_GROUPS + g_idx * EXPERTS_PER_GROUP
    emask = (lane >= lo) & (lane < lo + EXPERTS_PER_GROUP)
    el = jnp.where(emask, logits, NEG_INF)
    emax = jnp.max(el, axis=1, keepdims=True)
    eexp = jnp.exp(el - emax)
    prob = jnp.where(emask, eexp / jnp.sum(eexp, axis=1, keepdims=True), -1.0)
    p1 = jnp.max(prob, axis=1, keepdims=True)
    i1 = first_where(prob == p1)
    prob2 = jnp.where(lane == i1, -1.0, prob)
    p2 = jnp.max(prob2, axis=1, keepdims=True)
    i2 = first_where(prob2 == p2)
    denom = p1 + p2
    w1 = g_w * p1 / denom
    w2 = g_w * p2 / denom
    @pl.when(pl.program_id(0) == 0)
    def _():
        count_ref[...] = jnp.zeros_like(count_ref)

    tm = logits.shape[0]
    hot1 = jnp.where(lane == i1, 1.0, 0.0)
    hot2 = jnp.where(lane == i2, 1.0, 0.0)
    hot = hot1 + hot2
    row = lax.broadcasted_iota(jnp.int32, (tm, tm), 0)
    col = lax.broadcasted_iota(jnp.int32, (tm, tm), 1)
    before = jnp.where(col < row, 1.0, 0.0).astype(BF16)
    prior = jnp.dot(before, hot.astype(BF16), preferred_element_type=F32) + count_ref[0:1, :]
    r1 = jnp.sum(prior * hot1, axis=1, keepdims=True)
    r2 = jnp.sum(prior * hot2, axis=1, keepdims=True)
    count_ref[...] = count_ref[...] + jnp.sum(hot, axis=0, keepdims=True)
    info = jnp.zeros_like(logits)
    for k, val in enumerate((i1 - N_GROUPS, i2 - N_GROUPS, w1, w2, r1, r2)):
        info = jnp.where(lane == k, val, info)
    info_ref[...] = info


def _router(h, g, w_r, b_r, tm):
    n, d = h.shape
    tm = min(tm, n)
    return pl.pallas_call(
        _router_kernel,
        grid=(n // tm,),
        in_specs=[pl.BlockSpec((tm, d), lambda i: (i, 0)), pl.BlockSpec((1, d), lambda i: (0, 0)),
                  pl.BlockSpec((d, LANE), lambda i: (0, 0)), pl.BlockSpec((1, LANE), lambda i: (0, 0))],
        out_specs=[pl.BlockSpec((tm, d), lambda i: (i, 0)), pl.BlockSpec((tm, LANE), lambda i: (i, 0)),
                   pl.BlockSpec((8, LANE), lambda i: (0, 0))],
        out_shape=[jax.ShapeDtypeStruct((n, d), F32), jax.ShapeDtypeStruct((n, LANE), F32),
                   jax.ShapeDtypeStruct((8, LANE), F32)],
        compiler_params=_params("arbitrary"),
        name="moe_router",
    )(h, g.reshape(1, d).astype(F32), w_r, b_r)


def _row_copy(src_hbm, row, dst, dst_row, sem):
    return pltpu.make_async_copy(src_hbm.at[pl.ds(row, 1)], dst.at[pl.ds(dst_row, 1)], sem)


def _issue_rows(idx_ref, idx_off, src_hbm, dst, sem, n_rows):
    def issue(r, carry):
        _row_copy(src_hbm, idx_ref[0, idx_off + r], dst, r, sem).start()
        return carry

    lax.fori_loop(0, n_rows, issue, 0, unroll=GATHER_UNROLL)


def _wait_rows(src_hbm, dst, sem, n_rows):
    def wait(r, carry):
        _row_copy(src_hbm, 0, dst, r, sem).wait()
        return carry

    lax.fori_loop(0, n_rows, wait, 0, unroll=GATHER_UNROLL)


def _expert_kernel(blk_e_ref, nvalid_ref, tok0_ref, tok_next_ref, z_hbm, wg_lo, wu_lo, wd_lo, wg_hi, wu_hi, wd_hi,
                   y_ref, xbuf, sem, *, tb):
    i = pl.program_id(0)
    nvalid = nvalid_ref[0]
    slot = i % 2

    @pl.when((i == 0) & (nvalid > 0))
    def _():
        _issue_rows(tok0_ref, 0, z_hbm, xbuf.at[0], sem.at[0], tb)

    @pl.when(i + 1 < nvalid)
    def _():
        _issue_rows(tok_next_ref, 0, z_hbm, xbuf.at[1 - slot], sem.at[1 - slot], tb)

    @pl.when(i < nvalid)
    def _():
        _wait_rows(z_hbm, xbuf.at[slot], sem.at[slot], tb)
        x = xbuf[slot].astype(BF16)
        half = x.shape[1] // 2
        x_lo, x_hi = x[:, :half], x[:, half:]
        gate = (jnp.dot(x_lo, wg_lo[...], preferred_element_type=F32)
                + jnp.dot(x_hi, wg_hi[...], preferred_element_type=F32))
        up = (jnp.dot(x_lo, wu_lo[...], preferred_element_type=F32)
              + jnp.dot(x_hi, wu_hi[...], preferred_element_type=F32))
        hidden = (gate * jax.nn.sigmoid(gate) * up).astype(BF16)
        y_ref[:, :half] = jnp.dot(hidden, wd_lo[...], preferred_element_type=F32)
        y_ref[:, half:] = jnp.dot(hidden, wd_hi[...], preferred_element_type=F32)

    @pl.when(i >= nvalid)
    def _():
        y_ref[...] = jnp.zeros_like(y_ref)


def _expert_blocks(blk_e, nvalid, tok, z, w_lo, w_hi, tb):
    nblk = tok.shape[0]
    d = z.shape[1]
    ff = w_lo[0].shape[2]
    rows_half = pl.BlockSpec((None, d // 2, ff), lambda i, be, nv: (be[i], 0, 0))
    cols_half = pl.BlockSpec((None, ff, d // 2), lambda i, be, nv: (be[i], 0, 0))
    grid_spec = pltpu.PrefetchScalarGridSpec(
        num_scalar_prefetch=2,
        grid=(nblk,),
        in_specs=[
            pl.BlockSpec((None, 1, tb), lambda i, be, nv: (0, 0, 0), memory_space=pltpu.SMEM),
            pl.BlockSpec((None, 1, tb), lambda i, be, nv: (jnp.minimum(i + 1, nblk - 1), 0, 0),
                         memory_space=pltpu.SMEM),
            pl.BlockSpec(memory_space=pl.ANY),
            rows_half, rows_half, cols_half, rows_half, rows_half, cols_half,
        ],
        out_specs=pl.BlockSpec((tb, d), lambda i, be, nv: (i, 0)),
        scratch_shapes=[pltpu.VMEM((2, tb, d), F32), pltpu.SemaphoreType.DMA((2,))],
    )
    return pl.pallas_call(
        functools.partial(_expert_kernel, tb=tb),
        grid_spec=grid_spec,
        out_shape=jax.ShapeDtypeStruct((nblk * tb, d), F32),
        compiler_params=_params("arbitrary"),
        name="moe_experts",
    )(blk_e, nvalid, tok, tok, z, *w_lo, *w_hi)


def _combine_kernel(dest0_ref, dest_next_ref, y_hbm, h_ref, info_ref, g_ref, o_ref, ybuf, sem, *, tm, final_norm):
    i = pl.program_id(0)
    slot = i % 2

    def issue(idx_ref, sl):
        for k in range(2):
            _issue_rows(idx_ref, k * tm, y_hbm, ybuf.at[sl, k], sem.at[sl], tm)

    @pl.when(i == 0)
    def _():
        issue(dest0_ref, 0)

    @pl.when(i + 1 < pl.num_programs(0))
    def _():
        issue(dest_next_ref, 1 - slot)

    for k in range(2):
        _wait_rows(y_hbm, ybuf.at[slot, k], sem.at[slot], tm)
    info = info_ref[...]
    moe = ybuf[slot, 0] * info[:, 2:3] + ybuf[slot, 1] * info[:, 3:4]
    hx = h_ref[...] + moe
    if final_norm:
        hx = hx * lax.rsqrt(jnp.mean(hx * hx, axis=-1, keepdims=True) + EPS) * g_ref[...]
    o_ref[...] = hx


def _combine(dest, y, h, info, g_final, tm, final_norm):
    n, d = h.shape
    n_tiles = n // tm
    return pl.pallas_call(
        functools.partial(_combine_kernel, tm=tm, final_norm=final_norm),
        grid=(n_tiles,),
        in_specs=[
            pl.BlockSpec((None, 1, 2 * tm), lambda i: (0, 0, 0), memory_space=pltpu.SMEM),
            pl.BlockSpec((None, 1, 2 * tm), lambda i: (jnp.minimum(i + 1, n_tiles - 1), 0, 0),
                         memory_space=pltpu.SMEM),
            pl.BlockSpec(memory_space=pl.ANY),
            pl.BlockSpec((tm, d), lambda i: (i, 0)),
            pl.BlockSpec((tm, LANE), lambda i: (i, 0)),
            pl.BlockSpec((1, d), lambda i: (0, 0)),
        ],
        out_specs=pl.BlockSpec((tm, d), lambda i: (i, 0)),
        out_shape=jax.ShapeDtypeStruct((n, d), F32),
        scratch_shapes=[pltpu.VMEM((2, 2, tm, d), F32), pltpu.SemaphoreType.DMA((2,))],
        compiler_params=_params("arbitrary"),
        name="moe_combine_final_norm",
    )(dest, dest, y, h, info, g_final.reshape(1, d).astype(F32))


def _dispatch_plan(eid, rank, counts, n, tb):
    n_slots = 2 * n
    nblk = n_slots // tb + N_EXPERTS
    slot_e = eid.reshape(-1)
    rank = rank.reshape(-1)
    blocks_per_e = (counts + tb - 1) // tb
    blk_end = jnp.cumsum(blocks_per_e)
    blk_start = blk_end - blocks_per_e
    dest = blk_start[slot_e] * tb + rank
    slot_tok = jnp.arange(n_slots, dtype=jnp.int32) // 2
    tok = jnp.zeros((nblk * tb,), jnp.int32).at[dest].set(slot_tok)
    blk_ids = jnp.arange(nblk, dtype=jnp.int32)
    blk_e = jnp.minimum(jnp.sum(blk_end[None, :] <= blk_ids[:, None], axis=1), N_EXPERTS - 1).astype(jnp.int32)
    nvalid = blk_end[-1:].astype(jnp.int32)
    return blk_e, nvalid, tok.reshape(nblk, 1, tb), dest.reshape(n, 2)


def _rope_tables(positions):
    half = MLA_ROPE // 2
    inv_freq = ROPE_THETA ** (-jnp.arange(half, dtype=F32) / half)
    ang = positions.astype(F32).reshape(-1, 1) * inv_freq[None, :]
    cos, sin = jnp.cos(ang), jnp.sin(ang)
    zero = jnp.zeros_like(cos)
    cos_t = jnp.concatenate([cos, cos, zero, zero], axis=1)
    sin_lo = jnp.concatenate([-sin, zero, zero, zero], axis=1)
    sin_hi = jnp.concatenate([zero, sin, zero, zero], axis=1)
    return cos_t, sin_lo, sin_hi


def kernel(x, mem, positions, g_mix, w_in, b_forget, g_q_latent, g_kv_latent, w_q_up, w_k_up, w_v_up, w_out, g_cross, g_mem, w_cross_q, w_cross_kv, w_cross_out, g_moe, w_group_router, b_group_router, w_expert_router, b_expert_router, w_exp_gate, w_exp_up, w_exp_down, g_final):
    b, s, d = x.shape
    n = b * s
    n_mem = mem.shape[1]
    depth = w_in.shape[0]
    fw = FOX_HEADS * LANE
    h = x.reshape(n, d)
    rope_tabs = _rope_tables(positions)

    for l in range(depth):
        wi = w_in[l]
        fox_scale = (LANE ** -0.5) * LOG2E
        o_f = 3 * fw
        o_cq = o_f + FOX_HEADS
        o_kr = o_cq + 2 * MLA_RANK
        o_g = o_kr + MLA_ROPE
        w_big = jnp.concatenate([wi[:, :fw] * fox_scale, wi[:, fw:2 * fw], wi[:, o_g:]], axis=1).astype(BF16)
        w_fox_v_t = wi[:, 2 * fw:3 * fw].T.astype(BF16)
        w_small = jnp.concatenate(
            [wi[:, o_cq:o_kr], wi[:, o_kr:o_g], wi[:, o_f:o_cq],
             jnp.zeros((d, LANE - MLA_ROPE - FOX_HEADS), F32)], axis=1).astype(BF16)
        mla_scale = ((MLA_NOPE + MLA_ROPE) ** -0.5) * LOG2E
        wq = (w_q_up[l] * mla_scale).reshape(MLA_RANK, MLA_HEADS, MLA_NOPE + MLA_ROPE)
        wq = jnp.pad(wq, ((0, 0), (0, 0), (0, 2 * LANE - MLA_NOPE - MLA_ROPE))).reshape(MLA_RANK, -1).astype(BF16)
        wk = w_k_up[l].astype(BF16)
        w_mla_v_t = w_v_up[l].T.astype(BF16)

        big, small = _norm_matmul(h, 0, d, g_mix[l], w_big, blocked=True, out_dtype=BF16, tm=PROJ_TM, tn=PROJ_TN,
                                  w_extra=w_small, name="in_proj")
        nh = FOX_HEADS
        fox_v_t = _norm_matmul_t(h, 0, d, g_mix[l], w_fox_v_t, tm=PROJ_TM, tr=PROJ_TN, name="fox_v_t")
        bias = _forget_bias(small, 2 * MLA_RANK // LANE, MLA_ROPE, b_forget[l], b, s, 512)
        ff = w_exp_gate.shape[3]
        steps = _attn_steps(nh, b, s, ATTN_TILE)
        chunks = _cast_chunks(steps, N_EXPERTS, (d // 2, ff))
        if steps != _attn_steps(MLA_HEADS, b, s, ATTN_TILE):
            chunks = None

        def cast_half(half):
            if chunks is None:
                return None
            return (l, chunks, [(w_exp_gate, (d // 2, ff), (half, 0)), (w_exp_up, (d // 2, ff), (half, 0)),
                                (w_exp_down, (ff, d // 2), (0, half))])

        o_a, *expert_lo = _causal_attention(big, lambda hd: hd, None, None, big, lambda hd: nh + hd, bias,
                                            fox_v_t, nh, b, s, ATTN_TILE, 1, "fox_attention", cast_half(0))
        qm = _norm_matmul(small, 0, MLA_RANK, g_q_latent[l], wq, blocked=True, out_dtype=BF16, tm=PROJ_TM,
                          tn=PROJ_TN, rope_tabs=rope_tabs, name="mla_q_up")
        k_nope = _norm_matmul(small, 1, MLA_RANK, g_kv_latent[l], wk, blocked=True, out_dtype=BF16, tm=PROJ_TM,
                              tn=PROJ_TN, name="mla_k_up")
        mla_v_t = _norm_matmul_t(small, 1, MLA_RANK, g_kv_latent[l], w_mla_v_t, tm=PROJ_TM, tr=PROJ_TN,
                                 name="mla_v_t")
        k_rope = _krope(small, 2 * MLA_RANK // LANE, rope_tabs, 512)
        o_b, *expert_hi = _causal_attention(qm, lambda hd: 2 * hd, qm, lambda hd: 2 * hd + 1, k_nope, lambda hd: hd,
                                            k_rope, mla_v_t, MLA_HEADS, b, s, ATTN_TILE, CHUNK, "mla_attention",
                                            cast_half(1))
        if chunks is None:
            hd2 = d // 2
            expert_lo = [w_exp_gate[l, :, :hd2].astype(BF16), w_exp_up[l, :, :hd2].astype(BF16),
                         w_exp_down[l, :, :, :hd2].astype(BF16)]
            expert_hi = [w_exp_gate[l, :, hd2:].astype(BF16), w_exp_up[l, :, hd2:].astype(BF16),
                         w_exp_down[l, :, :, hd2:].astype(BF16)]
        h = _proj_residual(o_a, w_out[l].astype(BF16), h, tm=MERGE_TM, tn=d, merge=(o_b, big, 2, 3),
                           name="out_proj")
        cross_scale = ((d // MEM_HEADS) ** -0.5) * LOG2E
        q_c = _norm_matmul(h, 0, d, g_cross[l], (w_cross_q[l] * cross_scale).astype(BF16), blocked=True,
                           out_dtype=BF16, tm=PROJ_TM, tn=d, name="cross_q")
        mkv = _norm_matmul(mem.reshape(b * n_mem, d), 0, d, g_mem[l], w_cross_kv[l].astype(BF16), blocked=True,
                           out_dtype=BF16, tm=PROJ_TM, tn=PROJ_TN, name="cross_mem_kv")
        o_c = _cross_attention(q_c, mkv, b, s, n_mem, 512)
        h = _proj_residual(o_c, w_cross_out[l].astype(BF16), h, tm=MERGE_TM, tn=d, name="cross_out")
        w_r = jnp.concatenate([w_group_router[l], w_expert_router[l],
                               jnp.zeros((d, LANE - N_GROUPS - N_EXPERTS), F32)], axis=1)
        b_r = jnp.concatenate([b_group_router[l], b_expert_router[l],
                               jnp.zeros((LANE - N_GROUPS - N_EXPERTS,), F32)]).reshape(1, LANE)
        z, info, counts = _router(h, g_moe[l], w_r, b_r, 512)
        tb = 256
        eid = info[:, 0:2].astype(jnp.int32)
        rank = info[:, 4:6].astype(jnp.int32)
        counts = counts[0, N_GROUPS:N_GROUPS + N_EXPERTS].astype(jnp.int32)
        blk_e, nvalid, tok, dest = _dispatch_plan(eid, rank, counts, n, tb)
        y = _expert_blocks(blk_e, nvalid, tok, z, expert_lo, expert_hi, tb)
        tm_c = min(256, n)
        dest_blk = dest.reshape(n // tm_c, tm_c, 2).transpose(0, 2, 1).reshape(n // tm_c, 1, 2 * tm_c)
        h = _combine(dest_blk, y, h, info, g_final, tm_c, final_norm=(l + 1 == depth))
    return h.reshape(b, s, d)
```

```python
import functools
import math

import jax
import jax.numpy as jnp
from jax import lax
from jax.experimental import pallas as pl
from jax.experimental.pallas import tpu as pltpu

F32 = jnp.float32
BF16 = jnp.bfloat16

LANE = 128
BF16_SUBLANES = 16
EPS = 1e-6
CHUNK = 64
FOX_HEADS = 16
MLA_HEADS = 16
MLA_NOPE = 128
MLA_ROPE = 64
MLA_RANK = 512
ROPE_THETA = 10000.0
MEM_HEADS = 4
N_GROUPS = 8
EXPERTS_PER_GROUP = 8
N_EXPERTS = N_GROUPS * EXPERTS_PER_GROUP
LOG2E = math.log2(math.e)
VMEM_LIMIT = 56 * 1024 * 1024
NEG_INF = float("-inf")
ATTN_TILE = 1024
ATTN_HEAD_GROUP = 2
PROJ_TM = 1024
PROJ_TN = 1024
MERGE_TM = 512
GATHER_UNROLL = 8


def _params(*sem):
    return pltpu.CompilerParams(dimension_semantics=sem, vmem_limit_bytes=VMEM_LIMIT)


def _resident(whole):
    return pl.Buffered(1) if whole else None


def _rope_rotate(x, cos, sin_lo, sin_hi):
    return x * cos + pltpu.roll(x, 96, 1) * sin_lo + pltpu.roll(x, 32, 1) * sin_hi


def _norm_matmul_kernel(*refs, rope, blocked, nblk, extra):
    x_ref, g_ref, w_ref = refs[:3]
    rest = list(refs[3:])
    if rope:
        cos_ref, slo_ref, shi_ref = rest[:3]
        rest = rest[3:]
    if extra:
        we_ref, o_ref, oe_ref, xn_ref = rest
    else:
        o_ref, xn_ref = rest

    @pl.when(pl.program_id(1) == 0)
    def _():
        x = x_ref[...].astype(F32)
        y = x * lax.rsqrt(jnp.mean(x * x, axis=-1, keepdims=True) + EPS)
        xn_ref[...] = (y * g_ref[...]).astype(BF16)
        if extra:
            oe_ref[...] = jnp.dot(xn_ref[...], we_ref[...], preferred_element_type=F32)

    acc = jnp.dot(xn_ref[...], w_ref[...], preferred_element_type=F32)
    if blocked:
        for c in range(nblk):
            blk = acc[:, c * LANE:(c + 1) * LANE]
            if rope and c % 2 == 1:
                blk = _rope_rotate(blk, cos_ref[...], slo_ref[...], shi_ref[...])
            o_ref[c] = blk.astype(o_ref.dtype)
    else:
        o_ref[...] = acc.astype(o_ref.dtype)


def _norm_matmul(x, x_col, k, g, w, *, blocked, out_dtype, tm, tn, rope_tabs=None, w_extra=None, name=None):
    m = x.shape[0]
    nc = w.shape[1]
    tm = min(tm, m)
    tn = min(tn, nc)
    assert m % tm == 0 and nc % tn == 0 and tn % LANE == 0
    nblk = tn // LANE
    in_specs = [
        pl.BlockSpec((tm, k), lambda i, j: (i, x_col)),
        pl.BlockSpec((1, k), lambda i, j: (0, 0)),
        pl.BlockSpec((k, tn), lambda i, j: (0, j), pipeline_mode=_resident(nc == tn)),
    ]
    args = [x, g.reshape(1, k).astype(F32), w]
    if rope_tabs is not None:
        in_specs += [pl.BlockSpec((tm, LANE), lambda i, j: (i, 0))] * 3
        args += list(rope_tabs)
    if blocked:
        out_shape = jax.ShapeDtypeStruct((nc // LANE, m, LANE), out_dtype)
        out_spec = pl.BlockSpec((nblk, tm, LANE), lambda i, j: (j, i, 0))
    else:
        out_shape = jax.ShapeDtypeStruct((m, nc), out_dtype)
        out_spec = pl.BlockSpec((tm, tn), lambda i, j: (i, j))
    if w_extra is not None:
        ne = w_extra.shape[1]
        in_specs.append(pl.BlockSpec((k, ne), lambda i, j: (0, 0), pipeline_mode=_resident(True)))
        args.append(w_extra)
        out_spec = [out_spec, pl.BlockSpec((tm, ne), lambda i, j: (i, 0))]
        out_shape = [out_shape, jax.ShapeDtypeStruct((m, ne), F32)]
    return pl.pallas_call(
        functools.partial(_norm_matmul_kernel, rope=rope_tabs is not None, blocked=blocked, nblk=nblk,
                          extra=w_extra is not None),
        grid=(m // tm, nc // tn),
        in_specs=in_specs,
        out_specs=out_spec,
        out_shape=out_shape,
        scratch_shapes=[pltpu.VMEM((tm, k), BF16)],
        compiler_params=_params("parallel", "arbitrary"),
        name=name,
    )(*args)


def _krope_kernel(x_ref, cos_ref, slo_ref, shi_ref, o_ref):
    o_ref[...] = _rope_rotate(x_ref[...], cos_ref[...], slo_ref[...], shi_ref[...]).astype(o_ref.dtype)


def _krope(small, col_blk, rope_tabs, tm):
    m = small.shape[0]
    tm = min(tm, m)
    return pl.pallas_call(
        _krope_kernel,
        grid=(m // tm,),
        in_specs=[pl.BlockSpec((tm, LANE), lambda i: (i, col_blk))] + [pl.BlockSpec((tm, LANE), lambda i: (i, 0))] * 3,
        out_specs=pl.BlockSpec((tm, LANE), lambda i: (i, 0)),
        out_shape=jax.ShapeDtypeStruct((m, LANE), BF16),
        compiler_params=_params("parallel"),
        name="k_rope",
    )(small, *rope_tabs)


N_SPLIT = 3


def _forget_bias_kernel(f_ref, b_ref, o_ref, carry_ref, *, ts, lane0, heads):
    @pl.when(pl.program_id(1) == 0)
    def _():
        carry_ref[...] = jnp.zeros_like(carry_ref)

    x = f_ref[...] + b_ref[...]
    log_f = jnp.minimum(x, 0.0) - jnp.log1p(jnp.exp(-jnp.abs(x)))
    row = lax.broadcasted_iota(jnp.int32, (ts, ts), 0)
    col = lax.broadcasted_iota(jnp.int32, (ts, ts), 1)
    lower = (col <= row).astype(F32)
    c = jnp.dot(lower, log_f, preferred_element_type=F32, precision=lax.Precision.HIGHEST) + carry_ref[0:1, :]
    carry_ref[...] = jnp.broadcast_to(c[ts - 1:ts, :], carry_ref.shape)
    rest = -(c * LOG2E)
    src = lax.broadcasted_iota(jnp.int32, (LANE, LANE), 0) - lane0
    dst = lax.broadcasted_iota(jnp.int32, (LANE, LANE), 1)
    out = jnp.zeros((ts, LANE), F32)
    for pi in range(N_SPLIT):
        piece = rest.astype(BF16)
        rest = rest - piece.astype(F32)
        place = jnp.where((src >= 0) & (src < heads) & (dst == N_SPLIT * src + pi), 1.0, 0.0).astype(BF16)
        out = out + jnp.dot(piece, place, preferred_element_type=F32)
    o_ref[...] = out.astype(o_ref.dtype)


def _forget_bias(small, col_blk, lane0, b_forget, b, s, ts):
    heads = b_forget.shape[0]
    assert N_SPLIT * heads <= LANE
    ts = min(ts, s)
    per_b = s // ts
    b_row = jnp.zeros((1, LANE), F32).at[0, lane0:lane0 + heads].set(b_forget.astype(F32))
    return pl.pallas_call(
        functools.partial(_forget_bias_kernel, ts=ts, lane0=lane0, heads=heads),
        grid=(b, per_b),
        in_specs=[pl.BlockSpec((ts, LANE), lambda bi, j: (bi * per_b + j, col_blk)),
                  pl.BlockSpec((1, LANE), lambda bi, j: (0, 0))],
        out_specs=pl.BlockSpec((ts, LANE), lambda bi, j: (bi * per_b + j, 0)),
        out_shape=jax.ShapeDtypeStruct((b * s, LANE), BF16),
        scratch_shapes=[pltpu.VMEM((8, LANE), F32)],
        compiler_params=_params("parallel", "arbitrary"),
        name="forget_bias",
    )(small, b_row)


def _norm_matmul_t_kernel(x_ref, g_ref, w_ref, o_ref, xn_ref):
    @pl.when(pl.program_id(1) == 0)
    def _():
        x = x_ref[...].astype(F32)
        y = x * lax.rsqrt(jnp.mean(x * x, axis=-1, keepdims=True) + EPS)
        xn_ref[...] = (y * g_ref[...]).astype(BF16)

    o_ref[...] = lax.dot_general(w_ref[...], xn_ref[...], (((1,), (1,)), ((), ())),
                                 preferred_element_type=F32).astype(o_ref.dtype)


def _norm_matmul_t(x, x_col, k, g, w_t, *, tm, tr, name=None):
    m = x.shape[0]
    r = w_t.shape[0]
    tm = min(tm, m)
    tr = min(tr, r)
    return pl.pallas_call(
        _norm_matmul_t_kernel,
        grid=(m // tm, r // tr),
        in_specs=[pl.BlockSpec((tm, k), lambda i, j: (i, x_col)), pl.BlockSpec((1, k), lambda i, j: (0, 0)),
                  pl.BlockSpec((tr, k), lambda i, j: (j, 0))],
        out_specs=pl.BlockSpec((tr, tm), lambda i, j: (j, i)),
        out_shape=jax.ShapeDtypeStruct((r, m), BF16),
        scratch_shapes=[pltpu.VMEM((tm, k), BF16)],
        compiler_params=_params("parallel", "arbitrary"),
        name=name,
    )(x, g.reshape(1, k).astype(F32), w_t)


N_ATTN_SCRATCH = 7


def _attn_kernel(*refs, t, unit, q_ones, group, n_cast):
    n_in = 3 if q_ones else 4
    kb_ref = refs[group * n_in]
    cast_src = refs[group * n_in + 1:group * n_in + 1 + n_cast]
    n_inputs = group * n_in + 1 + n_cast
    o_ref = refs[n_inputs]
    cast_dst = refs[n_inputs + 1:n_inputs + 1 + n_cast]
    scratch = refs[n_inputs + 1 + n_cast:]
    i = pl.program_id(2)
    streams = []
    for g in range(group):
        head_refs = refs[g * n_in:(g + 1) * n_in]
        if q_ones:
            qa_ref, ka_ref, vt_ref = head_refs
            lane = lax.broadcasted_iota(jnp.int32, (t, LANE), 1)
            lo = N_SPLIT * (pl.program_id(1) * group + g)
            q_b = jnp.where((lane >= lo) & (lane < lo + N_SPLIT), 1.0, 0.0).astype(BF16)
        else:
            qa_ref, qb_ref, ka_ref, vt_ref = head_refs
            q_b = qb_ref[...]
        q = jnp.concatenate([qa_ref[...], q_b], axis=1)
        streams.append((q, ka_ref, vt_ref) + tuple(scratch[g * N_ATTN_SCRATCH:(g + 1) * N_ATTN_SCRATCH]))

    def stage(name, j=None):
        for st in streams:
            _attn_stream(st, kb_ref, j, t, unit, name)

    def body(tau, carry):
        for st in streams:
            _attn_stream(st, kb_ref, tau - 2, t, unit, "values")
            _attn_stream(st, kb_ref, None, t, unit, "softmax")
            _attn_stream(st, kb_ref, tau, t, unit, "logits")
        return carry

    stage("init")
    stage("logits", 0)
    for src_ref, dst_ref in zip(cast_src, cast_dst):
        dst_ref[...] = src_ref[...].astype(dst_ref.dtype)
    lax.fori_loop(1, i + 1, body, 0)
    for g, st in enumerate(streams):
        _attn_stream(st, kb_ref, i - 1, t, unit, "values")
        _attn_stream(st, kb_ref, None, t, unit, "softmax_masked")
        _attn_stream(st, kb_ref, i, t, unit, "values")
        o_ref[g] = jnp.transpose(st[5][...] / st[4][...]).astype(o_ref.dtype)


def _attn_stream(st, kb_ref, j, t, unit, stage):
    q, ka_ref, vt_ref, m_scr, l_scr, acc_scr, s_ref, p_ref, a_ref, bmax_ref = st
    if stage == "init":
        m_scr[...] = jnp.full(m_scr.shape, NEG_INF, F32)
        l_scr[...] = jnp.zeros(l_scr.shape, F32)
        acc_scr[...] = jnp.zeros(acc_scr.shape, F32)
        p_ref[...] = jnp.zeros(p_ref.shape, BF16)
        a_ref[...] = jnp.ones(a_ref.shape, F32)
    elif stage == "logits":
        off = pl.multiple_of(j * t, t)
        k = jnp.concatenate([ka_ref[pl.ds(off, t), :], kb_ref[pl.ds(off, t), :]], axis=1)
        s = lax.dot_general(k, q, (((1,), (1,)), ((), ())), preferred_element_type=F32)
        s_ref[...] = s
        bmax_ref[...] = jnp.max(s, axis=0, keepdims=True)
    elif stage == "values":
        off = pl.multiple_of(jnp.maximum(j, 0) * t, t)
        pv = jnp.dot(vt_ref[:, pl.ds(off, t)], p_ref[...], preferred_element_type=F32)
        acc_scr[...] = a_ref[...] * acc_scr[...] + pv
    else:
        masked = stage == "softmax_masked"
        s = s_ref[...]
        if masked:
            key = lax.broadcasted_iota(jnp.int32, (t, t), 0)
            qry = lax.broadcasted_iota(jnp.int32, (t, t), 1)
            s = jnp.where((key // unit) <= (qry // unit), s, NEG_INF)
            block_max = jnp.max(s, axis=0, keepdims=True)
        else:
            block_max = bmax_ref[...]
        m_prev = m_scr[...]
        m_new = jnp.maximum(m_prev, block_max)
        alpha = jnp.exp2(m_prev - m_new)
        p = jnp.exp2(s - m_new)
        l_scr[...] = alpha * l_scr[...] + jnp.sum(p, axis=0, keepdims=True)
        m_scr[...] = m_new
        p_ref[...] = p.astype(BF16)
        a_ref[...] = alpha


def _attn_steps(heads, b, s, t):
    return b * (heads // ATTN_HEAD_GROUP) * (s // min(t, s))


def _cast_chunks(steps, n_slices, row_counts):
    chunks, rem = divmod(steps, n_slices)
    ok = rem == 0 and chunks > 0 and all(r % (chunks * BF16_SUBLANES) == 0 for r in row_counts)
    return chunks if ok else None


def _causal_attention(q_a, qa_blk, q_b, qb_blk, k_a, ka_blk, k_b, v_t, heads, b, s, t, unit, name, side_cast=None):
    t = min(t, s)
    group = ATTN_HEAD_GROUP
    assert t % unit == 0 and heads % group == 0
    nq = s // t
    once = pl.Buffered(1)
    in_specs, args = [], []
    for g in range(group):
        def q_spec(blk, g=g):
            return pl.BlockSpec((None, t, LANE), lambda bi, hi, i: (blk(hi * group + g), bi * nq + i, 0))

        in_specs.append(q_spec(qa_blk))
        args.append(q_a)
        if q_b is not None:
            in_specs.append(q_spec(qb_blk))
            args.append(q_b)
        in_specs.append(pl.BlockSpec((None, s, LANE), lambda bi, hi, i, g=g: (ka_blk(hi * group + g), bi, 0),
                                     pipeline_mode=once))
        in_specs.append(pl.BlockSpec((LANE, s), lambda bi, hi, i, g=g: (hi * group + g, bi), pipeline_mode=once))
        args += [k_a, v_t]
    in_specs.append(pl.BlockSpec((s, LANE), lambda bi, hi, i: (bi, 0), pipeline_mode=once))
    args.append(k_b)
    stat = pltpu.VMEM((1, t), F32)
    per_head_scratch = [stat, stat, pltpu.VMEM((LANE, t), F32), pltpu.VMEM((t, t), F32),
                        pltpu.VMEM((t, t), BF16), stat, stat]
    assert len(per_head_scratch) == N_ATTN_SCRATCH
    out_specs = [pl.BlockSpec((group, t, LANE), lambda bi, hi, i: (hi, bi * nq + i, 0))]
    out_shape = [jax.ShapeDtypeStruct((heads, b * s, LANE), BF16)]
    n_cast = 0
    if side_cast is not None:
        layer, chunks, items = side_cast
        n_cast = len(items)

        def step_of(bi, hi, i):
            return (bi * (heads // group) + hi) * nq + i

        for src, (rows, cols), (row_half, col_half) in items:
            rb = rows // chunks
            in_specs.append(pl.BlockSpec(
                (None, None, rb, cols),
                lambda bi, hi, i, rh=row_half, ch=col_half: (layer, step_of(bi, hi, i) // chunks,
                                                             rh * chunks + step_of(bi, hi, i) % chunks, ch)))
            args.append(src)
            out_specs.append(pl.BlockSpec(
                (None, rb, cols), lambda bi, hi, i: (step_of(bi, hi, i) // chunks, step_of(bi, hi, i) % chunks, 0)))
            out_shape.append(jax.ShapeDtypeStruct((src.shape[1], rows, cols), BF16))
    return pl.pallas_call(
        functools.partial(_attn_kernel, t=t, unit=unit, q_ones=q_b is None, group=group, n_cast=n_cast),
        grid=(b, heads // group, nq),
        in_specs=in_specs,
        out_specs=out_specs,
        out_shape=out_shape,
        scratch_shapes=per_head_scratch * group,
        compiler_params=_params("parallel", "parallel", "arbitrary"),
        name=name,
    )(*args)


def _proj_residual_kernel(*refs, merge, nb):
    if merge:
        oa_ref, ob_ref, ga_ref, gb_ref, w_ref, res_ref, o_ref, a_scr = refs
    else:
        oa_ref, w_ref, res_ref, o_ref, a_scr = refs

    @pl.when(pl.program_id(1) == 0)
    def _():
        for c in range(nb):
            if merge:
                a = (jax.nn.sigmoid(ga_ref[c].astype(F32)) * oa_ref[c].astype(F32)
                     + jax.nn.sigmoid(gb_ref[c].astype(F32)) * ob_ref[c].astype(F32))
            else:
                a = oa_ref[c]
            a_scr[:, c * LANE:(c + 1) * LANE] = a.astype(BF16)

    o_ref[...] = res_ref[...] + jnp.dot(a_scr[...], w_ref[...], preferred_element_type=F32)


def _proj_residual(o_a, w, res, *, tm, tn, merge=None, name=None):
    nb, m, _ = o_a.shape
    k = nb * LANE
    nc = w.shape[1]
    tm = min(tm, m)
    tn = min(tn, nc)
    head_spec = pl.BlockSpec((nb, tm, LANE), lambda i, j: (0, i, 0))
    in_specs = [head_spec]
    args = [o_a]
    if merge is not None:
        o_b, gates, ga_grp, gb_grp = merge
        in_specs += [head_spec,
                     pl.BlockSpec((nb, tm, LANE), lambda i, j: (ga_grp, i, 0)),
                     pl.BlockSpec((nb, tm, LANE), lambda i, j: (gb_grp, i, 0))]
        args += [o_b, gates, gates]
    in_specs += [pl.BlockSpec((k, tn), lambda i, j: (0, j), pipeline_mode=_resident(nc == tn)),
                 pl.BlockSpec((tm, tn), lambda i, j: (i, j))]
    args += [w, res]
    return pl.pallas_call(
        functools.partial(_proj_residual_kernel, merge=merge is not None, nb=nb),
        grid=(m // tm, nc // tn),
        in_specs=in_specs,
        out_specs=pl.BlockSpec((tm, tn), lambda i, j: (i, j)),
        out_shape=jax.ShapeDtypeStruct((m, nc), F32),
        scratch_shapes=[pltpu.VMEM((tm, k), BF16)],
        compiler_params=_params("parallel", "arbitrary"),
        name=name,
    )(*args)


def _cross_kernel(h_ref, g_ref, wq_ref, mk_ref, mv_ref, wo_ref, o_ref, *, bph):
    x = h_ref[...]
    xn = (x * lax.rsqrt(jnp.mean(x * x, axis=-1, keepdims=True) + EPS) * g_ref[...]).astype(BF16)
    q_all = jnp.dot(xn, wq_ref[...], preferred_element_type=F32).astype(BF16)
    hd = bph * LANE
    heads_out = []
    for hh in range(MEM_HEADS):
        blocks = range(hh * bph, (hh + 1) * bph)
        q = q_all[:, hh * hd:(hh + 1) * hd]
        k = jnp.concatenate([mk_ref[c] for c in blocks], axis=1)
        v = jnp.concatenate([mv_ref[c] for c in blocks], axis=1)
        s = lax.dot_general(q, k, (((1,), (1,)), ((), ())), preferred_element_type=F32)
        p = jnp.exp2(s - jnp.max(s, axis=1, keepdims=True))
        p = p / jnp.sum(p, axis=1, keepdims=True)
        heads_out.append(jnp.dot(p.astype(BF16), v, preferred_element_type=F32).astype(BF16))
    o = jnp.concatenate(heads_out, axis=1)
    o_ref[...] = x + jnp.dot(o, wo_ref[...], preferred_element_type=F32)


def _cross_attention_layer(h, g, wq, mkv, wo, b, s, n_mem, tm):
    n, d = h.shape
    nb = d // LANE
    tm = min(tm, s)
    per_b = s // tm
    whole = pl.Buffered(1)
    return pl.pallas_call(
        functools.partial(_cross_kernel, bph=nb // MEM_HEADS),
        grid=(n // tm,),
        in_specs=[
            pl.BlockSpec((tm, d), lambda i: (i, 0)),
            pl.BlockSpec((1, d), lambda i: (0, 0)),
            pl.BlockSpec((d, d), lambda i: (0, 0), pipeline_mode=whole),
            pl.BlockSpec((nb, n_mem, LANE), lambda i: (0, i // per_b, 0)),
            pl.BlockSpec((nb, n_mem, LANE), lambda i: (1, i // per_b, 0)),
            pl.BlockSpec((d, d), lambda i: (0, 0), pipeline_mode=whole),
        ],
        out_specs=pl.BlockSpec((tm, d), lambda i: (i, 0)),
        out_shape=jax.ShapeDtypeStruct((n, d), F32),
        compiler_params=_params("parallel"),
        name="cross_attention_layer",
    )(h, g.reshape(1, d).astype(F32), wq, mkv, mkv, wo)


def _router_kernel(h_ref, g_ref, w_ref, b_ref, z_ref, info_ref, count_ref):
    x = h_ref[...]
    z = x * lax.rsqrt(jnp.mean(x * x, axis=-1, keepdims=True) + EPS) * g_ref[...]
    z_ref[...] = z
    logits = jnp.dot(z, w_ref[...], preferred_element_type=F32, precision=lax.Precision.HIGHEST) + b_ref[...]
    lane = lax.broadcasted_iota(jnp.int32, logits.shape, 1).astype(F32)
    big = float(LANE)

    def first_where(mask):
        return jnp.min(jnp.where(mask, lane, big), axis=1, keepdims=True)

    gmask = lane < N_GROUPS
    gl = jnp.where(gmask, logits, NEG_INF)
    gmax = jnp.max(gl, axis=1, keepdims=True)
    gsum = jnp.sum(jnp.exp(gl - gmax), axis=1, keepdims=True)
    g_idx = first_where(gl == gmax)
    g_w = 1.0 / gsum
    lo = N_GROUPS + g_idx * EXPERTS_PER_GROUP
    emask = (lane >= lo) & (lane < lo + EXPERTS_PER_GROUP)
    el = jnp.where(emask, logits, NEG_INF)
    emax = jnp.max(el, axis=1, keepdims=True)
    eexp = jnp.exp(el - emax)
    prob = jnp.where(emask, eexp / jnp.sum(eexp, axis=1, keepdims=True), -1.0)
    p1 = jnp.max(prob, axis=1, keepdims=True)
    i1 = first_where(prob == p1)
    prob2 = jnp.where(lane == i1, -1.0, prob)
    p2 = jnp.max(prob2, axis=1, keepdims=True)
    i2 = first_where(prob2 == p2)
    denom = p1 + p2
    w1 = g_w * p1 / denom
    w2 = g_w * p2 / denom
    @pl.when(pl.program_id(0) == 0)
    def _():
        count_ref[...] = jnp.zeros_like(count_ref)

    tm = logits.shape[0]
    hot1 = jnp.where(lane == i1, 1.0, 0.0)
    hot2 = jnp.where(lane == i2, 1.0, 0.0)
    hot = hot1 + hot2
    row = lax.broadcasted_iota(jnp.int32, (tm, tm), 0)
    col = lax.broadcasted_iota(jnp.int32, (tm, tm), 1)
    before = jnp.where(col < row, 1.0, 0.0).astype(BF16)
    prior = jnp.dot(before, hot.astype(BF16), preferred_element_type=F32) + count_ref[0:1, :]
    r1 = jnp.sum(prior * hot1, axis=1, keepdims=True)
    r2 = jnp.sum(prior * hot2, axis=1, keepdims=True)
    count_ref[...] = count_ref[...] + jnp.sum(hot, axis=0, keepdims=True)
    info = jnp.zeros_like(logits)
    for k, val in enumerate((i1 - N_GROUPS, i2 - N_GROUPS, w1, w2, r1, r2)):
        info = jnp.where(lane == k, val, info)
    info_ref[...] = info


def _router(h, g, w_r, b_r, tm):
    n, d = h.shape
    tm = min(tm, n)
    return pl.pallas_call(
        _router_kernel,
        grid=(n // tm,),
        in_specs=[pl.BlockSpec((tm, d), lambda i: (i, 0)), pl.BlockSpec((1, d), lambda i: (0, 0)),
                  pl.BlockSpec((d, LANE), lambda i: (0, 0)), pl.BlockSpec((1, LANE), lambda i: (0, 0))],
        out_specs=[pl.BlockSpec((tm, d), lambda i: (i, 0)), pl.BlockSpec((tm, LANE), lambda i: (i, 0)),
                   pl.BlockSpec((8, LANE), lambda i: (0, 0))],
        out_shape=[jax.ShapeDtypeStruct((n, d), F32), jax.ShapeDtypeStruct((n, LANE), F32),
                   jax.ShapeDtypeStruct((8, LANE), F32)],
        compiler_params=_params("arbitrary"),
        name="moe_router",
    )(h, g.reshape(1, d).astype(F32), w_r, b_r)


def _row_copy(src_hbm, row, dst, dst_row, sem):
    return pltpu.make_async_copy(src_hbm.at[pl.ds(row, 1)], dst.at[pl.ds(dst_row, 1)], sem)


def _issue_rows(idx_ref, idx_off, src_hbm, dst, sem, n_rows):
    def issue(r, carry):
        _row_copy(src_hbm, idx_ref[0, idx_off + r], dst, r, sem).start()
        return carry

    lax.fori_loop(0, n_rows, issue, 0, unroll=GATHER_UNROLL)


def _wait_rows(src_hbm, dst, sem, n_rows):
    def wait(r, carry):
        _row_copy(src_hbm, 0, dst, r, sem).wait()
        return carry

    lax.fori_loop(0, n_rows, wait, 0, unroll=GATHER_UNROLL)


def _expert_kernel(blk_e_ref, nvalid_ref, tok0_ref, tok_next_ref, z_hbm, wg_lo, wu_lo, wd_lo, wg_hi, wu_hi, wd_hi,
                   y_ref, xbuf, sem, *, tb):
    i = pl.program_id(0)
    nvalid = nvalid_ref[0]
    slot = i % 2

    @pl.when((i == 0) & (nvalid > 0))
    def _():
        _issue_rows(tok0_ref, 0, z_hbm, xbuf.at[0], sem.at[0], tb)

    @pl.when(i + 1 < nvalid)
    def _():
        _issue_rows(tok_next_ref, 0, z_hbm, xbuf.at[1 - slot], sem.at[1 - slot], tb)

    @pl.when(i < nvalid)
    def _():
        _wait_rows(z_hbm, xbuf.at[slot], sem.at[slot], tb)
        x = xbuf[slot].astype(BF16)
        half = x.shape[1] // 2
        x_lo, x_hi = x[:, :half], x[:, half:]
        gate = (jnp.dot(x_lo, wg_lo[...], preferred_element_type=F32)
                + jnp.dot(x_hi, wg_hi[...], preferred_element_type=F32))
        up = (jnp.dot(x_lo, wu_lo[...], preferred_element_type=F32)
              + jnp.dot(x_hi, wu_hi[...], preferred_element_type=F32))
        hidden = (gate * jax.nn.sigmoid(gate) * up).astype(BF16)
        y_ref[:, :half] = jnp.dot(hidden, wd_lo[...], preferred_element_type=F32)
        y_ref[:, half:] = jnp.dot(hidden, wd_hi[...], preferred_element_type=F32)

    @pl.when(i >= nvalid)
    def _():
        y_ref[...] = jnp.zeros_like(y_ref)


def _expert_blocks(blk_e, nvalid, tok, z, w_lo, w_hi, tb):
    nblk = tok.shape[0]
    d = z.shape[1]
    ff = w_lo[0].shape[2]
    rows_half = pl.BlockSpec((None, d // 2, ff), lambda i, be, nv: (be[i], 0, 0))
    cols_half = pl.BlockSpec((None, ff, d // 2), lambda i, be, nv: (be[i], 0, 0))
    grid_spec = pltpu.PrefetchScalarGridSpec(
        num_scalar_prefetch=2,
        grid=(nblk,),
        in_specs=[
            pl.BlockSpec((None, 1, tb), lambda i, be, nv: (0, 0, 0), memory_space=pltpu.SMEM),
            pl.BlockSpec((None, 1, tb), lambda i, be, nv: (jnp.minimum(i + 1, nblk - 1), 0, 0),
                         memory_space=pltpu.SMEM),
            pl.BlockSpec(memory_space=pl.ANY),
            rows_half, rows_half, cols_half, rows_half, rows_half, cols_half,
        ],
        out_specs=pl.BlockSpec((tb, d), lambda i, be, nv: (i, 0)),
        scratch_shapes=[pltpu.VMEM((2, tb, d), F32), pltpu.SemaphoreType.DMA((2,))],
    )
    return pl.pallas_call(
        functools.partial(_expert_kernel, tb=tb),
        grid_spec=grid_spec,
        out_shape=jax.ShapeDtypeStruct((nblk * tb, d), F32),
        compiler_params=_params("arbitrary"),
        name="moe_experts",
    )(blk_e, nvalid, tok, tok, z, *w_lo, *w_hi)


def _combine_kernel(dest0_ref, dest_next_ref, y_hbm, h_ref, info_ref, g_ref, o_ref, ybuf, sem, *, tm, final_norm):
    i = pl.program_id(0)
    slot = i % 2

    def issue(idx_ref, sl):
        for k in range(2):
            _issue_rows(idx_ref, k * tm, y_hbm, ybuf.at[sl, k], sem.at[sl], tm)

    @pl.when(i == 0)
    def _():
        issue(dest0_ref, 0)

    @pl.when(i + 1 < pl.num_programs(0))
    def _():
        issue(dest_next_ref, 1 - slot)

    for k in range(2):
        _wait_rows(y_hbm, ybuf.at[slot, k], sem.at[slot], tm)
    info = info_ref[...]
    moe = ybuf[slot, 0] * info[:, 2:3] + ybuf[slot, 1] * info[:, 3:4]
    hx = h_ref[...] + moe
    if final_norm:
        hx = hx * lax.rsqrt(jnp.mean(hx * hx, axis=-1, keepdims=True) + EPS) * g_ref[...]
    o_ref[...] = hx


def _combine(dest, y, h, info, g_final, tm, final_norm):
    n, d = h.shape
    n_tiles = n // tm
    return pl.pallas_call(
        functools.partial(_combine_kernel, tm=tm, final_norm=final_norm),
        grid=(n_tiles,),
        in_specs=[
            pl.BlockSpec((None, 1, 2 * tm), lambda i: (0, 0, 0), memory_space=pltpu.SMEM),
            pl.BlockSpec((None, 1, 2 * tm), lambda i: (jnp.minimum(i + 1, n_tiles - 1), 0, 0),
                         memory_space=pltpu.SMEM),
            pl.BlockSpec(memory_space=pl.ANY),
            pl.BlockSpec((tm, d), lambda i: (i, 0)),
            pl.BlockSpec((tm, LANE), lambda i: (i, 0)),
            pl.BlockSpec((1, d), lambda i: (0, 0)),
        ],
        out_specs=pl.BlockSpec((tm, d), lambda i: (i, 0)),
        out_shape=jax.ShapeDtypeStruct((n, d), F32),
        scratch_shapes=[pltpu.VMEM((2, 2, tm, d), F32), pltpu.SemaphoreType.DMA((2,))],
        compiler_params=_params("arbitrary"),
        name="moe_combine_final_norm",
    )(dest, dest, y, h, info, g_final.reshape(1, d).astype(F32))


def _dispatch_plan(eid, rank, counts, n, tb):
    n_slots = 2 * n
    nblk = n_slots // tb + N_EXPERTS
    slot_e = eid.reshape(-1)
    rank = rank.reshape(-1)
    blocks_per_e = (counts + tb - 1) // tb
    blk_end = jnp.cumsum(blocks_per_e)
    blk_start = blk_end - blocks_per_e
    dest = blk_start[slot_e] * tb + rank
    slot_tok = jnp.arange(n_slots, dtype=jnp.int32) // 2
    tok = jnp.zeros((nblk * tb,), jnp.int32).at[dest].set(slot_tok)
    blk_ids = jnp.arange(nblk, dtype=jnp.int32)
    blk_e = jnp.minimum(jnp.sum(blk_end[None, :] <= blk_ids[:, None], axis=1), N_EXPERTS - 1).astype(jnp.int32)
    nvalid = blk_end[-1:].astype(jnp.int32)
    return blk_e, nvalid, tok.reshape(nblk, 1, tb), dest.reshape(n, 2)


def _rope_tables(positions):
    half = MLA_ROPE // 2
    inv_freq = ROPE_THETA ** (-jnp.arange(half, dtype=F32) / half)
    ang = positions.astype(F32).reshape(-1, 1) * inv_freq[None, :]
    cos, sin = jnp.cos(ang), jnp.sin(ang)
    zero = jnp.zeros_like(cos)
    cos_t = jnp.concatenate([cos, cos, zero, zero], axis=1)
    sin_lo = jnp.concatenate([-sin, zero, zero, zero], axis=1)
    sin_hi = jnp.concatenate([zero, sin, zero, zero], axis=1)
    return cos_t, sin_lo, sin_hi


def kernel(x, mem, positions, g_mix, w_in, b_forget, g_q_latent, g_kv_latent, w_q_up, w_k_up, w_v_up, w_out, g_cross, g_mem, w_cross_q, w_cross_kv, w_cross_out, g_moe, w_group_router, b_group_router, w_expert_router, b_expert_router, w_exp_gate, w_exp_up, w_exp_down, g_final):
    b, s, d = x.shape
    n = b * s
    n_mem = mem.shape[1]
    depth = w_in.shape[0]
    fw = FOX_HEADS * LANE
    h = x.reshape(n, d)
    rope_tabs = _rope_tables(positions)

    for l in range(depth):
        wi = w_in[l]
        fox_scale = (LANE ** -0.5) * LOG2E
        o_f = 3 * fw
        o_cq = o_f + FOX_HEADS
        o_kr = o_cq + 2 * MLA_RANK
        o_g = o_kr + MLA_ROPE
        w_big = jnp.concatenate([wi[:, :fw] * fox_scale, wi[:, fw:2 * fw], wi[:, o_g:]], axis=1).astype(BF16)
        w_fox_v_t = wi[:, 2 * fw:3 * fw].T.astype(BF16)
        w_small = jnp.concatenate(
            [wi[:, o_cq:o_kr], wi[:, o_kr:o_g], wi[:, o_f:o_cq],
             jnp.zeros((d, LANE - MLA_ROPE - FOX_HEADS), F32)], axis=1).astype(BF16)
        mla_scale = ((MLA_NOPE + MLA_ROPE) ** -0.5) * LOG2E
        wq = (w_q_up[l] * mla_scale).reshape(MLA_RANK, MLA_HEADS, MLA_NOPE + MLA_ROPE)
        wq = jnp.pad(wq, ((0, 0), (0, 0), (0, 2 * LANE - MLA_NOPE - MLA_ROPE))).reshape(MLA_RANK, -1).astype(BF16)
        wk = w_k_up[l].astype(BF16)
        w_mla_v_t = w_v_up[l].T.astype(BF16)

        big, small = _norm_matmul(h, 0, d, g_mix[l], w_big, blocked=True, out_dtype=BF16, tm=PROJ_TM, tn=PROJ_TN,
                                  w_extra=w_small, name="in_proj")
        nh = FOX_HEADS
        fox_v_t = _norm_matmul_t(h, 0, d, g_mix[l], w_fox_v_t, tm=PROJ_TM, tr=PROJ_TN, name="fox_v_t")
        bias = _forget_bias(small, 2 * MLA_RANK // LANE, MLA_ROPE, b_forget[l], b, s, 512)
        ff = w_exp_gate.shape[3]
        steps = _attn_steps(nh, b, s, ATTN_TILE)
        chunks = _cast_chunks(steps, N_EXPERTS, (d // 2, ff))
        if steps != _attn_steps(MLA_HEADS, b, s, ATTN_TILE):
            chunks = None

        def cast_half(half):
            if chunks is None:
                return None
            return (l, chunks, [(w_exp_gate, (d // 2, ff), (half, 0)), (w_exp_up, (d // 2, ff), (half, 0)),
                                (w_exp_down, (ff, d // 2), (0, half))])

        o_a, *expert_lo = _causal_attention(big, lambda hd: hd, None, None, big, lambda hd: nh + hd, bias,
                                            fox_v_t, nh, b, s, ATTN_TILE, 1, "fox_attention", cast_half(0))
        qm = _norm_matmul(small, 0, MLA_RANK, g_q_latent[l], wq, blocked=True, out_dtype=BF16, tm=PROJ_TM,
                          tn=PROJ_TN, rope_tabs=rope_tabs, name="mla_q_up")
        k_nope = _norm_matmul(small, 1, MLA_RANK, g_kv_latent[l], wk, blocked=True, out_dtype=BF16, tm=PROJ_TM,
                              tn=PROJ_TN, name="mla_k_up")
        mla_v_t = _norm_matmul_t(small, 1, MLA_RANK, g_kv_latent[l], w_mla_v_t, tm=PROJ_TM, tr=PROJ_TN,
                                 name="mla_v_t")
        k_rope = _krope(small, 2 * MLA_RANK // LANE, rope_tabs, 512)
        o_b, *expert_hi = _causal_attention(qm, lambda hd: 2 * hd, qm, lambda hd: 2 * hd + 1, k_nope, lambda hd: hd,
                                            k_rope, mla_v_t, MLA_HEADS, b, s, ATTN_TILE, CHUNK, "mla_attention",
                                            cast_half(1))
        if chunks is None:
            hd2 = d // 2
            expert_lo = [w_exp_gate[l, :, :hd2].astype(BF16), w_exp_up[l, :, :hd2].astype(BF16),
                         w_exp_down[l, :, :, :hd2].astype(BF16)]
            expert_hi = [w_exp_gate[l, :, hd2:].astype(BF16), w_exp_up[l, :, hd2:].astype(BF16),
                         w_exp_down[l, :, :, hd2:].astype(BF16)]
        h = _proj_residual(o_a, w_out[l].astype(BF16), h, tm=MERGE_TM, tn=d, merge=(o_b, big, 2, 3),
                           name="out_proj")
        cross_scale = ((d // MEM_HEADS) ** -0.5) * LOG2E
        mkv = _norm_matmul(mem.reshape(b * n_mem, d), 0, d, g_mem[l], w_cross_kv[l].astype(BF16), blocked=True,
                           out_dtype=BF16, tm=PROJ_TM, tn=PROJ_TN, name="cross_mem_kv")
        h = _cross_attention_layer(h, g_cross[l], (w_cross_q[l] * cross_scale).astype(BF16), mkv,
                                   w_cross_out[l].astype(BF16), b, s, n_mem, 512)
        w_r = jnp.concatenate([w_group_router[l], w_expert_router[l],
                               jnp.zeros((d, LANE - N_GROUPS - N_EXPERTS), F32)], axis=1)
        b_r = jnp.concatenate([b_group_router[l], b_expert_router[l],
                               jnp.zeros((LANE - N_GROUPS - N_EXPERTS,), F32)]).reshape(1, LANE)
        z, info, counts = _router(h, g_moe[l], w_r, b_r, 512)
        tb = 256
        eid = info[:, 0:2].astype(jnp.int32)
        rank = info[:, 4:6].astype(jnp.int32)
        counts = counts[0, N_GROUPS:N_GROUPS + N_EXPERTS].astype(jnp.int32)
        blk_e, nvalid, tok, dest = _dispatch_plan(eid, rank, counts, n, tb)
        y = _expert_blocks(blk_e, nvalid, tok, z, expert_lo, expert_hi, tb)
        tm_c = min(256, n)
        dest_blk = dest.reshape(n // tm_c, tm_c, 2).transpose(0, 2, 1).reshape(n // tm_c, 1, 2 * tm_c)
        h = _combine(dest_blk, y, h, info, g_final, tm_c, final_norm=(l + 1 == depth))
    return h.reshape(b, s, d)
```

```python
import functools
import math

import jax
import jax.numpy as jnp
from jax import lax
from jax.experimental import pallas as pl
from jax.experimental.pallas import tpu as pltpu

F32 = jnp.float32
BF16 = jnp.bfloat16

LANE = 128
BF16_SUBLANES = 16
EPS = 1e-6
CHUNK = 64
FOX_HEADS = 16
MLA_HEADS = 16
MLA_NOPE = 128
MLA_ROPE = 64
MLA_RANK = 512
ROPE_THETA = 10000.0
MEM_HEADS = 4
N_GROUPS = 8
EXPERTS_PER_GROUP = 8
N_EXPERTS = N_GROUPS * EXPERTS_PER_GROUP
LOG2E = math.log2(math.e)
VMEM_LIMIT = 56 * 1024 * 1024
NEG_INF = float("-inf")
ATTN_TILE = 1024
ATTN_HEAD_GROUP = 2
PROJ_TM = 1024
PROJ_TN = 1024
MERGE_TM = 512
GATHER_UNROLL = 8


def _params(*sem):
    return pltpu.CompilerParams(dimension_semantics=sem, vmem_limit_bytes=VMEM_LIMIT)


def _resident(whole):
    return pl.Buffered(1) if whole else None


def _rope_rotate(x, cos, sin_lo, sin_hi):
    return x * cos + pltpu.roll(x, 96, 1) * sin_lo + pltpu.roll(x, 32, 1) * sin_hi


def _norm_matmul_kernel(*refs, rope, blocked, nblk, extra):
    x_ref, g_ref, w_ref = refs[:3]
    rest = list(refs[3:])
    if rope:
        cos_ref, slo_ref, shi_ref = rest[:3]
        rest = rest[3:]
    if extra:
        we_ref, o_ref, oe_ref, xn_ref = rest
    else:
        o_ref, xn_ref = rest

    @pl.when(pl.program_id(1) == 0)
    def _():
        x = x_ref[...].astype(F32)
        y = x * lax.rsqrt(jnp.mean(x * x, axis=-1, keepdims=True) + EPS)
        xn_ref[...] = (y * g_ref[...]).astype(BF16)
        if extra:
            oe_ref[...] = jnp.dot(xn_ref[...], we_ref[...], preferred_element_type=F32)

    acc = jnp.dot(xn_ref[...], w_ref[...], preferred_element_type=F32)
    if blocked:
        for c in range(nblk):
            blk = acc[:, c * LANE:(c + 1) * LANE]
            if rope and c % 2 == 1:
                blk = _rope_rotate(blk, cos_ref[...], slo_ref[...], shi_ref[...])
            o_ref[c] = blk.astype(o_ref.dtype)
    else:
        o_ref[...] = acc.astype(o_ref.dtype)


def _norm_matmul(x, x_col, k, g, w, *, blocked, out_dtype, tm, tn, rope_tabs=None, w_extra=None, name=None):
    m = x.shape[0]
    nc = w.shape[1]
    tm = min(tm, m)
    tn = min(tn, nc)
    assert m % tm == 0 and nc % tn == 0 and tn % LANE == 0
    nblk = tn // LANE
    in_specs = [
        pl.BlockSpec((tm, k), lambda i, j: (i, x_col)),
        pl.BlockSpec((1, k), lambda i, j: (0, 0)),
        pl.BlockSpec((k, tn), lambda i, j: (0, j), pipeline_mode=_resident(nc == tn)),
    ]
    args = [x, g.reshape(1, k).astype(F32), w]
    if rope_tabs is not None:
        in_specs += [pl.BlockSpec((tm, LANE), lambda i, j: (i, 0))] * 3
        args += list(rope_tabs)
    if blocked:
        out_shape = jax.ShapeDtypeStruct((nc // LANE, m, LANE), out_dtype)
        out_spec = pl.BlockSpec((nblk, tm, LANE), lambda i, j: (j, i, 0))
    else:
        out_shape = jax.ShapeDtypeStruct((m, nc), out_dtype)
        out_spec = pl.BlockSpec((tm, tn), lambda i, j: (i, j))
    if w_extra is not None:
        ne = w_extra.shape[1]
        in_specs.append(pl.BlockSpec((k, ne), lambda i, j: (0, 0), pipeline_mode=_resident(True)))
        args.append(w_extra)
        out_spec = [out_spec, pl.BlockSpec((tm, ne), lambda i, j: (i, 0))]
        out_shape = [out_shape, jax.ShapeDtypeStruct((m, ne), F32)]
    return pl.pallas_call(
        functools.partial(_norm_matmul_kernel, rope=rope_tabs is not None, blocked=blocked, nblk=nblk,
                          extra=w_extra is not None),
        grid=(m // tm, nc // tn),
        in_specs=in_specs,
        out_specs=out_spec,
        out_shape=out_shape,
        scratch_shapes=[pltpu.VMEM((tm, k), BF16)],
        compiler_params=_params("parallel", "arbitrary"),
        name=name,
    )(*args)


def _krope_kernel(x_ref, cos_ref, slo_ref, shi_ref, o_ref):
    o_ref[...] = _rope_rotate(x_ref[...], cos_ref[...], slo_ref[...], shi_ref[...]).astype(o_ref.dtype)


def _krope(small, col_blk, rope_tabs, tm):
    m = small.shape[0]
    tm = min(tm, m)
    return pl.pallas_call(
        _krope_kernel,
        grid=(m // tm,),
        in_specs=[pl.BlockSpec((tm, LANE), lambda i: (i, col_blk))] + [pl.BlockSpec((tm, LANE), lambda i: (i, 0))] * 3,
        out_specs=pl.BlockSpec((tm, LANE), lambda i: (i, 0)),
        out_shape=jax.ShapeDtypeStruct((m, LANE), BF16),
        compiler_params=_params("parallel"),
        name="k_rope",
    )(small, *rope_tabs)


N_SPLIT = 3


def _forget_bias_kernel(f_ref, b_ref, o_ref, carry_ref, *, ts, lane0, heads):
    @pl.when(pl.program_id(1) == 0)
    def _():
        carry_ref[...] = jnp.zeros_like(carry_ref)

    x = f_ref[...] + b_ref[...]
    log_f = jnp.minimum(x, 0.0) - jnp.log1p(jnp.exp(-jnp.abs(x)))
    row = lax.broadcasted_iota(jnp.int32, (ts, ts), 0)
    col = lax.broadcasted_iota(jnp.int32, (ts, ts), 1)
    lower = (col <= row).astype(F32)
    c = jnp.dot(lower, log_f, preferred_element_type=F32, precision=lax.Precision.HIGHEST) + carry_ref[0:1, :]
    carry_ref[...] = jnp.broadcast_to(c[ts - 1:ts, :], carry_ref.shape)
    rest = -(c * LOG2E)
    src = lax.broadcasted_iota(jnp.int32, (LANE, LANE), 0) - lane0
    dst = lax.broadcasted_iota(jnp.int32, (LANE, LANE), 1)
    out = jnp.zeros((ts, LANE), F32)
    for pi in range(N_SPLIT):
        piece = rest.astype(BF16)
        rest = rest - piece.astype(F32)
        place = jnp.where((src >= 0) & (src < heads) & (dst == N_SPLIT * src + pi), 1.0, 0.0).astype(BF16)
        out = out + jnp.dot(piece, place, preferred_element_type=F32)
    o_ref[...] = out.astype(o_ref.dtype)


def _forget_bias(small, col_blk, lane0, b_forget, b, s, ts):
    heads = b_forget.shape[0]
    assert N_SPLIT * heads <= LANE
    ts = min(ts, s)
    per_b = s // ts
    b_row = jnp.zeros((1, LANE), F32).at[0, lane0:lane0 + heads].set(b_forget.astype(F32))
    return pl.pallas_call(
        functools.partial(_forget_bias_kernel, ts=ts, lane0=lane0, heads=heads),
        grid=(b, per_b),
        in_specs=[pl.BlockSpec((ts, LANE), lambda bi, j: (bi * per_b + j, col_blk)),
                  pl.BlockSpec((1, LANE), lambda bi, j: (0, 0))],
        out_specs=pl.BlockSpec((ts, LANE), lambda bi, j: (bi * per_b + j, 0)),
        out_shape=jax.ShapeDtypeStruct((b * s, LANE), BF16),
        scratch_shapes=[pltpu.VMEM((8, LANE), F32)],
        compiler_params=_params("parallel", "arbitrary"),
        name="forget_bias",
    )(small, b_row)


def _norm_matmul_t_kernel(x_ref, g_ref, w_ref, o_ref, xn_ref):
    @pl.when(pl.program_id(1) == 0)
    def _():
        x = x_ref[...].astype(F32)
        y = x * lax.rsqrt(jnp.mean(x * x, axis=-1, keepdims=True) + EPS)
        xn_ref[...] = (y * g_ref[...]).astype(BF16)

    o_ref[...] = lax.dot_general(w_ref[...], xn_ref[...], (((1,), (1,)), ((), ())),
                                 preferred_element_type=F32).astype(o_ref.dtype)


def _norm_matmul_t(x, x_col, k, g, w_t, *, tm, tr, name=None):
    m = x.shape[0]
    r = w_t.shape[0]
    tm = min(tm, m)
    tr = min(tr, r)
    return pl.pallas_call(
        _norm_matmul_t_kernel,
        grid=(m // tm, r // tr),
        in_specs=[pl.BlockSpec((tm, k), lambda i, j: (i, x_col)), pl.BlockSpec((1, k), lambda i, j: (0, 0)),
                  pl.BlockSpec((tr, k), lambda i, j: (j, 0))],
        out_specs=pl.BlockSpec((tr, tm), lambda i, j: (j, i)),
        out_shape=jax.ShapeDtypeStruct((r, m), BF16),
        scratch_shapes=[pltpu.VMEM((tm, k), BF16)],
        compiler_params=_params("parallel", "arbitrary"),
        name=name,
    )(x, g.reshape(1, k).astype(F32), w_t)


N_ATTN_SCRATCH = 7


def _attn_kernel(*refs, t, unit, q_ones, group, n_cast):
    n_in = 3 if q_ones else 4
    kb_ref = refs[group * n_in]
    cast_src = refs[group * n_in + 1:group * n_in + 1 + n_cast]
    n_inputs = group * n_in + 1 + n_cast
    o_ref = refs[n_inputs]
    cast_dst = refs[n_inputs + 1:n_inputs + 1 + n_cast]
    scratch = refs[n_inputs + 1 + n_cast:]
    i = pl.program_id(2)
    streams = []
    for g in range(group):
        head_refs = refs[g * n_in:(g + 1) * n_in]
        if q_ones:
            qa_ref, ka_ref, vt_ref = head_refs
            lane = lax.broadcasted_iota(jnp.int32, (t, LANE), 1)
            lo = N_SPLIT * (pl.program_id(1) * group + g)
            q_b = jnp.where((lane >= lo) & (lane < lo + N_SPLIT), 1.0, 0.0).astype(BF16)
        else:
            qa_ref, qb_ref, ka_ref, vt_ref = head_refs
            q_b = qb_ref[...]
        q = jnp.concatenate([qa_ref[...], q_b], axis=1)
        streams.append((q, ka_ref, vt_ref) + tuple(scratch[g * N_ATTN_SCRATCH:(g + 1) * N_ATTN_SCRATCH]))

    def stage(name, j=None):
        for st in streams:
            _attn_stream(st, kb_ref, j, t, unit, name)

    def body(tau, carry):
        stage("values", tau - 2)
        stage("softmax")
        stage("logits", tau)
        return carry

    stage("init")
    stage("logits", 0)
    for src_ref, dst_ref in zip(cast_src, cast_dst):
        dst_ref[...] = src_ref[...].astype(dst_ref.dtype)
    lax.fori_loop(1, i + 1, body, 0)
    stage("values", i - 1)
    stage("softmax_masked")
    stage("values", i)
    for g, st in enumerate(streams):
        o_ref[g] = jnp.transpose(st[5][...] / st[4][...]).astype(o_ref.dtype)


def _attn_stream(st, kb_ref, j, t, unit, stage):
    q, ka_ref, vt_ref, m_scr, l_scr, acc_scr, s_ref, p_ref, a_ref, bmax_ref = st
    if stage == "init":
        m_scr[...] = jnp.full(m_scr.shape, NEG_INF, F32)
        l_scr[...] = jnp.zeros(l_scr.shape, F32)
        acc_scr[...] = jnp.zeros(acc_scr.shape, F32)
        p_ref[...] = jnp.zeros(p_ref.shape, BF16)
        a_ref[...] = jnp.ones(a_ref.shape, F32)
    elif stage == "logits":
        off = pl.multiple_of(j * t, t)
        k = jnp.concatenate([ka_ref[pl.ds(off, t), :], kb_ref[pl.ds(off, t), :]], axis=1)
        s = lax.dot_general(k, q, (((1,), (1,)), ((), ())), preferred_element_type=F32)
        s_ref[...] = s
        bmax_ref[...] = jnp.max(s, axis=0, keepdims=True)
    elif stage == "values":
        off = pl.multiple_of(jnp.maximum(j, 0) * t, t)
        pv = jnp.dot(vt_ref[:, pl.ds(off, t)], p_ref[...], preferred_element_type=F32)
        acc_scr[...] = a_ref[...] * acc_scr[...] + pv
    else:
        masked = stage == "softmax_masked"
        s = s_ref[...]
        if masked:
            key = lax.broadcasted_iota(jnp.int32, (t, t), 0)
            qry = lax.broadcasted_iota(jnp.int32, (t, t), 1)
            s = jnp.where((key // unit) <= (qry // unit), s, NEG_INF)
            block_max = jnp.max(s, axis=0, keepdims=True)
        else:
            block_max = bmax_ref[...]
        m_prev = m_scr[...]
        m_new = jnp.maximum(m_prev, block_max)
        alpha = jnp.exp2(m_prev - m_new)
        p = jnp.exp2(s - m_new)
        l_scr[...] = alpha * l_scr[...] + jnp.sum(p, axis=0, keepdims=True)
        m_scr[...] = m_new
        p_ref[...] = p.astype(BF16)
        a_ref[...] = alpha


def _attn_steps(heads, b, s, t):
    return b * (heads // ATTN_HEAD_GROUP) * (s // min(t, s))


def _cast_chunks(steps, n_slices, row_counts):
    chunks, rem = divmod(steps, n_slices)
    ok = rem == 0 and chunks > 0 and all(r % (chunks * BF16_SUBLANES) == 0 for r in row_counts)
    return chunks if ok else None


def _causal_attention(q_a, qa_blk, q_b, qb_blk, k_a, ka_blk, k_b, v_t, heads, b, s, t, unit, name, side_cast=None):
    t = min(t, s)
    group = ATTN_HEAD_GROUP
    assert t % unit == 0 and heads % group == 0
    nq = s // t
    once = pl.Buffered(1)
    in_specs, args = [], []
    for g in range(group):
        def q_spec(blk, g=g):
            return pl.BlockSpec((None, t, LANE), lambda bi, hi, i: (blk(hi * group + g), bi * nq + i, 0))

        in_specs.append(q_spec(qa_blk))
        args.append(q_a)
        if q_b is not None:
            in_specs.append(q_spec(qb_blk))
            args.append(q_b)
        in_specs.append(pl.BlockSpec((None, s, LANE), lambda bi, hi, i, g=g: (ka_blk(hi * group + g), bi, 0),
                                     pipeline_mode=once))
        in_specs.append(pl.BlockSpec((LANE, s), lambda bi, hi, i, g=g: (hi * group + g, bi), pipeline_mode=once))
        args += [k_a, v_t]
    in_specs.append(pl.BlockSpec((s, LANE), lambda bi, hi, i: (bi, 0), pipeline_mode=once))
    args.append(k_b)
    stat = pltpu.VMEM((1, t), F32)
    per_head_scratch = [stat, stat, pltpu.VMEM((LANE, t), F32), pltpu.VMEM((t, t), F32),
                        pltpu.VMEM((t, t), BF16), stat, stat]
    assert len(per_head_scratch) == N_ATTN_SCRATCH
    out_specs = [pl.BlockSpec((group, t, LANE), lambda bi, hi, i: (hi, bi * nq + i, 0))]
    out_shape = [jax.ShapeDtypeStruct((heads, b * s, LANE), BF16)]
    n_cast = 0
    if side_cast is not None:
        layer, chunks, items = side_cast
        n_cast = len(items)

        def step_of(bi, hi, i):
            return (bi * (heads // group) + hi) * nq + i

        for src, (rows, cols), (row_half, col_half) in items:
            rb = rows // chunks
            in_specs.append(pl.BlockSpec(
                (None, None, rb, cols),
                lambda bi, hi, i, rh=row_half, ch=col_half: (layer, step_of(bi, hi, i) // chunks,
                                                             rh * chunks + step_of(bi, hi, i) % chunks, ch)))
            args.append(src)
            out_specs.append(pl.BlockSpec(
                (None, rb, cols), lambda bi, hi, i: (step_of(bi, hi, i) // chunks, step_of(bi, hi, i) % chunks, 0)))
            out_shape.append(jax.ShapeDtypeStruct((src.shape[1], rows, cols), BF16))
    return pl.pallas_call(
        functools.partial(_attn_kernel, t=t, unit=unit, q_ones=q_b is None, group=group, n_cast=n_cast),
        grid=(b, heads // group, nq),
        in_specs=in_specs,
        out_specs=out_specs,
        out_shape=out_shape,
        scratch_shapes=per_head_scratch * group,
        compiler_params=_params("parallel", "parallel", "arbitrary"),
        name=name,
    )(*args)


def _proj_residual_kernel(*refs, merge, nb):
    if merge:
        oa_ref, ob_ref, ga_ref, gb_ref, w_ref, res_ref, o_ref, a_scr = refs
    else:
        oa_ref, w_ref, res_ref, o_ref, a_scr = refs

    @pl.when(pl.program_id(1) == 0)
    def _():
        for c in range(nb):
            if merge:
                a = (jax.nn.sigmoid(ga_ref[c].astype(F32)) * oa_ref[c].astype(F32)
                     + jax.nn.sigmoid(gb_ref[c].astype(F32)) * ob_ref[c].astype(F32))
            else:
                a = oa_ref[c]
            a_scr[:, c * LANE:(c + 1) * LANE] = a.astype(BF16)

    o_ref[...] = res_ref[...] + jnp.dot(a_scr[...], w_ref[...], preferred_element_type=F32)


def _proj_residual(o_a, w, res, *, tm, tn, merge=None, name=None):
    nb, m, _ = o_a.shape
    k = nb * LANE
    nc = w.shape[1]
    tm = min(tm, m)
    tn = min(tn, nc)
    head_spec = pl.BlockSpec((nb, tm, LANE), lambda i, j: (0, i, 0))
    in_specs = [head_spec]
    args = [o_a]
    if merge is not None:
        o_b, gates, ga_grp, gb_grp = merge
        in_specs += [head_spec,
                     pl.BlockSpec((nb, tm, LANE), lambda i, j: (ga_grp, i, 0)),
                     pl.BlockSpec((nb, tm, LANE), lambda i, j: (gb_grp, i, 0))]
        args += [o_b, gates, gates]
    in_specs += [pl.BlockSpec((k, tn), lambda i, j: (0, j), pipeline_mode=_resident(nc == tn)),
                 pl.BlockSpec((tm, tn), lambda i, j: (i, j))]
    args += [w, res]
    return pl.pallas_call(
        functools.partial(_proj_residual_kernel, merge=merge is not None, nb=nb),
        grid=(m // tm, nc // tn),
        in_specs=in_specs,
        out_specs=pl.BlockSpec((tm, tn), lambda i, j: (i, j)),
        out_shape=jax.ShapeDtypeStruct((m, nc), F32),
        scratch_shapes=[pltpu.VMEM((tm, k), BF16)],
        compiler_params=_params("parallel", "arbitrary"),
        name=name,
    )(*args)


def _cross_kernel(h_ref, g_ref, wq_ref, mk_ref, mv_ref, wo_ref, o_ref, *, bph):
    x = h_ref[...]
    xn = (x * lax.rsqrt(jnp.mean(x * x, axis=-1, keepdims=True) + EPS) * g_ref[...]).astype(BF16)
    q_all = jnp.dot(xn, wq_ref[...], preferred_element_type=F32).astype(BF16)
    hd = bph * LANE
    heads_out = []
    for hh in range(MEM_HEADS):
        blocks = range(hh * bph, (hh + 1) * bph)
        q = q_all[:, hh * hd:(hh + 1) * hd]
        k = jnp.concatenate([mk_ref[c] for c in blocks], axis=1)
        v = jnp.concatenate([mv_ref[c] for c in blocks], axis=1)
        s = lax.dot_general(q, k, (((1,), (1,)), ((), ())), preferred_element_type=F32)
        p = jnp.exp2(s - jnp.max(s, axis=1, keepdims=True))
        p = p / jnp.sum(p, axis=1, keepdims=True)
        heads_out.append(jnp.dot(p.astype(BF16), v, preferred_element_type=F32).astype(BF16))
    o = jnp.concatenate(heads_out, axis=1)
    o_ref[...] = x + jnp.dot(o, wo_ref[...], preferred_element_type=F32)


def _cross_attention_layer(h, g, wq, mkv, wo, b, s, n_mem, tm):
    n, d = h.shape
    nb = d // LANE
    tm = min(tm, s)
    per_b = s // tm
    whole = pl.Buffered(1)
    return pl.pallas_call(
        functools.partial(_cross_kernel, bph=nb // MEM_HEADS),
        grid=(n // tm,),
        in_specs=[
            pl.BlockSpec((tm, d), lambda i: (i, 0)),
            pl.BlockSpec((1, d), lambda i: (0, 0)),
            pl.BlockSpec((d, d), lambda i: (0, 0), pipeline_mode=whole),
            pl.BlockSpec((nb, n_mem, LANE), lambda i: (0, i // per_b, 0)),
            pl.BlockSpec((nb, n_mem, LANE), lambda i: (1, i // per_b, 0)),
            pl.BlockSpec((d, d), lambda i: (0, 0), pipeline_mode=whole),
        ],
        out_specs=pl.BlockSpec((tm, d), lambda i: (i, 0)),
        out_shape=jax.ShapeDtypeStruct((n, d), F32),
        compiler_params=_params("parallel"),
        name="cross_attention_layer",
    )(h, g.reshape(1, d).astype(F32), wq, mkv, mkv, wo)


def _router_kernel(h_ref, g_ref, w_ref, b_ref, z_ref, info_ref, count_ref):
    x = h_ref[...]
    z = x * lax.rsqrt(jnp.mean(x * x, axis=-1, keepdims=True) + EPS) * g_ref[...]
    z_ref[...] = z
    logits = jnp.dot(z, w_ref[...], preferred_element_type=F32, precision=lax.Precision.HIGHEST) + b_ref[...]
    lane = lax.broadcasted_iota(jnp.int32, logits.shape, 1).astype(F32)
    big = float(LANE)

    def first_where(mask):
        return jnp.min(jnp.where(mask, lane, big), axis=1, keepdims=True)

    gmask = lane < N_GROUPS
    gl = jnp.where(gmask, logits, NEG_INF)
    gmax = jnp.max(gl, axis=1, keepdims=True)
    gsum = jnp.sum(jnp.exp(gl - gmax), axis=1, keepdims=True)
    g_idx = first_where(gl == gmax)
    g_w = 1.0 / gsum
    lo = N_GROUPS + g_idx * EXPERTS_PER_GROUP
    emask = (lane >= lo) & (lane < lo + EXPERTS_PER_GROUP)
    el = jnp.where(emask, logits, NEG_INF)
    emax = jnp.max(el, axis=1, keepdims=True)
    eexp = jnp.exp(el - emax)
    prob = jnp.where(emask, eexp / jnp.sum(eexp, axis=1, keepdims=True), -1.0)
    p1 = jnp.max(prob, axis=1, keepdims=True)
    i1 = first_where(prob == p1)
    prob2 = jnp.where(lane == i1, -1.0, prob)
    p2 = jnp.max(prob2, axis=1, keepdims=True)
    i2 = first_where(prob2 == p2)
    denom = p1 + p2
    w1 = g_w * p1 / denom
    w2 = g_w * p2 / denom
    @pl.when(pl.program_id(0) == 0)
    def _():
        count_ref[...] = jnp.zeros_like(count_ref)

    tm = logits.shape[0]
    hot1 = jnp.where(lane == i1, 1.0, 0.0)
    hot2 = jnp.where(lane == i2, 1.0, 0.0)
    hot = hot1 + hot2
    row = lax.broadcasted_iota(jnp.int32, (tm, tm), 0)
    col = lax.broadcasted_iota(jnp.int32, (tm, tm), 1)
    before = jnp.where(col < row, 1.0, 0.0).astype(BF16)
    prior = jnp.dot(before, hot.astype(BF16), preferred_element_type=F32) + count_ref[0:1, :]
    r1 = jnp.sum(prior * hot1, axis=1, keepdims=True)
    r2 = jnp.sum(prior * hot2, axis=1, keepdims=True)
    count_ref[...] = count_ref[...] + jnp.sum(hot, axis=0, keepdims=True)
    info = jnp.zeros_like(logits)
    for k, val in enumerate((i1 - N_GROUPS, i2 - N_GROUPS, w1, w2, r1, r2)):
        info = jnp.where(lane == k, val, info)
    info_ref[...] = info


def _router(h, g, w_r, b_r, tm):
    n, d = h.shape
    tm = min(tm, n)
    return pl.pallas_call(
        _router_kernel,
        grid=(n // tm,),
        in_specs=[pl.BlockSpec((tm, d), lambda i: (i, 0)), pl.BlockSpec((1, d), lambda i: (0, 0)),
                  pl.BlockSpec((d, LANE), lambda i: (0, 0)), pl.BlockSpec((1, LANE), lambda i: (0, 0))],
        out_specs=[pl.BlockSpec((tm, d), lambda i: (i, 0)), pl.BlockSpec((tm, LANE), lambda i: (i, 0)),
                   pl.BlockSpec((8, LANE), lambda i: (0, 0))],
        out_shape=[jax.ShapeDtypeStruct((n, d), F32), jax.ShapeDtypeStruct((n, LANE), F32),
                   jax.ShapeDtypeStruct((8, LANE), F32)],
        compiler_params=_params("arbitrary"),
        name="moe_router",
    )(h, g.reshape(1, d).astype(F32), w_r, b_r)


def _row_copy(src_hbm, row, dst, dst_row, sem):
    return pltpu.make_async_copy(src_hbm.at[pl.ds(row, 1)], dst.at[pl.ds(dst_row, 1)], sem)


def _issue_rows(idx_ref, idx_off, src_hbm, dst, sem, n_rows):
    def issue(r, carry):
        _row_copy(src_hbm, idx_ref[0, idx_off + r], dst, r, sem).start()
        return carry

    lax.fori_loop(0, n_rows, issue, 0, unroll=GATHER_UNROLL)


def _wait_rows(src_hbm, dst, sem, n_rows):
    def wait(r, carry):
        _row_copy(src_hbm, 0, dst, r, sem).wait()
        return carry

    lax.fori_loop(0, n_rows, wait, 0, unroll=GATHER_UNROLL)


def _expert_kernel(blk_e_ref, nvalid_ref, tok0_ref, tok_next_ref, z_hbm, wg_lo, wu_lo, wd_lo, wg_hi, wu_hi, wd_hi,
                   y_ref, xbuf, sem, *, tb):
    i = pl.program_id(0)
    nvalid = nvalid_ref[0]
    slot = i % 2

    @pl.when((i == 0) & (nvalid > 0))
    def _():
        _issue_rows(tok0_ref, 0, z_hbm, xbuf.at[0], sem.at[0], tb)

    @pl.when(i + 1 < nvalid)
    def _():
        _issue_rows(tok_next_ref, 0, z_hbm, xbuf.at[1 - slot], sem.at[1 - slot], tb)

    @pl.when(i < nvalid)
    def _():
        _wait_rows(z_hbm, xbuf.at[slot], sem.at[slot], tb)
        x = xbuf[slot].astype(BF16)
        half = x.shape[1] // 2
        x_lo, x_hi = x[:, :half], x[:, half:]
        gate = (jnp.dot(x_lo, wg_lo[...], preferred_element_type=F32)
                + jnp.dot(x_hi, wg_hi[...], preferred_element_type=F32))
        up = (jnp.dot(x_lo, wu_lo[...], preferred_element_type=F32)
              + jnp.dot(x_hi, wu_hi[...], preferred_element_type=F32))
        hidden = (gate * jax.nn.sigmoid(gate) * up).astype(BF16)
        y_ref[:, :half] = jnp.dot(hidden, wd_lo[...], preferred_element_type=F32)
        y_ref[:, half:] = jnp.dot(hidden, wd_hi[...], preferred_element_type=F32)

    @pl.when(i >= nvalid)
    def _():
        y_ref[...] = jnp.zeros_like(y_ref)


def _expert_blocks(blk_e, nvalid, tok, z, w_lo, w_hi, tb):
    nblk = tok.shape[0]
    d = z.shape[1]
    ff = w_lo[0].shape[2]
    rows_half = pl.BlockSpec((None, d // 2, ff), lambda i, be, nv: (be[i], 0, 0))
    cols_half = pl.BlockSpec((None, ff, d // 2), lambda i, be, nv: (be[i], 0, 0))
    grid_spec = pltpu.PrefetchScalarGridSpec(
        num_scalar_prefetch=2,
        grid=(nblk,),
        in_specs=[
            pl.BlockSpec((None, 1, tb), lambda i, be, nv: (0, 0, 0), memory_space=pltpu.SMEM),
            pl.BlockSpec((None, 1, tb), lambda i, be, nv: (jnp.minimum(i + 1, nblk - 1), 0, 0),
                         memory_space=pltpu.SMEM),
            pl.BlockSpec(memory_space=pl.ANY),
            rows_half, rows_half, cols_half, rows_half, rows_half, cols_half,
        ],
        out_specs=pl.BlockSpec((tb, d), lambda i, be, nv: (i, 0)),
        scratch_shapes=[pltpu.VMEM((2, tb, d), F32), pltpu.SemaphoreType.DMA((2,))],
    )
    return pl.pallas_call(
        functools.partial(_expert_kernel, tb=tb),
        grid_spec=grid_spec,
        out_shape=jax.ShapeDtypeStruct((nblk * tb, d), F32),
        compiler_params=_params("arbitrary"),
        name="moe_experts",
    )(blk_e, nvalid, tok, tok, z, *w_lo, *w_hi)


def _combine_kernel(dest0_ref, dest_next_ref, y_hbm, h_ref, info_ref, g_ref, o_ref, ybuf, sem, *, tm, final_norm):
    i = pl.program_id(0)
    slot = i % 2

    def issue(idx_ref, sl):
        for k in range(2):
            _issue_rows(idx_ref, k * tm, y_hbm, ybuf.at[sl, k], sem.at[sl], tm)

    @pl.when(i == 0)
    def _():
        issue(dest0_ref, 0)

    @pl.when(i + 1 < pl.num_programs(0))
    def _():
        issue(dest_next_ref, 1 - slot)

    for k in range(2):
        _wait_rows(y_hbm, ybuf.at[slot, k], sem.at[slot], tm)
    info = info_ref[...]
    moe = ybuf[slot, 0] * info[:, 2:3] + ybuf[slot, 1] * info[:, 3:4]
    hx = h_ref[...] + moe
    if final_norm:
        hx = hx * lax.rsqrt(jnp.mean(hx * hx, axis=-1, keepdims=True) + EPS) * g_ref[...]
    o_ref[...] = hx


def _combine(dest, y, h, info, g_final, tm, final_norm):
    n, d = h.shape
    n_tiles = n // tm
    return pl.pallas_call(
        functools.partial(_combine_kernel, tm=tm, final_norm=final_norm),
        grid=(n_tiles,),
        in_specs=[
            pl.BlockSpec((None, 1, 2 * tm), lambda i: (0, 0, 0), memory_space=pltpu.SMEM),
            pl.BlockSpec((None, 1, 2 * tm), lambda i: (jnp.minimum(i + 1, n_tiles - 1), 0, 0),
                         memory_space=pltpu.SMEM),
            pl.BlockSpec(memory_space=pl.ANY),
            pl.BlockSpec((tm, d), lambda i: (i, 0)),
            pl.BlockSpec((tm, LANE), lambda i: (i, 0)),
            pl.BlockSpec((1, d), lambda i: (0, 0)),
        ],
        out_specs=pl.BlockSpec((tm, d), lambda i: (i, 0)),
        out_shape=jax.ShapeDtypeStruct((n, d), F32),
        scratch_shapes=[pltpu.VMEM((2, 2, tm, d), F32), pltpu.SemaphoreType.DMA((2,))],
        compiler_params=_params("arbitrary"),
        name="moe_combine_final_norm",
    )(dest, dest, y, h, info, g_final.reshape(1, d).astype(F32))


def _dispatch_plan(eid, rank, counts, n, tb):
    n_slots = 2 * n
    nblk = n_slots // tb + N_EXPERTS
    slot_e = eid.reshape(-1)
    rank = rank.reshape(-1)
    blocks_per_e = (counts + tb - 1) // tb
    blk_end = jnp.cumsum(blocks_per_e)
    blk_start = blk_end - blocks_per_e
    dest = blk_start[slot_e] * tb + rank
    slot_tok = jnp.arange(n_slots, dtype=jnp.int32) // 2
    tok = jnp.zeros((nblk * tb,), jnp.int32).at[dest].set(slot_tok, unique_indices=True, mode="promise_in_bounds")
    blk_ids = jnp.arange(nblk, dtype=jnp.int32)
    blk_e = jnp.minimum(jnp.sum(blk_end[None, :] <= blk_ids[:, None], axis=1), N_EXPERTS - 1).astype(jnp.int32)
    nvalid = blk_end[-1:].astype(jnp.int32)
    return blk_e, nvalid, tok.reshape(nblk, 1, tb), dest.reshape(n, 2)


def _rope_tables(positions):
    half = MLA_ROPE // 2
    inv_freq = ROPE_THETA ** (-jnp.arange(half, dtype=F32) / half)
    ang = positions.astype(F32).reshape(-1, 1) * inv_freq[None, :]
    cos, sin = jnp.cos(ang), jnp.sin(ang)
    zero = jnp.zeros_like(cos)
    cos_t = jnp.concatenate([cos, cos, zero, zero], axis=1)
    sin_lo = jnp.concatenate([-sin, zero, zero, zero], axis=1)
    sin_hi = jnp.concatenate([zero, sin, zero, zero], axis=1)
    return cos_t, sin_lo, sin_hi


def kernel(x, mem, positions, g_mix, w_in, b_forget, g_q_latent, g_kv_latent, w_q_up, w_k_up, w_v_up, w_out, g_cross, g_mem, w_cross_q, w_cross_kv, w_cross_out, g_moe, w_group_router, b_group_router, w_expert_router, b_expert_router, w_exp_gate, w_exp_up, w_exp_down, g_final):
    b, s, d = x.shape
    n = b * s
    n_mem = mem.shape[1]
    depth = w_in.shape[0]
    fw = FOX_HEADS * LANE
    h = x.reshape(n, d)
    rope_tabs = _rope_tables(positions)

    for l in range(depth):
        wi = w_in[l]
        fox_scale = (LANE ** -0.5) * LOG2E
        o_f = 3 * fw
        o_cq = o_f + FOX_HEADS
        o_kr = o_cq + 2 * MLA_RANK
        o_g = o_kr + MLA_ROPE
        w_big = jnp.concatenate([wi[:, :fw] * fox_scale, wi[:, fw:2 * fw], wi[:, o_g:]], axis=1).astype(BF16)
        w_fox_v_t = wi[:, 2 * fw:3 * fw].T.astype(BF16)
        w_small = jnp.concatenate(
            [wi[:, o_cq:o_kr], wi[:, o_kr:o_g], wi[:, o_f:o_cq],
             jnp.zeros((d, LANE - MLA_ROPE - FOX_HEADS), F32)], axis=1).astype(BF16)
        mla_scale = ((MLA_NOPE + MLA_ROPE) ** -0.5) * LOG2E
        wq = (w_q_up[l] * mla_scale).reshape(MLA_RANK, MLA_HEADS, MLA_NOPE + MLA_ROPE)
        wq = jnp.pad(wq, ((0, 0), (0, 0), (0, 2 * LANE - MLA_NOPE - MLA_ROPE))).reshape(MLA_RANK, -1).astype(BF16)
        wk = w_k_up[l].astype(BF16)
        w_mla_v_t = w_v_up[l].T.astype(BF16)

        big, small = _norm_matmul(h, 0, d, g_mix[l], w_big, blocked=True, out_dtype=BF16, tm=PROJ_TM, tn=PROJ_TN,
                                  w_extra=w_small, name="in_proj")
        nh = FOX_HEADS
        fox_v_t = _norm_matmul_t(h, 0, d, g_mix[l], w_fox_v_t, tm=PROJ_TM, tr=PROJ_TN, name="fox_v_t")
        bias = _forget_bias(small, 2 * MLA_RANK // LANE, MLA_ROPE, b_forget[l], b, s, 512)
        ff = w_exp_gate.shape[3]
        steps = _attn_steps(nh, b, s, ATTN_TILE)
        chunks = _cast_chunks(steps, N_EXPERTS, (d // 2, ff))
        if steps != _attn_steps(MLA_HEADS, b, s, ATTN_TILE):
            chunks = None

        def cast_half(half):
            if chunks is None:
                return None
            return (l, chunks, [(w_exp_gate, (d // 2, ff), (half, 0)), (w_exp_up, (d // 2, ff), (half, 0)),
                                (w_exp_down, (ff, d // 2), (0, half))])

        o_a, *expert_lo = _causal_attention(big, lambda hd: hd, None, None, big, lambda hd: nh + hd, bias,
                                            fox_v_t, nh, b, s, ATTN_TILE, 1, "fox_attention", cast_half(0))
        qm = _norm_matmul(small, 0, MLA_RANK, g_q_latent[l], wq, blocked=True, out_dtype=BF16, tm=PROJ_TM,
                          tn=PROJ_TN, rope_tabs=rope_tabs, name="mla_q_up")
        k_nope = _norm_matmul(small, 1, MLA_RANK, g_kv_latent[l], wk, blocked=True, out_dtype=BF16, tm=PROJ_TM,
                              tn=PROJ_TN, name="mla_k_up")
        mla_v_t = _norm_matmul_t(small, 1, MLA_RANK, g_kv_latent[l], w_mla_v_t, tm=PROJ_TM, tr=PROJ_TN,
                                 name="mla_v_t")
        k_rope = _krope(small, 2 * MLA_RANK // LANE, rope_tabs, 512)
        o_b, *expert_hi = _causal_attention(qm, lambda hd: 2 * hd, qm, lambda hd: 2 * hd + 1, k_nope, lambda hd: hd,
                                            k_rope, mla_v_t, MLA_HEADS, b, s, ATTN_TILE, CHUNK, "mla_attention",
                                            cast_half(1))
        if chunks is None:
            hd2 = d // 2
            expert_lo = [w_exp_gate[l, :, :hd2].astype(BF16), w_exp_up[l, :, :hd2].astype(BF16),
                         w_exp_down[l, :, :, :hd2].astype(BF16)]
            expert_hi = [w_exp_gate[l, :, hd2:].astype(BF16), w_exp_up[l, :, hd2:].astype(BF16),
                         w_exp_down[l, :, :, hd2:].astype(BF16)]
        h = _proj_residual(o_a, w_out[l].astype(BF16), h, tm=MERGE_TM, tn=d, merge=(o_b, big, 2, 3),
                           name="out_proj")
        cross_scale = ((d // MEM_HEADS) ** -0.5) * LOG2E
        mkv = _norm_matmul(mem.reshape(b * n_mem, d), 0, d, g_mem[l], w_cross_kv[l].astype(BF16), blocked=True,
                           out_dtype=BF16, tm=PROJ_TM, tn=PROJ_TN, name="cross_mem_kv")
        h = _cross_attention_layer(h, g_cross[l], (w_cross_q[l] * cross_scale).astype(BF16), mkv,
                                   w_cross_out[l].astype(BF16), b, s, n_mem, 512)
        w_r = jnp.concatenate([w_group_router[l], w_expert_router[l],
                               jnp.zeros((d, LANE - N_GROUPS - N_EXPERTS), F32)], axis=1)
        b_r = jnp.concatenate([b_group_router[l], b_expert_router[l],
                               jnp.zeros((LANE - N_GROUPS - N_EXPERTS,), F32)]).reshape(1, LANE)
        z, info, counts = _router(h, g_moe[l], w_r, b_r, 512)
        tb = 256
        eid = info[:, 0:2].astype(jnp.int32)
        rank = info[:, 4:6].astype(jnp.int32)
        counts = counts[0, N_GROUPS:N_GROUPS + N_EXPERTS].astype(jnp.int32)
        blk_e, nvalid, tok, dest = _dispatch_plan(eid, rank, counts, n, tb)
        y = _expert_blocks(blk_e, nvalid, tok, z, expert_lo, expert_hi, tb)
        tm_c = min(256, n)
        dest_blk = dest.reshape(n // tm_c, tm_c, 2).transpose(0, 2, 1).reshape(n // tm_c, 1, 2 * tm_c)
        h = _combine(dest_blk, y, h, info, g_final, tm_c, final_norm=(l + 1 == depth))
    return h.reshape(b, s, d)
```

```python
import functools
import math

import jax
import jax.numpy as jnp
from jax import lax
from jax.experimental import pallas as pl
from jax.experimental.pallas import tpu as pltpu

F32 = jnp.float32
BF16 = jnp.bfloat16

LANE = 128
BF16_SUBLANES = 16
EPS = 1e-6
CHUNK = 64
FOX_HEADS = 16
MLA_HEADS = 16
MLA_NOPE = 128
MLA_ROPE = 64
MLA_RANK = 512
ROPE_THETA = 10000.0
MEM_HEADS = 4
N_GROUPS = 8
EXPERTS_PER_GROUP = 8
N_EXPERTS = N_GROUPS * EXPERTS_PER_GROUP
LOG2E = math.log2(math.e)
VMEM_LIMIT = 56 * 1024 * 1024
NEG_INF = float("-inf")
ATTN_TILE = 1024
ATTN_HEAD_GROUP = 2
PROJ_TM = 1024
PROJ_TN = 1024
MERGE_TM = 512
GATHER_UNROLL = 8


def _params(*sem):
    return pltpu.CompilerParams(dimension_semantics=sem, vmem_limit_bytes=VMEM_LIMIT)


def _resident(whole):
    return pl.Buffered(1) if whole else None


def _rope_rotate(x, cos, sin_lo, sin_hi):
    return x * cos + pltpu.roll(x, 96, 1) * sin_lo + pltpu.roll(x, 32, 1) * sin_hi


def _norm_matmul_kernel(*refs, rope, blocked, nblk, extra):
    x_ref, g_ref, w_ref = refs[:3]
    rest = list(refs[3:])
    if rope:
        cos_ref, slo_ref, shi_ref = rest[:3]
        rest = rest[3:]
    if extra:
        we_ref, o_ref, oe_ref, xn_ref = rest
    else:
        o_ref, xn_ref = rest

    @pl.when(pl.program_id(1) == 0)
    def _():
        x = x_ref[...].astype(F32)
        y = x * lax.rsqrt(jnp.mean(x * x, axis=-1, keepdims=True) + EPS)
        xn_ref[...] = (y * g_ref[...]).astype(BF16)
        if extra:
            oe_ref[...] = jnp.dot(xn_ref[...], we_ref[...], preferred_element_type=F32)

    acc = jnp.dot(xn_ref[...], w_ref[...], preferred_element_type=F32)
    if blocked:
        for c in range(nblk):
            blk = acc[:, c * LANE:(c + 1) * LANE]
            if rope and c % 2 == 1:
                blk = _rope_rotate(blk, cos_ref[...], slo_ref[...], shi_ref[...])
            o_ref[c] = blk.astype(o_ref.dtype)
    else:
        o_ref[...] = acc.astype(o_ref.dtype)


def _norm_matmul(x, x_col, k, g, w, *, blocked, out_dtype, tm, tn, rope_tabs=None, w_extra=None, name=None):
    m = x.shape[0]
    nc = w.shape[1]
    tm = min(tm, m)
    tn = min(tn, nc)
    assert m % tm == 0 and nc % tn == 0 and tn % LANE == 0
    nblk = tn // LANE
    in_specs = [
        pl.BlockSpec((tm, k), lambda i, j: (i, x_col)),
        pl.BlockSpec((1, k), lambda i, j: (0, 0)),
        pl.BlockSpec((k, tn), lambda i, j: (0, j), pipeline_mode=_resident(nc == tn)),
    ]
    args = [x, g.reshape(1, k).astype(F32), w]
    if rope_tabs is not None:
        in_specs += [pl.BlockSpec((tm, LANE), lambda i, j: (i, 0))] * 3
        args += list(rope_tabs)
    if blocked:
        out_shape = jax.ShapeDtypeStruct((nc // LANE, m, LANE), out_dtype)
        out_spec = pl.BlockSpec((nblk, tm, LANE), lambda i, j: (j, i, 0))
    else:
        out_shape = jax.ShapeDtypeStruct((m, nc), out_dtype)
        out_spec = pl.BlockSpec((tm, tn), lambda i, j: (i, j))
    if w_extra is not None:
        ne = w_extra.shape[1]
        in_specs.append(pl.BlockSpec((k, ne), lambda i, j: (0, 0), pipeline_mode=_resident(True)))
        args.append(w_extra)
        out_spec = [out_spec, pl.BlockSpec((tm, ne), lambda i, j: (i, 0))]
        out_shape = [out_shape, jax.ShapeDtypeStruct((m, ne), F32)]
    return pl.pallas_call(
        functools.partial(_norm_matmul_kernel, rope=rope_tabs is not None, blocked=blocked, nblk=nblk,
                          extra=w_extra is not None),
        grid=(m // tm, nc // tn),
        in_specs=in_specs,
        out_specs=out_spec,
        out_shape=out_shape,
        scratch_shapes=[pltpu.VMEM((tm, k), BF16)],
        compiler_params=_params("parallel", "arbitrary"),
        name=name,
    )(*args)


def _krope_kernel(x_ref, cos_ref, slo_ref, shi_ref, o_ref):
    o_ref[...] = _rope_rotate(x_ref[...], cos_ref[...], slo_ref[...], shi_ref[...]).astype(o_ref.dtype)


def _krope(small, col_blk, rope_tabs, tm):
    m = small.shape[0]
    tm = min(tm, m)
    return pl.pallas_call(
        _krope_kernel,
        grid=(m // tm,),
        in_specs=[pl.BlockSpec((tm, LANE), lambda i: (i, col_blk))] + [pl.BlockSpec((tm, LANE), lambda i: (i, 0))] * 3,
        out_specs=pl.BlockSpec((tm, LANE), lambda i: (i, 0)),
        out_shape=jax.ShapeDtypeStruct((m, LANE), BF16),
        compiler_params=_params("parallel"),
        name="k_rope",
    )(small, *rope_tabs)


N_SPLIT = 3


def _forget_bias_kernel(f_ref, b_ref, o_ref, carry_ref, *, ts, lane0, heads):
    @pl.when(pl.program_id(1) == 0)
    def _():
        carry_ref[...] = jnp.zeros_like(carry_ref)

    x = f_ref[...] + b_ref[...]
    log_f = jnp.minimum(x, 0.0) - jnp.log1p(jnp.exp(-jnp.abs(x)))
    row = lax.broadcasted_iota(jnp.int32, (ts, ts), 0)
    col = lax.broadcasted_iota(jnp.int32, (ts, ts), 1)
    lower = (col <= row).astype(F32)
    c = jnp.dot(lower, log_f, preferred_element_type=F32, precision=lax.Precision.HIGHEST) + carry_ref[0:1, :]
    carry_ref[...] = jnp.broadcast_to(c[ts - 1:ts, :], carry_ref.shape)
    rest = -(c * LOG2E)
    src = lax.broadcasted_iota(jnp.int32, (LANE, LANE), 0) - lane0
    dst = lax.broadcasted_iota(jnp.int32, (LANE, LANE), 1)
    out = jnp.zeros((ts, LANE), F32)
    for pi in range(N_SPLIT):
        piece = rest.astype(BF16)
        rest = rest - piece.astype(F32)
        place = jnp.where((src >= 0) & (src < heads) & (dst == N_SPLIT * src + pi), 1.0, 0.0).astype(BF16)
        out = out + jnp.dot(piece, place, preferred_element_type=F32)
    o_ref[...] = out.astype(o_ref.dtype)


def _forget_bias(small, col_blk, lane0, b_forget, b, s, ts):
    heads = b_forget.shape[0]
    assert N_SPLIT * heads <= LANE
    ts = min(ts, s)
    per_b = s // ts
    b_row = jnp.zeros((1, LANE), F32).at[0, lane0:lane0 + heads].set(b_forget.astype(F32))
    return pl.pallas_call(
        functools.partial(_forget_bias_kernel, ts=ts, lane0=lane0, heads=heads),
        grid=(b, per_b),
        in_specs=[pl.BlockSpec((ts, LANE), lambda bi, j: (bi * per_b + j, col_blk)),
                  pl.BlockSpec((1, LANE), lambda bi, j: (0, 0))],
        out_specs=pl.BlockSpec((ts, LANE), lambda bi, j: (bi * per_b + j, 0)),
        out_shape=jax.ShapeDtypeStruct((b * s, LANE), BF16),
        scratch_shapes=[pltpu.VMEM((8, LANE), F32)],
        compiler_params=_params("parallel", "arbitrary"),
        name="forget_bias",
    )(small, b_row)


def _norm_matmul_t_kernel(x_ref, g_ref, w_ref, o_ref, xn_ref):
    @pl.when(pl.program_id(1) == 0)
    def _():
        x = x_ref[...].astype(F32)
        y = x * lax.rsqrt(jnp.mean(x * x, axis=-1, keepdims=True) + EPS)
        xn_ref[...] = (y * g_ref[...]).astype(BF16)

    o_ref[...] = lax.dot_general(w_ref[...], xn_ref[...], (((1,), (1,)), ((), ())),
                                 preferred_element_type=F32).astype(o_ref.dtype)


def _norm_matmul_t(x, x_col, k, g, w_t, *, tm, tr, name=None):
    m = x.shape[0]
    r = w_t.shape[0]
    tm = min(tm, m)
    tr = min(tr, r)
    return pl.pallas_call(
        _norm_matmul_t_kernel,
        grid=(m // tm, r // tr),
        in_specs=[pl.BlockSpec((tm, k), lambda i, j: (i, x_col)), pl.BlockSpec((1, k), lambda i, j: (0, 0)),
                  pl.BlockSpec((tr, k), lambda i, j: (j, 0))],
        out_specs=pl.BlockSpec((tr, tm), lambda i, j: (j, i)),
        out_shape=jax.ShapeDtypeStruct((r, m), BF16),
        scratch_shapes=[pltpu.VMEM((tm, k), BF16)],
        compiler_params=_params("parallel", "arbitrary"),
        name=name,
    )(x, g.reshape(1, k).astype(F32), w_t)


N_ATTN_SCRATCH = 7


def _attn_kernel(*refs, t, unit, q_ones, group, n_cast):
    n_in = 3 if q_ones else 4
    kb_ref = refs[group * n_in]
    cast_src = refs[group * n_in + 1:group * n_in + 1 + n_cast]
    n_inputs = group * n_in + 1 + n_cast
    o_ref = refs[n_inputs]
    cast_dst = refs[n_inputs + 1:n_inputs + 1 + n_cast]
    scratch = refs[n_inputs + 1 + n_cast:]
    i = pl.program_id(2)
    streams = []
    for g in range(group):
        head_refs = refs[g * n_in:(g + 1) * n_in]
        if q_ones:
            qa_ref, ka_ref, vt_ref = head_refs
            lane = lax.broadcasted_iota(jnp.int32, (t, LANE), 1)
            lo = N_SPLIT * (pl.program_id(1) * group + g)
            q_b = jnp.where((lane >= lo) & (lane < lo + N_SPLIT), 1.0, 0.0).astype(BF16)
        else:
            qa_ref, qb_ref, ka_ref, vt_ref = head_refs
            q_b = qb_ref[...]
        q = jnp.concatenate([qa_ref[...], q_b], axis=1)
        streams.append((q, ka_ref, vt_ref) + tuple(scratch[g * N_ATTN_SCRATCH:(g + 1) * N_ATTN_SCRATCH]))

    def stage(name, j=None):
        for st in streams:
            _attn_stream(st, kb_ref, j, t, unit, name)

    def body(tau, carry):
        stage("values", tau - 2)
        stage("softmax")
        stage("logits", tau)
        return carry

    stage("init")
    stage("logits", 0)
    for src_ref, dst_ref in zip(cast_src, cast_dst):
        dst_ref[...] = src_ref[...].astype(dst_ref.dtype)
    lax.fori_loop(1, i + 1, body, 0)
    stage("values", i - 1)
    stage("softmax_masked")
    stage("values", i)
    for g, st in enumerate(streams):
        o_ref[g] = jnp.transpose(st[5][...] / st[4][...]).astype(o_ref.dtype)


def _attn_stream(st, kb_ref, j, t, unit, stage):
    q, ka_ref, vt_ref, m_scr, l_scr, acc_scr, s_ref, p_ref, a_ref, bmax_ref = st
    if stage == "init":
        m_scr[...] = jnp.full(m_scr.shape, NEG_INF, F32)
        l_scr[...] = jnp.zeros(l_scr.shape, F32)
        acc_scr[...] = jnp.zeros(acc_scr.shape, F32)
        p_ref[...] = jnp.zeros(p_ref.shape, BF16)
        a_ref[...] = jnp.ones(a_ref.shape, F32)
    elif stage == "logits":
        off = pl.multiple_of(j * t, t)
        k = jnp.concatenate([ka_ref[pl.ds(off, t), :], kb_ref[pl.ds(off, t), :]], axis=1)
        s = lax.dot_general(k, q, (((1,), (1,)), ((), ())), preferred_element_type=F32)
        s_ref[...] = s
        bmax_ref[...] = jnp.max(s, axis=0, keepdims=True)
    elif stage == "values":
        off = pl.multiple_of(jnp.maximum(j, 0) * t, t)
        pv = jnp.dot(vt_ref[:, pl.ds(off, t)], p_ref[...], preferred_element_type=F32)
        acc_scr[...] = a_ref[...] * acc_scr[...] + pv
    else:
        masked = stage == "softmax_masked"
        s = s_ref[...]
        if masked:
            key = lax.broadcasted_iota(jnp.int32, (t, t), 0)
            qry = lax.broadcasted_iota(jnp.int32, (t, t), 1)
            s = jnp.where((key // unit) <= (qry // unit), s, NEG_INF)
            block_max = jnp.max(s, axis=0, keepdims=True)
        else:
            block_max = bmax_ref[...]
        m_prev = m_scr[...]
        m_new = jnp.maximum(m_prev, block_max)
        alpha = jnp.exp2(m_prev - m_new)
        p = jnp.exp2(s - m_new)
        l_scr[...] = alpha * l_scr[...] + jnp.sum(p, axis=0, keepdims=True)
        m_scr[...] = m_new
        p_ref[...] = p.astype(BF16)
        a_ref[...] = alpha


def _attn_steps(heads, b, s, t):
    return b * (heads // ATTN_HEAD_GROUP) * (s // min(t, s))


def _cast_chunks(steps, n_slices, row_counts):
    chunks, rem = divmod(steps, n_slices)
    ok = rem == 0 and chunks > 0 and all(r % (chunks * BF16_SUBLANES) == 0 for r in row_counts)
    return chunks if ok else None


def _causal_attention(q_a, qa_blk, q_b, qb_blk, k_a, ka_blk, k_b, v_t, heads, b, s, t, unit, name, side_cast=None):
    t = min(t, s)
    group = ATTN_HEAD_GROUP
    assert t % unit == 0 and heads % group == 0
    nq = s // t
    once = pl.Buffered(1)
    in_specs, args = [], []
    for g in range(group):
        def q_spec(blk, g=g):
            return pl.BlockSpec((None, t, LANE), lambda bi, hi, i: (blk(hi * group + g), bi * nq + i, 0))

        in_specs.append(q_spec(qa_blk))
        args.append(q_a)
        if q_b is not None:
            in_specs.append(q_spec(qb_blk))
            args.append(q_b)
        in_specs.append(pl.BlockSpec((None, s, LANE), lambda bi, hi, i, g=g: (ka_blk(hi * group + g), bi, 0),
                                     pipeline_mode=once))
        in_specs.append(pl.BlockSpec((LANE, s), lambda bi, hi, i, g=g: (hi * group + g, bi), pipeline_mode=once))
        args += [k_a, v_t]
    in_specs.append(pl.BlockSpec((s, LANE), lambda bi, hi, i: (bi, 0), pipeline_mode=once))
    args.append(k_b)
    stat = pltpu.VMEM((1, t), F32)
    per_head_scratch = [stat, stat, pltpu.VMEM((LANE, t), F32), pltpu.VMEM((t, t), F32),
                        pltpu.VMEM((t, t), BF16), stat, stat]
    assert len(per_head_scratch) == N_ATTN_SCRATCH
    out_specs = [pl.BlockSpec((group, t, LANE), lambda bi, hi, i: (hi, bi * nq + i, 0))]
    out_shape = [jax.ShapeDtypeStruct((heads, b * s, LANE), BF16)]
    n_cast = 0
    if side_cast is not None:
        layer, chunks, items = side_cast
        n_cast = len(items)

        def step_of(bi, hi, i):
            return (bi * (heads // group) + hi) * nq + i

        for src, (rows, cols), (row_half, col_half) in items:
            rb = rows // chunks
            in_specs.append(pl.BlockSpec(
                (None, None, rb, cols),
                lambda bi, hi, i, rh=row_half, ch=col_half: (layer, step_of(bi, hi, i) // chunks,
                                                             rh * chunks + step_of(bi, hi, i) % chunks, ch)))
            args.append(src)
            out_specs.append(pl.BlockSpec(
                (None, rb, cols), lambda bi, hi, i: (step_of(bi, hi, i) // chunks, step_of(bi, hi, i) % chunks, 0)))
            out_shape.append(jax.ShapeDtypeStruct((src.shape[1], rows, cols), BF16))
    return pl.pallas_call(
        functools.partial(_attn_kernel, t=t, unit=unit, q_ones=q_b is None, group=group, n_cast=n_cast),
        grid=(b, heads // group, nq),
        in_specs=in_specs,
        out_specs=out_specs,
        out_shape=out_shape,
        scratch_shapes=per_head_scratch * group,
        compiler_params=_params("parallel", "parallel", "arbitrary"),
        name=name,
    )(*args)


def _proj_residual_kernel(*refs, merge, nb):
    if merge:
        oa_ref, ob_ref, ga_ref, gb_ref, w_ref, res_ref, o_ref, a_scr = refs
    else:
        oa_ref, w_ref, res_ref, o_ref, a_scr = refs

    @pl.when(pl.program_id(1) == 0)
    def _():
        for c in range(nb):
            if merge:
                a = (jax.nn.sigmoid(ga_ref[c].astype(F32)) * oa_ref[c].astype(F32)
                     + jax.nn.sigmoid(gb_ref[c].astype(F32)) * ob_ref[c].astype(F32))
            else:
                a = oa_ref[c]
            a_scr[:, c * LANE:(c + 1) * LANE] = a.astype(BF16)

    o_ref[...] = res_ref[...] + jnp.dot(a_scr[...], w_ref[...], preferred_element_type=F32)


def _proj_residual(o_a, w, res, *, tm, tn, merge=None, name=None):
    nb, m, _ = o_a.shape
    k = nb * LANE
    nc = w.shape[1]
    tm = min(tm, m)
    tn = min(tn, nc)
    head_spec = pl.BlockSpec((nb, tm, LANE), lambda i, j: (0, i, 0))
    in_specs = [head_spec]
    args = [o_a]
    if merge is not None:
        o_b, gates, ga_grp, gb_grp = merge
        in_specs += [head_spec,
                     pl.BlockSpec((nb, tm, LANE), lambda i, j: (ga_grp, i, 0)),
                     pl.BlockSpec((nb, tm, LANE), lambda i, j: (gb_grp, i, 0))]
        args += [o_b, gates, gates]
    in_specs += [pl.BlockSpec((k, tn), lambda i, j: (0, j), pipeline_mode=_resident(nc == tn)),
                 pl.BlockSpec((tm, tn), lambda i, j: (i, j))]
    args += [w, res]
    return pl.pallas_call(
        functools.partial(_proj_residual_kernel, merge=merge is not None, nb=nb),
        grid=(m // tm, nc // tn),
        in_specs=in_specs,
        out_specs=pl.BlockSpec((tm, tn), lambda i, j: (i, j)),
        out_shape=jax.ShapeDtypeStruct((m, nc), F32),
        scratch_shapes=[pltpu.VMEM((tm, k), BF16)],
        compiler_params=_params("parallel", "arbitrary"),
        name=name,
    )(*args)


def _cross_kernel(h_ref, g_ref, wq_ref, mk_ref, mv_ref, wo_ref, o_ref, *, bph):
    x = h_ref[...]
    xn = (x * lax.rsqrt(jnp.mean(x * x, axis=-1, keepdims=True) + EPS) * g_ref[...]).astype(BF16)
    q_all = jnp.dot(xn, wq_ref[...], preferred_element_type=F32).astype(BF16)
    hd = bph * LANE
    heads_out = []
    for hh in range(MEM_HEADS):
        blocks = range(hh * bph, (hh + 1) * bph)
        q = q_all[:, hh * hd:(hh + 1) * hd]
        k = jnp.concatenate([mk_ref[c] for c in blocks], axis=1)
        v = jnp.concatenate([mv_ref[c] for c in blocks], axis=1)
        s = lax.dot_general(q, k, (((1,), (1,)), ((), ())), preferred_element_type=F32)
        p = jnp.exp2(s - jnp.max(s, axis=1, keepdims=True))
        p = p / jnp.sum(p, axis=1, keepdims=True)
        heads_out.append(jnp.dot(p.astype(BF16), v, preferred_element_type=F32).astype(BF16))
    o = jnp.concatenate(heads_out, axis=1)
    o_ref[...] = x + jnp.dot(o, wo_ref[...], preferred_element_type=F32)


def _cross_attention_layer(h, g, wq, mkv, wo, b, s, n_mem, tm):
    n, d = h.shape
    nb = d // LANE
    tm = min(tm, s)
    per_b = s // tm
    whole = pl.Buffered(1)
    return pl.pallas_call(
        functools.partial(_cross_kernel, bph=nb // MEM_HEADS),
        grid=(n // tm,),
        in_specs=[
            pl.BlockSpec((tm, d), lambda i: (i, 0)),
            pl.BlockSpec((1, d), lambda i: (0, 0)),
            pl.BlockSpec((d, d), lambda i: (0, 0), pipeline_mode=whole),
            pl.BlockSpec((nb, n_mem, LANE), lambda i: (0, i // per_b, 0)),
            pl.BlockSpec((nb, n_mem, LANE), lambda i: (1, i // per_b, 0)),
            pl.BlockSpec((d, d), lambda i: (0, 0), pipeline_mode=whole),
        ],
        out_specs=pl.BlockSpec((tm, d), lambda i: (i, 0)),
        out_shape=jax.ShapeDtypeStruct((n, d), F32),
        compiler_params=_params("parallel"),
        name="cross_attention_layer",
    )(h, g.reshape(1, d).astype(F32), wq, mkv, mkv, wo)


def _router_kernel(h_ref, g_ref, w_ref, b_ref, info_ref, count_ref):
    x = h_ref[...]
    z = x * lax.rsqrt(jnp.mean(x * x, axis=-1, keepdims=True) + EPS) * g_ref[...]
    logits = jnp.dot(z, w_ref[...], preferred_element_type=F32, precision=lax.Precision.HIGHEST) + b_ref[...]
    lane = lax.broadcasted_iota(jnp.int32, logits.shape, 1).astype(F32)
    big = float(LANE)

    def first_where(mask):
        return jnp.min(jnp.where(mask, lane, big), axis=1, keepdims=True)

    gmask = lane < N_GROUPS
    gl = jnp.where(gmask, logits, NEG_INF)
    gmax = jnp.max(gl, axis=1, keepdims=True)
    gsum = jnp.sum(jnp.exp(gl - gmax), axis=1, keepdims=True)
    g_idx = first_where(gl == gmax)
    g_w = 1.0 / gsum
    lo = N_GROUPS + g_idx * EXPERTS_PER_GROUP
    emask = (lane >= lo) & (lane < lo + EXPERTS_PER_GROUP)
    el = jnp.where(emask, logits, NEG_INF)
    emax = jnp.max(el, axis=1, keepdims=True)
    eexp = jnp.exp(el - emax)
    prob = jnp.where(emask, eexp / jnp.sum(eexp, axis=1, keepdims=True), -1.0)
    p1 = jnp.max(prob, axis=1, keepdims=True)
    i1 = first_where(prob == p1)
    prob2 = jnp.where(lane == i1, -1.0, prob)
    p2 = jnp.max(prob2, axis=1, keepdims=True)
    i2 = first_where(prob2 == p2)
    denom = p1 + p2
    w1 = g_w * p1 / denom
    w2 = g_w * p2 / denom
    @pl.when(pl.program_id(0) == 0)
    def _():
        count_ref[...] = jnp.zeros_like(count_ref)

    tm = logits.shape[0]
    hot1 = jnp.where(lane == i1, 1.0, 0.0)
    hot2 = jnp.where(lane == i2, 1.0, 0.0)
    hot = hot1 + hot2
    row = lax.broadcasted_iota(jnp.int32, (tm, tm), 0)
    col = lax.broadcasted_iota(jnp.int32, (tm, tm), 1)
    before = jnp.where(col < row, 1.0, 0.0).astype(BF16)
    prior = jnp.dot(before, hot.astype(BF16), preferred_element_type=F32) + count_ref[0:1, :]
    r1 = jnp.sum(prior * hot1, axis=1, keepdims=True)
    r2 = jnp.sum(prior * hot2, axis=1, keepdims=True)
    count_ref[...] = count_ref[...] + jnp.sum(hot, axis=0, keepdims=True)
    info = jnp.zeros_like(logits)
    for k, val in enumerate((i1 - N_GROUPS, i2 - N_GROUPS, w1, w2, r1, r2)):
        info = jnp.where(lane == k, val, info)
    info_ref[...] = info


def _router(h, g, w_r, b_r, tm):
    n, d = h.shape
    tm = min(tm, n)
    return pl.pallas_call(
        _router_kernel,
        grid=(n // tm,),
        in_specs=[pl.BlockSpec((tm, d), lambda i: (i, 0)), pl.BlockSpec((1, d), lambda i: (0, 0)),
                  pl.BlockSpec((d, LANE), lambda i: (0, 0)), pl.BlockSpec((1, LANE), lambda i: (0, 0))],
        out_specs=[pl.BlockSpec((tm, LANE), lambda i: (i, 0)), pl.BlockSpec((8, LANE), lambda i: (0, 0))],
        out_shape=[jax.ShapeDtypeStruct((n, LANE), F32), jax.ShapeDtypeStruct((8, LANE), F32)],
        compiler_params=_params("arbitrary"),
        name="moe_router",
    )(h, g.reshape(1, d).astype(F32), w_r, b_r)


def _row_copy(src_hbm, row, dst, dst_row, sem):
    return pltpu.make_async_copy(src_hbm.at[pl.ds(row, 1)], dst.at[pl.ds(dst_row, 1)], sem)


def _issue_rows(idx_ref, idx_off, src_hbm, dst, sem, n_rows):
    def issue(r, carry):
        _row_copy(src_hbm, idx_ref[0, idx_off + r], dst, r, sem).start()
        return carry

    lax.fori_loop(0, n_rows, issue, 0, unroll=GATHER_UNROLL)


def _wait_rows(src_hbm, dst, sem, n_rows):
    def wait(r, carry):
        _row_copy(src_hbm, 0, dst, r, sem).wait()
        return carry

    lax.fori_loop(0, n_rows, wait, 0, unroll=GATHER_UNROLL)


def _expert_kernel(blk_e_ref, nvalid_ref, tok0_ref, tok_next_ref, z_hbm, g_ref, wg_lo, wu_lo, wd_lo, wg_hi, wu_hi,
                   wd_hi, y_ref, xbuf, sem, *, tb):
    i = pl.program_id(0)
    nvalid = nvalid_ref[0]
    slot = i % 2

    @pl.when((i == 0) & (nvalid > 0))
    def _():
        _issue_rows(tok0_ref, 0, z_hbm, xbuf.at[0], sem.at[0], tb)

    @pl.when(i + 1 < nvalid)
    def _():
        _issue_rows(tok_next_ref, 0, z_hbm, xbuf.at[1 - slot], sem.at[1 - slot], tb)

    @pl.when(i < nvalid)
    def _():
        _wait_rows(z_hbm, xbuf.at[slot], sem.at[slot], tb)
        rows = xbuf[slot]
        x = (rows * lax.rsqrt(jnp.mean(rows * rows, axis=-1, keepdims=True) + EPS) * g_ref[...]).astype(BF16)
        half = x.shape[1] // 2
        x_lo, x_hi = x[:, :half], x[:, half:]
        gate = (jnp.dot(x_lo, wg_lo[...], preferred_element_type=F32)
                + jnp.dot(x_hi, wg_hi[...], preferred_element_type=F32))
        up = (jnp.dot(x_lo, wu_lo[...], preferred_element_type=F32)
              + jnp.dot(x_hi, wu_hi[...], preferred_element_type=F32))
        hidden = (gate * jax.nn.sigmoid(gate) * up).astype(BF16)
        y_ref[:, :half] = jnp.dot(hidden, wd_lo[...], preferred_element_type=F32)
        y_ref[:, half:] = jnp.dot(hidden, wd_hi[...], preferred_element_type=F32)

    @pl.when(i >= nvalid)
    def _():
        y_ref[...] = jnp.zeros_like(y_ref)


def _expert_blocks(blk_e, nvalid, tok, z, g, w_lo, w_hi, tb):
    nblk = tok.shape[0]
    d = z.shape[1]
    ff = w_lo[0].shape[2]
    rows_half = pl.BlockSpec((None, d // 2, ff), lambda i, be, nv: (be[i], 0, 0))
    cols_half = pl.BlockSpec((None, ff, d // 2), lambda i, be, nv: (be[i], 0, 0))
    grid_spec = pltpu.PrefetchScalarGridSpec(
        num_scalar_prefetch=2,
        grid=(nblk,),
        in_specs=[
            pl.BlockSpec((None, 1, tb), lambda i, be, nv: (0, 0, 0), memory_space=pltpu.SMEM),
            pl.BlockSpec((None, 1, tb), lambda i, be, nv: (jnp.minimum(i + 1, nblk - 1), 0, 0),
                         memory_space=pltpu.SMEM),
            pl.BlockSpec(memory_space=pl.ANY),
            pl.BlockSpec((1, d), lambda i, be, nv: (0, 0)),
            rows_half, rows_half, cols_half, rows_half, rows_half, cols_half,
        ],
        out_specs=pl.BlockSpec((tb, d), lambda i, be, nv: (i, 0)),
        scratch_shapes=[pltpu.VMEM((2, tb, d), F32), pltpu.SemaphoreType.DMA((2,))],
    )
    return pl.pallas_call(
        functools.partial(_expert_kernel, tb=tb),
        grid_spec=grid_spec,
        out_shape=jax.ShapeDtypeStruct((nblk * tb, d), F32),
        compiler_params=_params("arbitrary"),
        name="moe_experts",
    )(blk_e, nvalid, tok, tok, z, g.reshape(1, d).astype(F32), *w_lo, *w_hi)


def _combine_kernel(dest0_ref, dest_next_ref, y_hbm, h_ref, info_ref, g_ref, o_ref, ybuf, sem, *, tm, final_norm):
    i = pl.program_id(0)
    slot = i % 2

    def issue(idx_ref, sl):
        for k in range(2):
            _issue_rows(idx_ref, k * tm, y_hbm, ybuf.at[sl, k], sem.at[sl], tm)

    @pl.when(i == 0)
    def _():
        issue(dest0_ref, 0)

    @pl.when(i + 1 < pl.num_programs(0))
    def _():
        issue(dest_next_ref, 1 - slot)

    for k in range(2):
        _wait_rows(y_hbm, ybuf.at[slot, k], sem.at[slot], tm)
    info = info_ref[...]
    moe = ybuf[slot, 0] * info[:, 2:3] + ybuf[slot, 1] * info[:, 3:4]
    hx = h_ref[...] + moe
    if final_norm:
        hx = hx * lax.rsqrt(jnp.mean(hx * hx, axis=-1, keepdims=True) + EPS) * g_ref[...]
    o_ref[...] = hx


def _combine(dest, y, h, info, g_final, tm, final_norm):
    n, d = h.shape
    n_tiles = n // tm
    return pl.pallas_call(
        functools.partial(_combine_kernel, tm=tm, final_norm=final_norm),
        grid=(n_tiles,),
        in_specs=[
            pl.BlockSpec((None, 1, 2 * tm), lambda i: (0, 0, 0), memory_space=pltpu.SMEM),
            pl.BlockSpec((None, 1, 2 * tm), lambda i: (jnp.minimum(i + 1, n_tiles - 1), 0, 0),
                         memory_space=pltpu.SMEM),
            pl.BlockSpec(memory_space=pl.ANY),
            pl.BlockSpec((tm, d), lambda i: (i, 0)),
            pl.BlockSpec((tm, LANE), lambda i: (i, 0)),
            pl.BlockSpec((1, d), lambda i: (0, 0)),
        ],
        out_specs=pl.BlockSpec((tm, d), lambda i: (i, 0)),
        out_shape=jax.ShapeDtypeStruct((n, d), F32),
        scratch_shapes=[pltpu.VMEM((2, 2, tm, d), F32), pltpu.SemaphoreType.DMA((2,))],
        compiler_params=_params("arbitrary"),
        name="moe_combine_final_norm",
    )(dest, dest, y, h, info, g_final.reshape(1, d).astype(F32))


def _dispatch_plan(eid, rank, counts, n, tb):
    n_slots = 2 * n
    nblk = n_slots // tb + N_EXPERTS
    slot_e = eid.reshape(-1)
    rank = rank.reshape(-1)
    blocks_per_e = (counts + tb - 1) // tb
    blk_end = jnp.cumsum(blocks_per_e)
    blk_start = blk_end - blocks_per_e
    dest = blk_start[slot_e] * tb + rank
    slot_tok = jnp.arange(n_slots, dtype=jnp.int32) // 2
    tok = jnp.zeros((nblk * tb,), jnp.int32).at[dest].set(slot_tok, unique_indices=True, mode="promise_in_bounds")
    blk_ids = jnp.arange(nblk, dtype=jnp.int32)
    blk_e = jnp.minimum(jnp.sum(blk_end[None, :] <= blk_ids[:, None], axis=1), N_EXPERTS - 1).astype(jnp.int32)
    nvalid = blk_end[-1:].astype(jnp.int32)
    return blk_e, nvalid, tok.reshape(nblk, 1, tb), dest.reshape(n, 2)


def _rope_tables(positions):
    half = MLA_ROPE // 2
    inv_freq = ROPE_THETA ** (-jnp.arange(half, dtype=F32) / half)
    ang = positions.astype(F32).reshape(-1, 1) * inv_freq[None, :]
    cos, sin = jnp.cos(ang), jnp.sin(ang)
    zero = jnp.zeros_like(cos)
    cos_t = jnp.concatenate([cos, cos, zero, zero], axis=1)
    sin_lo = jnp.concatenate([-sin, zero, zero, zero], axis=1)
    sin_hi = jnp.concatenate([zero, sin, zero, zero], axis=1)
    return cos_t, sin_lo, sin_hi


def kernel(x, mem, positions, g_mix, w_in, b_forget, g_q_latent, g_kv_latent, w_q_up, w_k_up, w_v_up, w_out, g_cross, g_mem, w_cross_q, w_cross_kv, w_cross_out, g_moe, w_group_router, b_group_router, w_expert_router, b_expert_router, w_exp_gate, w_exp_up, w_exp_down, g_final):
    b, s, d = x.shape
    n = b * s
    n_mem = mem.shape[1]
    depth = w_in.shape[0]
    fw = FOX_HEADS * LANE
    h = x.reshape(n, d)
    rope_tabs = _rope_tables(positions)

    for l in range(depth):
        wi = w_in[l]
        fox_scale = (LANE ** -0.5) * LOG2E
        o_f = 3 * fw
        o_cq = o_f + FOX_HEADS
        o_kr = o_cq + 2 * MLA_RANK
        o_g = o_kr + MLA_ROPE
        w_big = jnp.concatenate([wi[:, :fw] * fox_scale, wi[:, fw:2 * fw], wi[:, o_g:]], axis=1).astype(BF16)
        w_fox_v_t = wi[:, 2 * fw:3 * fw].T.astype(BF16)
        w_small = jnp.concatenate(
            [wi[:, o_cq:o_kr], wi[:, o_kr:o_g], wi[:, o_f:o_cq],
             jnp.zeros((d, LANE - MLA_ROPE - FOX_HEADS), F32)], axis=1).astype(BF16)
        mla_scale = ((MLA_NOPE + MLA_ROPE) ** -0.5) * LOG2E
        wq = (w_q_up[l] * mla_scale).reshape(MLA_RANK, MLA_HEADS, MLA_NOPE + MLA_ROPE)
        wq = jnp.pad(wq, ((0, 0), (0, 0), (0, 2 * LANE - MLA_NOPE - MLA_ROPE))).reshape(MLA_RANK, -1).astype(BF16)
        wk = w_k_up[l].astype(BF16)
        w_mla_v_t = w_v_up[l].T.astype(BF16)

        big, small = _norm_matmul(h, 0, d, g_mix[l], w_big, blocked=True, out_dtype=BF16, tm=PROJ_TM, tn=PROJ_TN,
                                  w_extra=w_small, name="in_proj")
        nh = FOX_HEADS
        fox_v_t = _norm_matmul_t(h, 0, d, g_mix[l], w_fox_v_t, tm=PROJ_TM, tr=PROJ_TN, name="fox_v_t")
        bias = _forget_bias(small, 2 * MLA_RANK // LANE, MLA_ROPE, b_forget[l], b, s, 512)
        ff = w_exp_gate.shape[3]
        steps = _attn_steps(nh, b, s, ATTN_TILE)
        chunks = _cast_chunks(steps, N_EXPERTS, (d // 2, ff))
        if steps != _attn_steps(MLA_HEADS, b, s, ATTN_TILE):
            chunks = None

        def cast_half(half):
            if chunks is None:
                return None
            return (l, chunks, [(w_exp_gate, (d // 2, ff), (half, 0)), (w_exp_up, (d // 2, ff), (half, 0)),
                                (w_exp_down, (ff, d // 2), (0, half))])

        o_a, *expert_lo = _causal_attention(big, lambda hd: hd, None, None, big, lambda hd: nh + hd, bias,
                                            fox_v_t, nh, b, s, ATTN_TILE, 1, "fox_attention", cast_half(0))
        qm = _norm_matmul(small, 0, MLA_RANK, g_q_latent[l], wq, blocked=True, out_dtype=BF16, tm=PROJ_TM,
                          tn=PROJ_TN, rope_tabs=rope_tabs, name="mla_q_up")
        k_nope = _norm_matmul(small, 1, MLA_RANK, g_kv_latent[l], wk, blocked=True, out_dtype=BF16, tm=PROJ_TM,
                              tn=PROJ_TN, name="mla_k_up")
        mla_v_t = _norm_matmul_t(small, 1, MLA_RANK, g_kv_latent[l], w_mla_v_t, tm=PROJ_TM, tr=PROJ_TN,
                                 name="mla_v_t")
        k_rope = _krope(small, 2 * MLA_RANK // LANE, rope_tabs, 512)
        o_b, *expert_hi = _causal_attention(qm, lambda hd: 2 * hd, qm, lambda hd: 2 * hd + 1, k_nope, lambda hd: hd,
                                            k_rope, mla_v_t, MLA_HEADS, b, s, ATTN_TILE, CHUNK, "mla_attention",
                                            cast_half(1))
        if chunks is None:
            hd2 = d // 2
            expert_lo = [w_exp_gate[l, :, :hd2].astype(BF16), w_exp_up[l, :, :hd2].astype(BF16),
                         w_exp_down[l, :, :, :hd2].astype(BF16)]
            expert_hi = [w_exp_gate[l, :, hd2:].astype(BF16), w_exp_up[l, :, hd2:].astype(BF16),
                         w_exp_down[l, :, :, hd2:].astype(BF16)]
        h = _proj_residual(o_a, w_out[l].astype(BF16), h, tm=MERGE_TM, tn=d, merge=(o_b, big, 2, 3),
                           name="out_proj")
        cross_scale = ((d // MEM_HEADS) ** -0.5) * LOG2E
        mkv = _norm_matmul(mem.reshape(b * n_mem, d), 0, d, g_mem[l], w_cross_kv[l].astype(BF16), blocked=True,
                           out_dtype=BF16, tm=PROJ_TM, tn=PROJ_TN, name="cross_mem_kv")
        h = _cross_attention_layer(h, g_cross[l], (w_cross_q[l] * cross_scale).astype(BF16), mkv,
                                   w_cross_out[l].astype(BF16), b, s, n_mem, 512)
        w_r = jnp.concatenate([w_group_router[l], w_expert_router[l],
                               jnp.zeros((d, LANE - N_GROUPS - N_EXPERTS), F32)], axis=1)
        b_r = jnp.concatenate([b_group_router[l], b_expert_router[l],
                               jnp.zeros((LANE - N_GROUPS - N_EXPERTS,), F32)]).reshape(1, LANE)
        info, counts = _router(h, g_moe[l], w_r, b_r, 512)
        tb = 256
        eid = info[:, 0:2].astype(jnp.int32)
        rank = info[:, 4:6].astype(jnp.int32)
        counts = counts[0, N_GROUPS:N_GROUPS + N_EXPERTS].astype(jnp.int32)
        blk_e, nvalid, tok, dest = _dispatch_plan(eid, rank, counts, n, tb)
        y = _expert_blocks(blk_e, nvalid, tok, h, g_moe[l], expert_lo, expert_hi, tb)
        tm_c = min(256, n)
        dest_blk = dest.reshape(n // tm_c, tm_c, 2).transpose(0, 2, 1).reshape(n // tm_c, 1, 2 * tm_c)
        h = _combine(dest_blk, y, h, info, g_final, tm_c, final_norm=(l + 1 == depth))
    return h.reshape(b, s, d)
```
